```python
import jax, jax.numpy as jnp
from jax import lax
import numpy as np

D_MODEL = 1024
BATCH = 2
SEQ = 16384
DEPTH = 2

MEM_LEN = 256
HEAD_DIM = 64
LRU_WIDTH = 256
LRU_BLOCKS = 4
LRU_BLOCK = LRU_WIDTH // LRU_BLOCKS
CONV_WIDTH = 4
LRU_C = 8.0
FOX_HEADS = 8
FOX_WIDTH = FOX_HEADS * HEAD_DIM
RET_HEADS = 4
RET_WIDTH = RET_HEADS * HEAD_DIM
MIX_WIDTH = LRU_WIDTH + FOX_WIDTH + RET_WIDTH
SPLIT_SIZES = (LRU_WIDTH, LRU_WIDTH, FOX_WIDTH, FOX_WIDTH, FOX_WIDTH, FOX_HEADS, RET_WIDTH, RET_WIDTH, RET_WIDTH, RET_WIDTH)
IN_WIDTH = sum(SPLIT_SIZES)
CROSS_HEADS = 4
CROSS_HEAD_DIM = D_MODEL // CROSS_HEADS
D_FF = ((8 * D_MODEL + 3 * 256 - 1) // (3 * 256)) * 256
Q_BLOCK = 128
RET_CHUNK = 128
ROPE_THETA = 10000.0
EPS = 1e-6

kernel_name = 'hymba_style_lru_fox_retention_hybrid'


def rms_norm(x, g):
    xf = x.astype(jnp.float32)
    y = xf * lax.rsqrt(jnp.mean(xf * xf, axis=-1, keepdims=True) + EPS)
    return (y * g.astype(jnp.float32)).astype(x.dtype)


def causal_depthwise_conv(x, w, b):
    c = x.shape[-1]
    xp = jnp.pad(x, ((0, 0), (CONV_WIDTH - 1, 0), (0, 0)))
    out = lax.conv_general_dilated(xp, w.astype(x.dtype)[:, None, :], window_strides=(1,), padding='VALID', dimension_numbers=('NWC', 'WIO', 'NWC'), feature_group_count=c)
    return out + b.astype(x.dtype)


def rg_lru(x, w_r, b_r, w_i, b_i, lam):
    bsz, s, _ = x.shape
    xb = x.reshape(bsz, s, LRU_BLOCKS, LRU_BLOCK)
    r = jax.nn.sigmoid((jnp.einsum('bsni,nij->bsnj', xb, w_r).reshape(bsz, s, LRU_WIDTH) + b_r).astype(jnp.float32))
    i = jax.nn.sigmoid((jnp.einsum('bsni,nij->bsnj', xb, w_i).reshape(bsz, s, LRU_WIDTH) + b_i).astype(jnp.float32))
    log_a = -LRU_C * r * jax.nn.softplus(-lam.astype(jnp.float32))
    a = jnp.exp(log_a)
    u = jnp.sqrt(-jnp.expm1(2.0 * log_a)) * (i * x.astype(jnp.float32))

    def combine(left, right):
        a1, b1 = left
        a2, b2 = right
        return a1 * a2, a2 * b1 + b2

    _, h = lax.associative_scan(combine, (a, u), axis=1)
    return h.astype(x.dtype)


def forgetting_attention(q, k, v, f_logit, b_f):
    bsz, s, _ = q.shape
    q = q.reshape(bsz, s, FOX_HEADS, HEAD_DIM).transpose(0, 2, 1, 3)
    k = k.reshape(bsz, s, FOX_HEADS, HEAD_DIM).transpose(0, 2, 1, 3)
    v = v.reshape(bsz, s, FOX_HEADS, HEAD_DIM).transpose(0, 2, 1, 3)
    log_f = jax.nn.log_sigmoid((f_logit + b_f).astype(jnp.float32))
    c = jnp.cumsum(log_f, axis=1).transpose(0, 2, 1)
    key_pos = jnp.arange(s)
    scale = HEAD_DIM ** -0.5

    def block(bi):
        start = bi * Q_BLOCK
        qb = lax.dynamic_slice_in_dim(q, start, Q_BLOCK, axis=2)
        cb = lax.dynamic_slice_in_dim(c, start, Q_BLOCK, axis=2)
        logits = jnp.einsum('bhqd,bhkd->bhqk', qb, k).astype(jnp.float32) * scale + cb[..., None] - c[:, :, None, :]
        q_pos = start + jnp.arange(Q_BLOCK)
        mask = key_pos[None, :] <= q_pos[:, None]
        p = jax.nn.softmax(jnp.where(mask, logits, -jnp.inf), axis=-1)
        return jnp.einsum('bhqk,bhkd->bhqd', p.astype(v.dtype), v)

    out = lax.map(block, jnp.arange(s // Q_BLOCK))
    return out.transpose(1, 0, 3, 2, 4).reshape(bsz, s, FOX_WIDTH)


def rotary(x, positions):
    half = x.shape[-1] // 2
    inv_freq = ROPE_THETA ** (-jnp.arange(half, dtype=jnp.float32) / half)
    ang = positions.astype(jnp.float32)[..., None] * inv_freq
    cos = jnp.cos(ang)[:, :, None, :]
    sin = jnp.sin(ang)[:, :, None, :]
    xf = x.astype(jnp.float32)
    x1, x2 = xf[..., :half], xf[..., half:]
    return jnp.concatenate([x1 * cos - x2 * sin, x1 * sin + x2 * cos], axis=-1)


def retention(q, k, v, g, positions):
    bsz, s, _ = q.shape
    n_chunks = s // RET_CHUNK
    shp = (bsz, n_chunks, RET_CHUNK, RET_HEADS, HEAD_DIM)
    qc = rotary(q.reshape(bsz, s, RET_HEADS, HEAD_DIM), positions).reshape(shp)
    kc = (rotary(k.reshape(bsz, s, RET_HEADS, HEAD_DIM), positions) * HEAD_DIM ** -0.5).reshape(shp)
    vc = v.astype(jnp.float32).reshape(shp)
    log_gamma = jnp.log1p(-jnp.exp2(-5.0 - jnp.arange(RET_HEADS, dtype=jnp.float32)))
    idx = jnp.arange(RET_CHUNK, dtype=jnp.float32)
    diff = idx[:, None] - idx[None, :]
    decay = jnp.where(diff >= 0, jnp.exp(log_gamma[:, None, None] * jnp.maximum(diff, 0.0)), 0.0)
    scores = jnp.einsum('bnihd,bnjhd->bnhij', qc, kc) * decay[None, None]
    y_inner = jnp.einsum('bnhij,bnjhd->bnihd', scores, vc)
    k_w = jnp.exp(log_gamma[None, :] * (RET_CHUNK - 1.0 - idx)[:, None])
    u = jnp.einsum('bnjhd,bnjhe,jh->bnhde', kc, vc, k_w)
    chunk_decay = jnp.exp(log_gamma * RET_CHUNK)[None, :, None, None]

    def step(state, u_n):
        return chunk_decay * state + u_n, state

    init = jnp.zeros((bsz, RET_HEADS, HEAD_DIM, HEAD_DIM), jnp.float32)
    _, prev_states = lax.scan(step, init, u.transpose(1, 0, 2, 3, 4))
    q_w = jnp.exp(log_gamma[None, :] * (idx + 1.0)[:, None])
    y_cross = jnp.einsum('bnihd,nbhde,ih->bnihe', qc, prev_states, q_w)
    y = (y_inner + y_cross).reshape(bsz, s, RET_HEADS, HEAD_DIM)
    mu = jnp.mean(y, axis=-1, keepdims=True)
    var = jnp.mean(jnp.square(y - mu), axis=-1, keepdims=True)
    y = ((y - mu) * lax.rsqrt(var + EPS)).reshape(bsz, s, RET_WIDTH)
    return (jax.nn.silu(g.astype(jnp.float32)) * y).astype(g.dtype)


def parallel_mixer(h, positions, w_in, conv_w, conv_b, w_rg, b_rg, w_ig, b_ig, lru_lambda, fox_b_f, w_out):
    proj = h @ w_in
    lx, ly, fq, fk, fv, ff, rq, rk, rv, rg = jnp.split(proj, np.cumsum(SPLIT_SIZES)[:-1].tolist(), axis=-1)
    lru_out = rg_lru(causal_depthwise_conv(lx, conv_w, conv_b), w_rg, b_rg, w_ig, b_ig, lru_lambda) * jax.nn.gelu(ly)
    fox_out = forgetting_attention(fq, fk, fv, ff, fox_b_f)
    ret_out = retention(rq, rk, rv, rg, positions)
    return jnp.concatenate([lru_out, fox_out.astype(h.dtype), ret_out], axis=-1) @ w_out


def memory_cross_attention(h, mem_n, w_q, w_k, w_v, w_o):
    bsz, s, _ = h.shape
    m = mem_n.shape[1]
    q = (h @ w_q).reshape(bsz, s, CROSS_HEADS, CROSS_HEAD_DIM)
    k = (mem_n @ w_k).reshape(bsz, m, CROSS_HEADS, CROSS_HEAD_DIM)
    v = (mem_n @ w_v).reshape(bsz, m, CROSS_HEADS, CROSS_HEAD_DIM)
    logits = jnp.einsum('bshd,bmhd->bhsm', q, k).astype(jnp.float32) * CROSS_HEAD_DIM ** -0.5
    p = jax.nn.softmax(logits, axis=-1)
    o = jnp.einsum('bhsm,bmhd->bshd', p.astype(v.dtype), v).reshape(bsz, s, D_MODEL)
    return o @ w_o


def swiglu(h, w_gate, w_up, w_down):
    return (jax.nn.silu(h @ w_gate) * (h @ w_up)) @ w_down


def setup_inputs(seed: int = 0) -> dict:
    key = jax.random.key(seed)
    ks = jax.random.split(key, 32)

    def nrm(k, shape, scale):
        return scale * jax.random.normal(k, shape, jnp.float32)

    def gain(k, shape):
        return 1.0 + 0.02 * jax.random.normal(k, shape, jnp.float32)

    u = jax.random.uniform(ks[12], (DEPTH, LRU_WIDTH), jnp.float32, 0.9, 0.999)
    a_base = u ** (1.0 / LRU_C)
    lru_lambda = jnp.log(a_base) - jnp.log1p(-a_base)
    return {
        'x': nrm(ks[0], (BATCH, SEQ, D_MODEL), 1.0),
        'mem': nrm(ks[1], (BATCH, MEM_LEN, D_MODEL), 1.0),
        'positions': jnp.tile(jnp.arange(SEQ, dtype=jnp.int32)[None, :], (BATCH, 1)),
        'pre_mix_g': gain(ks[2], (DEPTH, D_MODEL)),
        'post_mix_g': gain(ks[3], (DEPTH, D_MODEL)),
        'w_in': nrm(ks[4], (DEPTH, D_MODEL, IN_WIDTH), D_MODEL ** -0.5),
        'conv_w': nrm(ks[5], (DEPTH, CONV_WIDTH, LRU_WIDTH), CONV_WIDTH ** -0.5),
        'conv_b': nrm(ks[6], (DEPTH, LRU_WIDTH), 0.01),
        'w_rg': nrm(ks[7], (DEPTH, LRU_BLOCKS, LRU_BLOCK, LRU_BLOCK), LRU_BLOCK ** -0.5),
        'b_rg': nrm(ks[8], (DEPTH, LRU_WIDTH), 0.01),
        'w_ig': nrm(ks[9], (DEPTH, LRU_BLOCKS, LRU_BLOCK, LRU_BLOCK), LRU_BLOCK ** -0.5),
        'b_ig': nrm(ks[10], (DEPTH, LRU_WIDTH), 0.01),
        'lru_lambda': lru_lambda,
        'fox_b_f': 2.0 + nrm(ks[11], (DEPTH, FOX_HEADS), 0.5),
        'w_out': nrm(ks[13], (DEPTH, MIX_WIDTH, D_MODEL), MIX_WIDTH ** -0.5),
        'pre_cross_g': gain(ks[14], (DEPTH, D_MODEL)),
        'post_cross_g': gain(ks[15], (DEPTH, D_MODEL)),
        'mem_norm_g': gain(ks[16], (D_MODEL,)),
        'w_cq': nrm(ks[17], (DEPTH, D_MODEL, D_MODEL), D_MODEL ** -0.5),
        'w_ck': nrm(ks[18], (DEPTH, D_MODEL, D_MODEL), D_MODEL ** -0.5),
        'w_cv': nrm(ks[19], (DEPTH, D_MODEL, D_MODEL), D_MODEL ** -0.5),
        'w_co': nrm(ks[20], (DEPTH, D_MODEL, D_MODEL), D_MODEL ** -0.5),
        'pre_ffn_g': gain(ks[21], (DEPTH, D_MODEL)),
        'post_ffn_g': gain(ks[22], (DEPTH, D_MODEL)),
        'w_gate': nrm(ks[23], (DEPTH, D_MODEL, D_FF), D_MODEL ** -0.5),
        'w_up': nrm(ks[24], (DEPTH, D_MODEL, D_FF), D_MODEL ** -0.5),
        'w_down': nrm(ks[25], (DEPTH, D_FF, D_MODEL), D_FF ** -0.5),
    }


def reference(x, mem, positions, pre_mix_g, post_mix_g, w_in, conv_w, conv_b, w_rg, b_rg, w_ig, b_ig, lru_lambda, fox_b_f, w_out, pre_cross_g, post_cross_g, mem_norm_g, w_cq, w_ck, w_cv, w_co, pre_ffn_g, post_ffn_g, w_gate, w_up, w_down):
    mem_n = rms_norm(mem, mem_norm_g)
    for l in range(DEPTH):
        h = rms_norm(x, pre_mix_g[l])
        mix = parallel_mixer(h, positions, w_in[l], conv_w[l], conv_b[l], w_rg[l], b_rg[l], w_ig[l], b_ig[l], lru_lambda[l], fox_b_f[l], w_out[l])
        x = x + rms_norm(mix, post_mix_g[l])
        h = rms_norm(x, pre_cross_g[l])
        x = x + rms_norm(memory_cross_attention(h, mem_n, w_cq[l], w_ck[l], w_cv[l], w_co[l]), post_cross_g[l])
        h = rms_norm(x, pre_ffn_g[l])
        x = x + rms_norm(swiglu(h, w_gate[l], w_up[l], w_down[l]), post_ffn_g[l])
    return x
```

```python
import functools

import jax
import jax.numpy as jnp
from jax import lax
from jax.experimental import pallas as pl
from jax.experimental.pallas import tpu as pltpu

F32 = jnp.float32
BF16 = jnp.bfloat16

HEAD_DIM = 64
LRU_WIDTH = 256
LRU_BLOCKS = 4
CONV_WIDTH = 4
LRU_C = 8.0
FOX_HEADS = 8
FOX_WIDTH = FOX_HEADS * HEAD_DIM
RET_HEADS = 4
RET_WIDTH = RET_HEADS * HEAD_DIM
CROSS_HEADS = 4
RET_CHUNK = 128
ROPE_THETA = 10000.0
EPS = 1e-6

LANES = 128
SUBLANES = 8
VMEM_LIMIT = 48 * 1024 * 1024

ROW_TILE = 512
FOX_TILE = 512
SCAN_TILE = 256
NEG_BIG = -1e30


def _params(sem):
    return pltpu.CompilerParams(dimension_semantics=sem, vmem_limit_bytes=VMEM_LIMIT)


def _rms(x, g):
    ms = jnp.mean(x * x, axis=-1, keepdims=True)
    return x * lax.rsqrt(ms + EPS) * g


def _dot(a, b):
    return jnp.dot(a, b, preferred_element_type=F32)


def _dot_nt(a, b):
    return lax.dot_general(a, b, (((1,), (1,)), ((), ())), preferred_element_type=F32)


def _dot_tn(a, b):
    return lax.dot_general(a, b, (((0,), (0,)), ((), ())), preferred_element_type=F32)


def _full(shape):
    nd = len(shape)
    return pl.BlockSpec(shape, lambda *_: (0,) * nd)


def _scan_rows(a, b):
    n = b.shape[0]
    row = lax.broadcasted_iota(jnp.int32, b.shape, 0)
    k = 1
    while k < n:
        keep = row >= k
        b_prev = jnp.where(keep, pltpu.roll(b, k, 0), 0.0)
        if a is None:
            b = b + b_prev
        else:
            a_prev = jnp.where(keep, pltpu.roll(a, k, 0), 1.0)
            b = a * b_prev + b
            a = a * a_prev
        k *= 2
    return a, b


def _inproj_kernel(x_ref, g_ref, wl_ref, wf_ref, wr_ref, wff_ref, lru_ref, fox_ref, ret_ref, ff_ref):
    h = _rms(x_ref[...], g_ref[...]).astype(BF16)
    lru_ref[...] = _dot(h, wl_ref[...])
    fox_ref[:, :FOX_WIDTH] = (_dot(h, wf_ref[:, :FOX_WIDTH]) * HEAD_DIM**-0.5).astype(BF16)
    fox_ref[:, FOX_WIDTH:] = _dot(h, wf_ref[:, FOX_WIDTH:]).astype(BF16)
    ret_ref[...] = _dot(h, wr_ref[...])
    ff_ref[...] = _dot(h, wff_ref[...])


def _inproj(x, g, wl, wf, wr, wff):
    m, d = x.shape
    tm = ROW_TILE
    row = lambda n: pl.BlockSpec((tm, n), lambda i: (i, 0))
    return pl.pallas_call(
        _inproj_kernel,
        grid=(m // tm,),
        in_specs=[row(d), _full(g.shape), _full(wl.shape), _full(wf.shape), _full(wr.shape), _full(wff.shape)],
        out_specs=[row(wl.shape[1]), row(wf.shape[1]), row(wr.shape[1]), row(wff.shape[1])],
        out_shape=[
            jax.ShapeDtypeStruct((m, wl.shape[1]), F32),
            jax.ShapeDtypeStruct((m, wf.shape[1]), BF16),
            jax.ShapeDtypeStruct((m, wr.shape[1]), F32),
            jax.ShapeDtypeStruct((m, wff.shape[1]), F32),
        ],
        compiler_params=_params(("parallel",)),
        name="mixer_inproj",
    )(x, g, wl, wf, wr, wff)


def _cumsum_kernel(ff_ref, bf_ref, c_ref, carry_ref):
    @pl.when(pl.program_id(1) == 0)
    def _():
        carry_ref[...] = jnp.zeros_like(carry_ref)

    log_f = jax.nn.log_sigmoid(ff_ref[...] + bf_ref[...])
    _, c = _scan_rows(None, log_f)
    c = c + carry_ref[...]
    c_ref[...] = c
    carry_ref[...] = c[-1:, :]


def _fox_cumsum(ff, bf, bsz):
    m, n = ff.shape
    ts = SCAN_TILE
    nt = m // bsz // ts
    return pl.pallas_call(
        _cumsum_kernel,
        grid=(bsz, nt),
        in_specs=[pl.BlockSpec((ts, n), lambda b, i: (b * nt + i, 0)), _full(bf.shape)],
        out_specs=pl.BlockSpec((ts, n), lambda b, i: (b * nt + i, 0)),
        out_shape=jax.ShapeDtypeStruct((m, n), F32),
        scratch_shapes=[pltpu.VMEM((1, n), F32)],
        compiler_params=_params(("parallel", "arbitrary")),
        name="fox_gate_cumsum",
    )(ff, bf)


def _fox_kernel(q_ref, k_ref, v_ref, ccol_ref, crow_ref, o_ref, m_scr, l_scr, acc_scr):
    t = q_ref.shape[0]
    pair = pl.program_id(1)
    qi = pl.program_id(2)
    lane = lax.broadcasted_iota(jnp.int32, (t, LANES), 1)
    q = q_ref[...]
    ccol = ccol_ref[...]
    qh, cq = [], []
    for h in range(2):
        in_head = (lane >= h * HEAD_DIM) & (lane < (h + 1) * HEAD_DIM)
        qh.append(jnp.where(in_head, q, jnp.zeros_like(q)))
        cq.append(jnp.sum(jnp.where(lane == 2 * pair + h, ccol, 0.0), axis=-1, keepdims=True))

    def logits(h, kb):
        start = pl.multiple_of(kb * t, t)
        k_blk = k_ref[pl.ds(start, t), :]
        ck = crow_ref[0, pl.ds(2 * pair + h, 1), pl.ds(start, t)]
        return _dot_nt(qh[h], k_blk) + cq[h] - ck

    start = pl.multiple_of(qi * t, t)
    v_blk = v_ref[pl.ds(start, t), :]
    causal = lax.broadcasted_iota(jnp.int32, (t, t), 0) >= lax.broadcasted_iota(jnp.int32, (t, t), 1)
    for h in range(2):
        s = jnp.where(causal, logits(h, qi), NEG_BIG)
        m = jnp.max(s, axis=-1, keepdims=True)
        p = jnp.exp(s - m)
        m_scr[h] = m
        l_scr[h] = jnp.sum(p, axis=-1, keepdims=True)
        acc_scr[h] = _dot(p.astype(BF16), v_blk)

    def body(kb, carry):
        v_blk = v_ref[pl.ds(pl.multiple_of(kb * t, t), t), :]
        for h in range(2):
            s = logits(h, kb)
            m_old = m_scr[h]
            m_new = jnp.maximum(m_old, jnp.max(s, axis=-1, keepdims=True))
            alpha = jnp.exp(m_old - m_new)
            p = jnp.exp(s - m_new)
            m_scr[h] = m_new
            l_scr[h] = alpha * l_scr[h] + jnp.sum(p, axis=-1, keepdims=True)
            acc_scr[h] = alpha * acc_scr[h] + _dot(p.astype(BF16), v_blk)
        return carry

    lax.fori_loop(0, qi, body, 0)
    o0 = acc_scr[0] / l_scr[0]
    o1 = acc_scr[1] / l_scr[1]
    o_ref[...] = jnp.where(lane < HEAD_DIM, o0, o1).astype(o_ref.dtype)


def _fox_attention(fox, ccol, crow, bsz):
    m = fox.shape[0]
    s = m // bsz
    t = FOX_TILE
    nq = s // t
    npair = FOX_HEADS // 2
    return pl.pallas_call(
        _fox_kernel,
        grid=(bsz, npair, nq),
        in_specs=[
            pl.BlockSpec((t, LANES), lambda b, p, i: (b * nq + i, p)),
            pl.BlockSpec((s, LANES), lambda b, p, i: (b, npair + p)),
            pl.BlockSpec((s, LANES), lambda b, p, i: (b, 2 * npair + p)),
            pl.BlockSpec((t, LANES), lambda b, p, i: (b * nq + i, 0)),
            pl.BlockSpec((1, FOX_HEADS, s), lambda b, p, i: (b, 0, 0)),
        ],
        out_specs=pl.BlockSpec((t, LANES), lambda b, p, i: (b * nq + i, p)),
        out_shape=jax.ShapeDtypeStruct((m, FOX_WIDTH), BF16),
        scratch_shapes=[
            pltpu.VMEM((2, t, 1), F32),
            pltpu.VMEM((2, t, 1), F32),
            pltpu.VMEM((2, t, LANES), F32),
        ],
        compiler_params=_params(("parallel", "parallel", "arbitrary")),
        name="fox_attention",
    )(fox, fox, fox, ccol, crow)


def _lru_kernel(xy_ref, cw_ref, cb_ref, wg_ref, bg_ref, lam_ref, o_ref, tail_scr, h_scr):
    w = LRU_WIDTH

    @pl.when(pl.program_id(1) == 0)
    def _():
        tail_scr[...] = jnp.zeros_like(tail_scr)
        h_scr[...] = jnp.zeros_like(h_scr)

    x = xy_ref[:, :w]
    y = xy_ref[:, w:]
    tail = tail_scr[...]
    row8 = lax.broadcasted_iota(jnp.int32, (SUBLANES, w), 0)
    conv = x * cw_ref[CONV_WIDTH - 1 : CONV_WIDTH, :] + cb_ref[...]
    for j in range(1, CONV_WIDTH):
        xs = pltpu.roll(x, j, 0)
        head = jnp.where(row8 < j, pltpu.roll(tail, j, 0), xs[:SUBLANES])
        xs = jnp.concatenate([head, xs[SUBLANES:]], axis=0)
        conv = conv + xs * cw_ref[CONV_WIDTH - 1 - j : CONV_WIDTH - j, :]
    tail_scr[...] = x[-SUBLANES:]

    gates = jax.nn.sigmoid(_dot(conv.astype(BF16), wg_ref[...]) + bg_ref[...])
    r = gates[:, :w]
    i = gates[:, w:]
    log_a = -LRU_C * r * jax.nn.softplus(-lam_ref[...])
    a = jnp.exp(log_a)
    th = jnp.tanh(log_a)
    u = jnp.sqrt(-2.0 * th / (1.0 - th)) * (i * conv)
    a_run, h = _scan_rows(a, u)
    h = h + a_run * h_scr[...]
    h_scr[...] = h[-1:, :]
    o_ref[...] = (h * jax.nn.gelu(y)).astype(o_ref.dtype)


def _rg_lru(xy, cw, cb, wg, bg, lam, bsz):
    m = xy.shape[0]
    ts = SCAN_TILE
    nt = m // bsz // ts
    w = LRU_WIDTH
    return pl.pallas_call(
        _lru_kernel,
        grid=(bsz, nt),
        in_specs=[
            pl.BlockSpec((ts, 2 * w), lambda b, i: (b * nt + i, 0)),
            _full(cw.shape), _full(cb.shape), _full(wg.shape), _full(bg.shape), _full(lam.shape),
        ],
        out_specs=pl.BlockSpec((ts, w), lambda b, i: (b * nt + i, 0)),
        out_shape=jax.ShapeDtypeStruct((m, w), BF16),
        scratch_shapes=[pltpu.VMEM((SUBLANES, w), F32), pltpu.VMEM((1, w), F32)],
        compiler_params=_params(("parallel", "arbitrary")),
        name="rg_lru",
    )(xy, cw, cb, wg, bg, lam)


def _rope_kernel(pos_ref, freq_ref, cos_ref, sin_ref):
    ang = pos_ref[...].astype(F32) * freq_ref[...]
    lane = lax.broadcasted_iota(jnp.int32, ang.shape, 1)
    first_half = (lane % HEAD_DIM) < HEAD_DIM // 2
    cos_ref[...] = jnp.cos(ang)
    s = jnp.sin(ang)
    sin_ref[...] = jnp.where(first_half, -s, s)


def _rope_tables(pos, freq):
    m = pos.shape[0]
    tm = ROW_TILE
    spec = pl.BlockSpec((tm, LANES), lambda i: (i, 0))
    return pl.pallas_call(
        _rope_kernel,
        grid=(m // tm,),
        in_specs=[spec, _full(freq.shape)],
        out_specs=[spec, spec],
        out_shape=[jax.ShapeDtypeStruct((m, LANES), F32)] * 2,
        compiler_params=_params(("parallel",)),
        name="rope_tables",
    )(pos, freq)


def _split3(z):
    hi = z.astype(BF16)
    r1 = z - hi.astype(F32)
    mid = r1.astype(BF16)
    lo = (r1 - mid.astype(F32)).astype(BF16)
    return hi, mid, lo


def _group_mean(z, avg):
    hi, mid, lo = _split3(z)
    return _dot(hi, avg) + _dot(mid, avg) + _dot(lo, avg)


def _ret_kernel(x_ref, cos_ref, sin_ref, decay_ref, kw_ref, qw_ref, cd_ref, avg_ref, o_ref, state_scr):
    c = RET_CHUNK
    w = RET_WIDTH
    npair = RET_HEADS // 2

    @pl.when(pl.program_id(1) == 0)
    def _():
        state_scr[...] = jnp.zeros_like(state_scr)

    cos = cos_ref[...]
    sin = sin_ref[...]
    lane = lax.broadcasted_iota(jnp.int32, cos.shape, 1)
    first_half = (lane % HEAD_DIM) < HEAD_DIM // 2
    low_head = lax.broadcasted_iota(jnp.int32, (c, LANES), 1) < HEAD_DIM
    head_mask = (low_head, jnp.logical_not(low_head))

    def rotary(z):
        swapped = jnp.where(first_half, pltpu.roll(z, LANES - HEAD_DIM // 2, 1), pltpu.roll(z, HEAD_DIM // 2, 1))
        return z * cos + swapped * sin

    ys = []
    for p in range(npair):
        cols = slice(p * LANES, (p + 1) * LANES)
        q = rotary(x_ref[:, cols])
        k = rotary(x_ref[:, w + p * LANES : w + (p + 1) * LANES]) * HEAD_DIM**-0.5
        v = x_ref[:, 2 * w + p * LANES : 2 * w + (p + 1) * LANES].astype(BF16)
        state = state_scr[p]
        chunks = []
        for n in range(x_ref.shape[0] // c):
            rows = slice(n * c, (n + 1) * c)
            qn, kn, vn = q[rows], k[rows], v[rows]
            kb = kn.astype(BF16)
            inner = []
            for h in range(2):
                qm = jnp.where(head_mask[h], qn, 0.0).astype(BF16)
                scores = _dot_nt(qm, kb) * decay_ref[2 * p + h]
                inner.append(_dot(scores.astype(BF16), vn))
            y = jnp.where(low_head, inner[0], inner[1])
            y = y + _dot((qn * qw_ref[:, cols]).astype(BF16), state.astype(BF16))
            u = _dot_tn((kn * kw_ref[:, cols]).astype(BF16), vn)
            cd = cd_ref[p]
            state = cd * state + jnp.where(cd > 0.0, u, 0.0)
            chunks.append(y)
        state_scr[p] = state
        ys.append(jnp.concatenate(chunks, axis=0))
    y = jnp.concatenate(ys, axis=1)
    avg = avg_ref[...]
    mu = _group_mean(y, avg)
    d = y - mu
    var = _group_mean(d * d, avg)
    yn = d * lax.rsqrt(var + EPS)
    g = x_ref[:, 3 * w :]
    o_ref[...] = (g * jax.nn.sigmoid(g) * yn).astype(o_ref.dtype)


def _retention(ret, cos, sin, decay, kw, qw, cd, avg, bsz):
    m = ret.shape[0]
    ts = ROW_TILE
    nt = m // bsz // ts
    w = RET_WIDTH
    return pl.pallas_call(
        _ret_kernel,
        grid=(bsz, nt),
        in_specs=[
            pl.BlockSpec((ts, 4 * w), lambda b, i: (b * nt + i, 0)),
            pl.BlockSpec((ts, LANES), lambda b, i: (b * nt + i, 0)),
            pl.BlockSpec((ts, LANES), lambda b, i: (b * nt + i, 0)),
            _full(decay.shape), _full(kw.shape), _full(qw.shape), _full(cd.shape), _full(avg.shape),
        ],
        out_specs=pl.BlockSpec((ts, w), lambda b, i: (b * nt + i, 0)),
        out_shape=jax.ShapeDtypeStruct((m, w), BF16),
        scratch_shapes=[pltpu.VMEM((RET_HEADS // 2, LANES, LANES), F32)],
        compiler_params=_params(("parallel", "arbitrary")),
        name="retention",
    )(ret, cos, sin, decay, kw, qw, cd, avg)


def _outproj_kernel(x_ref, lru_ref, fox_ref, ret_ref, w_ref, g_ref, o_ref):
    a, b = LRU_WIDTH, LRU_WIDTH + FOX_WIDTH
    y = _dot(lru_ref[...], w_ref[:a, :]) + _dot(fox_ref[...], w_ref[a:b, :]) + _dot(ret_ref[...], w_ref[b:, :])
    o_ref[...] = x_ref[...] + _rms(y, g_ref[...])


def _outproj(x, lru, fox, ret, w, g):
    m, d = x.shape
    tm = ROW_TILE
    row = lambda n: pl.BlockSpec((tm, n), lambda i: (i, 0))
    return pl.pallas_call(
        _outproj_kernel,
        grid=(m // tm,),
        in_specs=[row(d), row(lru.shape[1]), row(fox.shape[1]), row(ret.shape[1]), _full(w.shape), _full(g.shape)],
        out_specs=row(d),
        out_shape=jax.ShapeDtypeStruct((m, d), F32),
        compiler_params=_params(("parallel",)),
        name="mixer_outproj",
    )(x, lru, fox, ret, w, g)


def _norm_matmul_kernel(x_ref, g_ref, w_ref, o_ref):
    o_ref[...] = _dot(_rms(x_ref[...], g_ref[...]).astype(BF16), w_ref[...]).astype(o_ref.dtype)


def _norm_matmul(x, g, w, tm):
    m, d = x.shape
    n = w.shape[1]
    return pl.pallas_call(
        _norm_matmul_kernel,
        grid=(m // tm,),
        in_specs=[pl.BlockSpec((tm, d), lambda i: (i, 0)), _full(g.shape), _full(w.shape)],
        out_specs=pl.BlockSpec((tm, n), lambda i: (i, 0)),
        out_shape=jax.ShapeDtypeStruct((m, n), BF16),
        compiler_params=_params(("parallel",)),
        name="memory_kv_proj",
    )(x, g, w)


def _cross_kernel(x_ref, g1_ref, wq_ref, kt_ref, v_ref, wo_ref, g2_ref, o_ref):
    x = x_ref[...]
    d = x.shape[1]
    hd = d // CROSS_HEADS
    q = _dot(_rms(x, g1_ref[...]).astype(BF16), wq_ref[...]).astype(BF16)
    outs = []
    for h in range(CROSS_HEADS):
        cols = slice(h * hd, (h + 1) * hd)
        s = _dot(q[:, cols], kt_ref[0, cols, :]) * hd**-0.5
        e = jnp.exp(s - jnp.max(s, axis=-1, keepdims=True))
        p = e / jnp.sum(e, axis=-1, keepdims=True)
        outs.append(_dot(p.astype(BF16), v_ref[0, :, cols]).astype(BF16))
    o = jnp.concatenate(outs, axis=1)
    o_ref[...] = x + _rms(_dot(o, wo_ref[...]), g2_ref[...])


def _cross_attention(x, g1, wq, kt, v, wo, g2, bsz):
    m, d = x.shape
    tm = ROW_TILE
    nt = m // bsz // tm
    mem_len = v.shape[1]
    return pl.pallas_call(
        _cross_kernel,
        grid=(m // tm,),
        in_specs=[
            pl.BlockSpec((tm, d), lambda i: (i, 0)),
            _full(g1.shape), _full(wq.shape),
            pl.BlockSpec((1, d, mem_len), lambda i: (i // nt, 0, 0)),
            pl.BlockSpec((1, mem_len, d), lambda i: (i // nt, 0, 0)),
            _full(wo.shape), _full(g2.shape),
        ],
        out_specs=pl.BlockSpec((tm, d), lambda i: (i, 0)),
        out_shape=jax.ShapeDtypeStruct((m, d), F32),
        compiler_params=_params(("parallel",)),
        name="memory_cross_attention",
    )(x, g1, wq, kt, v, wo, g2)


def _ffn_kernel(x_ref, g1_ref, wg_ref, wu_ref, wd_ref, g2_ref, o_ref, h_scr, acc_scr):
    j = pl.program_id(1)

    @pl.when(j == 0)
    def _():
        h_scr[...] = _rms(x_ref[...], g1_ref[...]).astype(BF16)
        acc_scr[...] = jnp.zeros_like(acc_scr)

    h = h_scr[...]
    gate = _dot(h, wg_ref[...])
    up = _dot(h, wu_ref[...])
    act = (gate * jax.nn.sigmoid(gate) * up).astype(BF16)
    acc_scr[...] += _dot(act, wd_ref[...])

    @pl.when(j == pl.num_programs(1) - 1)
    def _():
        o_ref[...] = x_ref[...] + _rms(acc_scr[...], g2_ref[...])


def _ffn(x, g1, wg, wu, wd, g2):
    m, d = x.shape
    dff = wg.shape[1]
    tm = ROW_TILE
    nf = 2
    tf = dff // nf
    assert tf * nf == dff and tf % LANES == 0
    return pl.pallas_call(
        _ffn_kernel,
        grid=(m // tm, nf),
        in_specs=[
            pl.BlockSpec((tm, d), lambda i, j: (i, 0)),
            _full(g1.shape),
            pl.BlockSpec((d, tf), lambda i, j: (0, j)),
            pl.BlockSpec((d, tf), lambda i, j: (0, j)),
            pl.BlockSpec((tf, d), lambda i, j: (j, 0)),
            _full(g2.shape),
        ],
        out_specs=pl.BlockSpec((tm, d), lambda i, j: (i, 0)),
        out_shape=jax.ShapeDtypeStruct((m, d), F32),
        scratch_shapes=[pltpu.VMEM((tm, d), BF16), pltpu.VMEM((tm, d), F32)],
        compiler_params=_params(("parallel", "arbitrary")),
        name="swiglu_ffn",
    )(x, g1, wg, wu, wd, g2)


def _block_diag(w):
    n, a, b = w.shape
    out = jnp.zeros((n * a, n * b), w.dtype)
    for i in range(n):
        out = out.at[i * a : (i + 1) * a, i * b : (i + 1) * b].set(w[i])
    return out


def _retention_tables():
    c = RET_CHUNK
    log_gamma = jnp.log1p(-jnp.exp2(-5.0 - jnp.arange(RET_HEADS, dtype=F32)))
    idx = jnp.arange(c, dtype=F32)
    diff = idx[:, None] - idx[None, :]
    decay = jnp.where(diff >= 0, jnp.exp(log_gamma[:, None, None] * jnp.maximum(diff, 0.0)), 0.0)
    k_w = jnp.exp(log_gamma[None, :] * (c - 1.0 - idx)[:, None])
    q_w = jnp.exp(log_gamma[None, :] * (idx + 1.0)[:, None])
    chunk_decay = jnp.exp(log_gamma * c)
    kw = jnp.repeat(k_w, HEAD_DIM, axis=1)
    qw = jnp.repeat(q_w, HEAD_DIM, axis=1)
    eye = jnp.eye(2, dtype=F32)
    cd = jnp.stack([
        jnp.kron(eye * chunk_decay[2 * p : 2 * p + 2][None, :], jnp.ones((HEAD_DIM, HEAD_DIM), F32))
        for p in range(RET_HEADS // 2)
    ])
    avg = jnp.kron(jnp.eye(RET_HEADS, dtype=F32), jnp.full((HEAD_DIM, HEAD_DIM), 1.0 / HEAD_DIM, F32)).astype(BF16)
    return decay, kw, qw, cd, avg


def kernel(x, mem, positions, pre_mix_g, post_mix_g, w_in, conv_w, conv_b, w_rg, b_rg, w_ig, b_ig, lru_lambda, fox_b_f, w_out, pre_cross_g, post_cross_g, mem_norm_g, w_cq, w_ck, w_cv, w_co, pre_ffn_g, post_ffn_g, w_gate, w_up, w_down):
    bsz, seq, d = x.shape
    depth = w_in.shape[0]
    m = bsz * seq
    mem_len = mem.shape[1]
    xf = x.reshape(m, d)
    row = lambda v: v.reshape(1, -1)

    half = HEAD_DIM // 2
    inv_freq = ROPE_THETA ** (-jnp.arange(half, dtype=F32) / half)
    freq = jnp.tile(inv_freq, LANES // half).reshape(1, LANES)
    pos = jnp.broadcast_to(positions.reshape(m, 1), (m, LANES))
    cos, sin = _rope_tables(pos, freq)
    decay, kw, qw, cd, avg = _retention_tables()

    o_fox = 2 * LRU_WIDTH
    o_ff = o_fox + 3 * FOX_WIDTH
    o_ret = o_ff + FOX_HEADS

    for l in range(depth):
        wl = w_in[l, :, :o_fox].astype(BF16)
        wf = w_in[l, :, o_fox:o_ff].astype(BF16)
        wff = jnp.pad(w_in[l, :, o_ff:o_ret], ((0, 0), (0, LANES - FOX_HEADS))).astype(BF16)
        wr = w_in[l, :, o_ret:].astype(BF16)
        lru, fox, ret, ff = _inproj(xf, row(pre_mix_g[l]), wl, wf, wr, wff)

        bf = jnp.pad(fox_b_f[l], (0, LANES - FOX_HEADS)).reshape(1, LANES)
        ccol = _fox_cumsum(ff, bf, bsz)
        crow = ccol[:, :FOX_HEADS].reshape(bsz, seq, FOX_HEADS).transpose(0, 2, 1)
        fox_o = _fox_attention(fox, ccol, crow, bsz)

        wg = jnp.concatenate([_block_diag(w_rg[l]), _block_diag(w_ig[l])], axis=1).astype(BF16)
        bg = jnp.concatenate([b_rg[l], b_ig[l]]).reshape(1, -1)
        lru_o = _rg_lru(lru, conv_w[l], row(conv_b[l]), wg, bg, row(lru_lambda[l]), bsz)

        ret_o = _retention(ret, cos, sin, decay, kw, qw, cd, avg, bsz)

        xf = _outproj(xf, lru_o, fox_o, ret_o, w_out[l].astype(BF16), row(post_mix_g[l]))

        wkv = jnp.concatenate([w_ck[l], w_cv[l]], axis=1).astype(BF16)
        kv = _norm_matmul(mem.reshape(bsz * mem_len, d), row(mem_norm_g), wkv, mem_len)
        kt = kv[:, :d].reshape(bsz, mem_len, d).transpose(0, 2, 1)
        vv = kv[:, d:].reshape(bsz, mem_len, d)
        xf = _cross_attention(xf, row(pre_cross_g[l]), w_cq[l].astype(BF16), kt, vv, w_co[l].astype(BF16), row(post_cross_g[l]), bsz)

        xf = _ffn(xf, row(pre_ffn_g[l]), w_gate[l].astype(BF16), w_up[l].astype(BF16), w_down[l].astype(BF16), row(post_ffn_g[l]))
    return xf.reshape(bsz, seq, d)
```

```python
import functools

import jax
import jax.numpy as jnp
from jax import lax
from jax.experimental import pallas as pl
from jax.experimental.pallas import tpu as pltpu

F32 = jnp.float32
BF16 = jnp.bfloat16

HEAD_DIM = 64
LRU_WIDTH = 256
LRU_BLOCKS = 4
CONV_WIDTH = 4
LRU_C = 8.0
FOX_HEADS = 8
FOX_WIDTH = FOX_HEADS * HEAD_DIM
RET_HEADS = 4
RET_WIDTH = RET_HEADS * HEAD_DIM
CROSS_HEADS = 4
RET_CHUNK = 128
ROPE_THETA = 10000.0
EPS = 1e-6

LANES = 128
SUBLANES = 8
VMEM_LIMIT = 48 * 1024 * 1024

ROW_TILE = 512
FOX_TILE = 512
SCAN_TILE = 256
NEG_BIG = -1e30
LOG2E = 1.4426950408889634


def _params(sem):
    return pltpu.CompilerParams(dimension_semantics=sem, vmem_limit_bytes=VMEM_LIMIT)


def _rms(x, g):
    ms = jnp.mean(x * x, axis=-1, keepdims=True)
    return x * lax.rsqrt(ms + EPS) * g


def _dot(a, b):
    return jnp.dot(a, b, preferred_element_type=F32)


def _dot_nt(a, b):
    return lax.dot_general(a, b, (((1,), (1,)), ((), ())), preferred_element_type=F32)


def _dot_tn(a, b):
    return lax.dot_general(a, b, (((0,), (0,)), ((), ())), preferred_element_type=F32)


def _full(shape):
    nd = len(shape)
    return pl.BlockSpec(shape, lambda *_: (0,) * nd)


def _scan_rows(a, b):
    n = b.shape[0]
    row = lax.broadcasted_iota(jnp.int32, b.shape, 0)
    k = 1
    while k < n:
        keep = row >= k
        b_prev = jnp.where(keep, pltpu.roll(b, k, 0), 0.0)
        if a is None:
            b = b + b_prev
        else:
            a_prev = jnp.where(keep, pltpu.roll(a, k, 0), 1.0)
            b = a * b_prev + b
            a = a * a_prev
        k *= 2
    return a, b


def _inproj_kernel(x_ref, g_ref, wl_ref, wf_ref, wr_ref, wff_ref, lru_ref, fox_ref, ret_ref, ff_ref):
    h = _rms(x_ref[...], g_ref[...]).astype(BF16)
    lru_ref[...] = _dot(h, wl_ref[...])
    fox_ref[:, :FOX_WIDTH] = (_dot(h, wf_ref[:, :FOX_WIDTH]) * (HEAD_DIM**-0.5 * LOG2E)).astype(BF16)
    fox_ref[:, FOX_WIDTH:] = _dot(h, wf_ref[:, FOX_WIDTH:]).astype(BF16)
    ret_ref[...] = _dot(h, wr_ref[...])
    ff_ref[...] = _dot(h, wff_ref[...])


def _inproj(x, g, wl, wf, wr, wff):
    m, d = x.shape
    tm = ROW_TILE
    row = lambda n: pl.BlockSpec((tm, n), lambda i: (i, 0))
    return pl.pallas_call(
        _inproj_kernel,
        grid=(m // tm,),
        in_specs=[row(d), _full(g.shape), _full(wl.shape), _full(wf.shape), _full(wr.shape), _full(wff.shape)],
        out_specs=[row(wl.shape[1]), row(wf.shape[1]), row(wr.shape[1]), row(wff.shape[1])],
        out_shape=[
            jax.ShapeDtypeStruct((m, wl.shape[1]), F32),
            jax.ShapeDtypeStruct((m, wf.shape[1]), BF16),
            jax.ShapeDtypeStruct((m, wr.shape[1]), F32),
            jax.ShapeDtypeStruct((m, wff.shape[1]), F32),
        ],
        compiler_params=_params(("parallel",)),
        name="mixer_inproj",
    )(x, g, wl, wf, wr, wff)


def _cumsum_kernel(ff_ref, bf_ref, c_ref, carry_ref):
    @pl.when(pl.program_id(1) == 0)
    def _():
        carry_ref[...] = jnp.zeros_like(carry_ref)

    log_f = jax.nn.log_sigmoid(ff_ref[...] + bf_ref[...])
    _, c = _scan_rows(None, log_f)
    c = c + carry_ref[...]
    c_ref[...] = c
    carry_ref[...] = c[-1:, :]


def _fox_cumsum(ff, bf, bsz):
    m, n = ff.shape
    ts = SCAN_TILE
    nt = m // bsz // ts
    return pl.pallas_call(
        _cumsum_kernel,
        grid=(bsz, nt),
        in_specs=[pl.BlockSpec((ts, n), lambda b, i: (b * nt + i, 0)), _full(bf.shape)],
        out_specs=pl.BlockSpec((ts, n), lambda b, i: (b * nt + i, 0)),
        out_shape=jax.ShapeDtypeStruct((m, n), F32),
        scratch_shapes=[pltpu.VMEM((1, n), F32)],
        compiler_params=_params(("parallel", "arbitrary")),
        name="fox_gate_cumsum",
    )(ff, bf)


def _split3(z):
    hi = z.astype(BF16).astype(F32)
    mid = (z - hi).astype(BF16).astype(F32)
    lo = (z - hi - mid).astype(BF16).astype(F32)
    return hi, mid, lo


def _fox_prep_kernel(fox_ref, ccol_ref, qt_ref, k_ref, vt_ref):
    ts = fox_ref.shape[0]
    lane = lax.broadcasted_iota(jnp.int32, (ts, HEAD_DIM), 1)
    ccol = ccol_ref[...] * LOG2E
    for h in range(FOX_HEADS):
        hi, mid, lo = _split3(ccol[:, h : h + 1])
        q_extra = jnp.where(lane == 0, hi, jnp.where(lane == 1, mid, jnp.where(lane == 2, lo, jnp.where(lane < 6, 1.0, 0.0))))
        k_extra = jnp.where(lane < 3, 1.0, jnp.where(lane == 3, -hi, jnp.where(lane == 4, -mid, jnp.where(lane == 5, -lo, 0.0))))
        v_extra = jnp.where(lane == 0, 1.0, 0.0)
        cols = slice(h * HEAD_DIM, (h + 1) * HEAD_DIM)
        q = fox_ref[:, cols].astype(F32)
        k = fox_ref[:, FOX_WIDTH + h * HEAD_DIM : FOX_WIDTH + (h + 1) * HEAD_DIM].astype(F32)
        v = fox_ref[:, 2 * FOX_WIDTH + h * HEAD_DIM : 2 * FOX_WIDTH + (h + 1) * HEAD_DIM].astype(F32)
        qt_ref[0, h] = jnp.concatenate([q, q_extra], axis=1).T.astype(BF16)
        k_ref[0, h] = jnp.concatenate([k, k_extra], axis=1).astype(BF16)
        vt_ref[0, h] = jnp.concatenate([v, v_extra], axis=1).T.astype(BF16)


def _fox_prep(fox, ccol, bsz):
    m = fox.shape[0]
    s = m // bsz
    ts = ROW_TILE
    nt = s // ts
    return pl.pallas_call(
        _fox_prep_kernel,
        grid=(bsz, nt),
        in_specs=[
            pl.BlockSpec((ts, fox.shape[1]), lambda b, i: (b * nt + i, 0)),
            pl.BlockSpec((ts, LANES), lambda b, i: (b * nt + i, 0)),
        ],
        out_specs=[
            pl.BlockSpec((1, FOX_HEADS, LANES, ts), lambda b, i: (b, 0, 0, i)),
            pl.BlockSpec((1, FOX_HEADS, ts, LANES), lambda b, i: (b, 0, i, 0)),
            pl.BlockSpec((1, FOX_HEADS, LANES, ts), lambda b, i: (b, 0, 0, i)),
        ],
        out_shape=[
            jax.ShapeDtypeStruct((bsz, FOX_HEADS, LANES, s), BF16),
            jax.ShapeDtypeStruct((bsz, FOX_HEADS, s, LANES), BF16),
            jax.ShapeDtypeStruct((bsz, FOX_HEADS, LANES, s), BF16),
        ],
        compiler_params=_params(("parallel", "parallel")),
        name="fox_prep",
    )(fox, ccol)


def _fox_kernel(qt_ref, k_ref, vt_ref, o_ref, m_scr, acc_scr):
    t = qt_ref.shape[-1]
    qi = pl.program_id(2)

    def tile(h, kb, masked):
        start = pl.multiple_of(kb * t, t)
        s = _dot(k_ref[0, h, pl.ds(start, t), :], qt_ref[0, h])
        if masked:
            key = lax.broadcasted_iota(jnp.int32, (t, t), 0)
            qry = lax.broadcasted_iota(jnp.int32, (t, t), 1)
            s = jnp.where(key <= qry, s, NEG_BIG)
        return s, vt_ref[0, h, :, pl.ds(start, t)]

    for h in range(2):
        s, vt = tile(h, qi, True)
        m = jnp.max(s, axis=0, keepdims=True)
        m_scr[h] = m
        acc_scr[h] = _dot(vt, jnp.exp2(s - m).astype(BF16))

    def body(kb, carry):
        for h in range(2):
            s, vt = tile(h, kb, False)
            m_old = m_scr[h]
            m_new = jnp.maximum(m_old, jnp.max(s, axis=0, keepdims=True))
            m_scr[h] = m_new
            acc_scr[h] = jnp.exp2(m_old - m_new) * acc_scr[h] + _dot(vt, jnp.exp2(s - m_new).astype(BF16))
        return carry

    lax.fori_loop(0, qi, body, 0)
    outs = []
    for h in range(2):
        acc = acc_scr[h]
        o = acc / acc[HEAD_DIM : HEAD_DIM + 1, :]
        outs.append(o.T[:, :HEAD_DIM])
    o_ref[...] = jnp.concatenate(outs, axis=1).astype(o_ref.dtype)


def _fox_attention(qt, k, vt):
    bsz, nh, _, s = qt.shape
    t = FOX_TILE
    nq = s // t
    npair = nh // 2
    resident = pl.Buffered(1)
    return pl.pallas_call(
        _fox_kernel,
        grid=(bsz, npair, nq),
        in_specs=[
            pl.BlockSpec((1, 2, LANES, t), lambda b, p, i: (b, p, 0, i)),
            pl.BlockSpec((1, 2, s, LANES), lambda b, p, i: (b, p, 0, 0), pipeline_mode=resident),
            pl.BlockSpec((1, 2, LANES, s), lambda b, p, i: (b, p, 0, 0), pipeline_mode=resident),
        ],
        out_specs=pl.BlockSpec((t, LANES), lambda b, p, i: (b * nq + i, p)),
        out_shape=jax.ShapeDtypeStruct((bsz * s, nh * HEAD_DIM), BF16),
        scratch_shapes=[pltpu.VMEM((2, 1, t), F32), pltpu.VMEM((2, LANES, t), F32)],
        compiler_params=_params(("parallel", "parallel", "arbitrary")),
        name="fox_attention",
    )(qt, k, vt)


def _lru_kernel(xy_ref, cw_ref, cb_ref, wg_ref, bg_ref, lam_ref, o_ref, tail_scr, h_scr):
    w = LRU_WIDTH

    @pl.when(pl.program_id(1) == 0)
    def _():
        tail_scr[...] = jnp.zeros_like(tail_scr)
        h_scr[...] = jnp.zeros_like(h_scr)

    x = xy_ref[:, :w]
    y = xy_ref[:, w:]
    tail = tail_scr[...]
    row8 = lax.broadcasted_iota(jnp.int32, (SUBLANES, w), 0)
    conv = x * cw_ref[CONV_WIDTH - 1 : CONV_WIDTH, :] + cb_ref[...]
    for j in range(1, CONV_WIDTH):
        xs = pltpu.roll(x, j, 0)
        head = jnp.where(row8 < j, pltpu.roll(tail, j, 0), xs[:SUBLANES])
        xs = jnp.concatenate([head, xs[SUBLANES:]], axis=0)
        conv = conv + xs * cw_ref[CONV_WIDTH - 1 - j : CONV_WIDTH - j, :]
    tail_scr[...] = x[-SUBLANES:]

    gates = jax.nn.sigmoid(_dot(conv.astype(BF16), wg_ref[...]) + bg_ref[...])
    r = gates[:, :w]
    i = gates[:, w:]
    log_a = -LRU_C * r * jax.nn.softplus(-lam_ref[...])
    a = jnp.exp(log_a)
    th = jnp.tanh(log_a)
    u = jnp.sqrt(-2.0 * th / (1.0 - th)) * (i * conv)
    a_run, h = _scan_rows(a, u)
    h = h + a_run * h_scr[...]
    h_scr[...] = h[-1:, :]
    o_ref[...] = (h * jax.nn.gelu(y)).astype(o_ref.dtype)


def _rg_lru(xy, cw, cb, wg, bg, lam, bsz):
    m = xy.shape[0]
    ts = SCAN_TILE
    nt = m // bsz // ts
    w = LRU_WIDTH
    return pl.pallas_call(
        _lru_kernel,
        grid=(bsz, nt),
        in_specs=[
            pl.BlockSpec((ts, 2 * w), lambda b, i: (b * nt + i, 0)),
            _full(cw.shape), _full(cb.shape), _full(wg.shape), _full(bg.shape), _full(lam.shape),
        ],
        out_specs=pl.BlockSpec((ts, w), lambda b, i: (b * nt + i, 0)),
        out_shape=jax.ShapeDtypeStruct((m, w), BF16),
        scratch_shapes=[pltpu.VMEM((SUBLANES, w), F32), pltpu.VMEM((1, w), F32)],
        compiler_params=_params(("parallel", "arbitrary")),
        name="rg_lru",
    )(xy, cw, cb, wg, bg, lam)


def _rope_kernel(pos_ref, freq_ref, cos_ref, sin_ref):
    ang = pos_ref[...].astype(F32) * freq_ref[...]
    lane = lax.broadcasted_iota(jnp.int32, ang.shape, 1)
    first_half = (lane % HEAD_DIM) < HEAD_DIM // 2
    cos_ref[...] = jnp.cos(ang)
    s = jnp.sin(ang)
    sin_ref[...] = jnp.where(first_half, -s, s)


def _rope_tables(pos, freq):
    m = pos.shape[0]
    tm = ROW_TILE
    spec = pl.BlockSpec((tm, LANES), lambda i: (i, 0))
    return pl.pallas_call(
        _rope_kernel,
        grid=(m // tm,),
        in_specs=[spec, _full(freq.shape)],
        out_specs=[spec, spec],
        out_shape=[jax.ShapeDtypeStruct((m, LANES), F32)] * 2,
        compiler_params=_params(("parallel",)),
        name="rope_tables",
    )(pos, freq)


def _group_mean(z, avg):
    hi, mid, lo = _split3(z)
    return _dot(hi.astype(BF16), avg) + _dot(mid.astype(BF16), avg) + _dot(lo.astype(BF16), avg)


def _ret_kernel(x_ref, cos_ref, sin_ref, decay_ref, kw_ref, qw_ref, cd_ref, avg_ref, o_ref, state_scr):
    c = RET_CHUNK
    w = RET_WIDTH
    npair = RET_HEADS // 2

    @pl.when(pl.program_id(1) == 0)
    def _():
        state_scr[...] = jnp.zeros_like(state_scr)

    cos = cos_ref[...]
    sin = sin_ref[...]
    lane = lax.broadcasted_iota(jnp.int32, cos.shape, 1)
    first_half = (lane % HEAD_DIM) < HEAD_DIM // 2
    low_head = lax.broadcasted_iota(jnp.int32, (c, LANES), 1) < HEAD_DIM
    head_mask = (low_head, jnp.logical_not(low_head))

    def rotary(z):
        swapped = jnp.where(first_half, pltpu.roll(z, LANES - HEAD_DIM // 2, 1), pltpu.roll(z, HEAD_DIM // 2, 1))
        return z * cos + swapped * sin

    ys = []
    for p in range(npair):
        cols = slice(p * LANES, (p + 1) * LANES)
        q = rotary(x_ref[:, cols])
        k = rotary(x_ref[:, w + p * LANES : w + (p + 1) * LANES]) * HEAD_DIM**-0.5
        v = x_ref[:, 2 * w + p * LANES : 2 * w + (p + 1) * LANES].astype(BF16)
        state = state_scr[p]
        chunks = []
        for n in range(x_ref.shape[0] // c):
            rows = slice(n * c, (n + 1) * c)
            qn, kn, vn = q[rows], k[rows], v[rows]
            kb = kn.astype(BF16)
            inner = []
            for h in range(2):
                qm = jnp.where(head_mask[h], qn, 0.0).astype(BF16)
                scores = _dot_nt(qm, kb) * decay_ref[2 * p + h]
                inner.append(_dot(scores.astype(BF16), vn))
            y = jnp.where(low_head, inner[0], inner[1])
            y = y + _dot((qn * qw_ref[:, cols]).astype(BF16), state.astype(BF16))
            u = _dot_tn((kn * kw_ref[:, cols]).astype(BF16), vn)
            cd = cd_ref[p]
            state = cd * state + jnp.where(cd > 0.0, u, 0.0)
            chunks.append(y)
        state_scr[p] = state
        ys.append(jnp.concatenate(chunks, axis=0))
    y = jnp.concatenate(ys, axis=1)
    avg = avg_ref[...]
    mu = _group_mean(y, avg)
    d = y - mu
    var = _group_mean(d * d, avg)
    yn = d * lax.rsqrt(var + EPS)
    g = x_ref[:, 3 * w :]
    o_ref[...] = (g * jax.nn.sigmoid(g) * yn).astype(o_ref.dtype)


def _retention(ret, cos, sin, decay, kw, qw, cd, avg, bsz):
    m = ret.shape[0]
    ts = ROW_TILE
    nt = m // bsz // ts
    w = RET_WIDTH
    return pl.pallas_call(
        _ret_kernel,
        grid=(bsz, nt),
        in_specs=[
            pl.BlockSpec((ts, 4 * w), lambda b, i: (b * nt + i, 0)),
            pl.BlockSpec((ts, LANES), lambda b, i: (b * nt + i, 0)),
            pl.BlockSpec((ts, LANES), lambda b, i: (b * nt + i, 0)),
            _full(decay.shape), _full(kw.shape), _full(qw.shape), _full(cd.shape), _full(avg.shape),
        ],
        out_specs=pl.BlockSpec((ts, w), lambda b, i: (b * nt + i, 0)),
        out_shape=jax.ShapeDtypeStruct((m, w), BF16),
        scratch_shapes=[pltpu.VMEM((RET_HEADS // 2, LANES, LANES), F32)],
        compiler_params=_params(("parallel", "arbitrary")),
        name="retention",
    )(ret, cos, sin, decay, kw, qw, cd, avg)


def _outproj_kernel(x_ref, lru_ref, fox_ref, ret_ref, w_ref, g_ref, o_ref):
    a, b = LRU_WIDTH, LRU_WIDTH + FOX_WIDTH
    y = _dot(lru_ref[...], w_ref[:a, :]) + _dot(fox_ref[...], w_ref[a:b, :]) + _dot(ret_ref[...], w_ref[b:, :])
    o_ref[...] = x_ref[...] + _rms(y, g_ref[...])


def _outproj(x, lru, fox, ret, w, g):
    m, d = x.shape
    tm = ROW_TILE
    row = lambda n: pl.BlockSpec((tm, n), lambda i: (i, 0))
    return pl.pallas_call(
        _outproj_kernel,
        grid=(m // tm,),
        in_specs=[row(d), row(lru.shape[1]), row(fox.shape[1]), row(ret.shape[1]), _full(w.shape), _full(g.shape)],
        out_specs=row(d),
        out_shape=jax.ShapeDtypeStruct((m, d), F32),
        compiler_params=_params(("parallel",)),
        name="mixer_outproj",
    )(x, lru, fox, ret, w, g)


def _norm_matmul_kernel(x_ref, g_ref, w_ref, o_ref):
    o_ref[...] = _dot(_rms(x_ref[...], g_ref[...]).astype(BF16), w_ref[...]).astype(o_ref.dtype)


def _norm_matmul(x, g, w, tm):
    m, d = x.shape
    n = w.shape[1]
    return pl.pallas_call(
        _norm_matmul_kernel,
        grid=(m // tm,),
        in_specs=[pl.BlockSpec((tm, d), lambda i: (i, 0)), _full(g.shape), _full(w.shape)],
        out_specs=pl.BlockSpec((tm, n), lambda i: (i, 0)),
        out_shape=jax.ShapeDtypeStruct((m, n), BF16),
        compiler_params=_params(("parallel",)),
        name="memory_kv_proj",
    )(x, g, w)


def _cross_kernel(x_ref, g1_ref, wq_ref, kt_ref, v_ref, wo_ref, g2_ref, o_ref):
    x = x_ref[...]
    d = x.shape[1]
    hd = d // CROSS_HEADS
    q = _dot(_rms(x, g1_ref[...]).astype(BF16), wq_ref[...]).astype(BF16)
    outs = []
    for h in range(CROSS_HEADS):
        cols = slice(h * hd, (h + 1) * hd)
        s = _dot(q[:, cols], kt_ref[0, cols, :]) * hd**-0.5
        e = jnp.exp(s - jnp.max(s, axis=-1, keepdims=True))
        p = e / jnp.sum(e, axis=-1, keepdims=True)
        outs.append(_dot(p.astype(BF16), v_ref[0, :, cols]).astype(BF16))
    o = jnp.concatenate(outs, axis=1)
    o_ref[...] = x + _rms(_dot(o, wo_ref[...]), g2_ref[...])


def _cross_attention(x, g1, wq, kt, v, wo, g2, bsz):
    m, d = x.shape
    tm = ROW_TILE
    nt = m // bsz // tm
    mem_len = v.shape[1]
    return pl.pallas_call(
        _cross_kernel,
        grid=(m // tm,),
        in_specs=[
            pl.BlockSpec((tm, d), lambda i: (i, 0)),
            _full(g1.shape), _full(wq.shape),
            pl.BlockSpec((1, d, mem_len), lambda i: (i // nt, 0, 0)),
            pl.BlockSpec((1, mem_len, d), lambda i: (i // nt, 0, 0)),
            _full(wo.shape), _full(g2.shape),
        ],
        out_specs=pl.BlockSpec((tm, d), lambda i: (i, 0)),
        out_shape=jax.ShapeDtypeStruct((m, d), F32),
        compiler_params=_params(("parallel",)),
        name="memory_cross_attention",
    )(x, g1, wq, kt, v, wo, g2)


def _ffn_kernel(x_ref, g1_ref, wg_ref, wu_ref, wd_ref, g2_ref, o_ref, h_scr, acc_scr):
    j = pl.program_id(1)

    @pl.when(j == 0)
    def _():
        h_scr[...] = _rms(x_ref[...], g1_ref[...]).astype(BF16)
        acc_scr[...] = jnp.zeros_like(acc_scr)

    h = h_scr[...]
    gate = _dot(h, wg_ref[...])
    up = _dot(h, wu_ref[...])
    act = (gate * jax.nn.sigmoid(gate) * up).astype(BF16)
    acc_scr[...] += _dot(act, wd_ref[...])

    @pl.when(j == pl.num_programs(1) - 1)
    def _():
        o_ref[...] = x_ref[...] + _rms(acc_scr[...], g2_ref[...])


def _ffn(x, g1, wg, wu, wd, g2):
    m, d = x.shape
    dff = wg.shape[1]
    tm = ROW_TILE
    nf = 2
    tf = dff // nf
    assert tf * nf == dff and tf % LANES == 0
    return pl.pallas_call(
        _ffn_kernel,
        grid=(m // tm, nf),
        in_specs=[
            pl.BlockSpec((tm, d), lambda i, j: (i, 0)),
            _full(g1.shape),
            pl.BlockSpec((d, tf), lambda i, j: (0, j)),
            pl.BlockSpec((d, tf), lambda i, j: (0, j)),
            pl.BlockSpec((tf, d), lambda i, j: (j, 0)),
            _full(g2.shape),
        ],
        out_specs=pl.BlockSpec((tm, d), lambda i, j: (i, 0)),
        out_shape=jax.ShapeDtypeStruct((m, d), F32),
        scratch_shapes=[pltpu.VMEM((tm, d), BF16), pltpu.VMEM((tm, d), F32)],
        compiler_params=_params(("parallel", "arbitrary")),
        name="swiglu_ffn",
    )(x, g1, wg, wu, wd, g2)


def _block_diag(w):
    n, a, b = w.shape
    out = jnp.zeros((n * a, n * b), w.dtype)
    for i in range(n):
        out = out.at[i * a : (i + 1) * a, i * b : (i + 1) * b].set(w[i])
    return out


def _retention_tables():
    c = RET_CHUNK
    log_gamma = jnp.log1p(-jnp.exp2(-5.0 - jnp.arange(RET_HEADS, dtype=F32)))
    idx = jnp.arange(c, dtype=F32)
    diff = idx[:, None] - idx[None, :]
    decay = jnp.where(diff >= 0, jnp.exp(log_gamma[:, None, None] * jnp.maximum(diff, 0.0)), 0.0)
    k_w = jnp.exp(log_gamma[None, :] * (c - 1.0 - idx)[:, None])
    q_w = jnp.exp(log_gamma[None, :] * (idx + 1.0)[:, None])
    chunk_decay = jnp.exp(log_gamma * c)
    kw = jnp.repeat(k_w, HEAD_DIM, axis=1)
    qw = jnp.repeat(q_w, HEAD_DIM, axis=1)
    eye = jnp.eye(2, dtype=F32)
    cd = jnp.stack([
        jnp.kron(eye * chunk_decay[2 * p : 2 * p + 2][None, :], jnp.ones((HEAD_DIM, HEAD_DIM), F32))
        for p in range(RET_HEADS // 2)
    ])
    avg = jnp.kron(jnp.eye(RET_HEADS, dtype=F32), jnp.full((HEAD_DIM, HEAD_DIM), 1.0 / HEAD_DIM, F32)).astype(BF16)
    return decay, kw, qw, cd, avg


def kernel(x, mem, positions, pre_mix_g, post_mix_g, w_in, conv_w, conv_b, w_rg, b_rg, w_ig, b_ig, lru_lambda, fox_b_f, w_out, pre_cross_g, post_cross_g, mem_norm_g, w_cq, w_ck, w_cv, w_co, pre_ffn_g, post_ffn_g, w_gate, w_up, w_down):
    bsz, seq, d = x.shape
    depth = w_in.shape[0]
    m = bsz * seq
    mem_len = mem.shape[1]
    xf = x.reshape(m, d)
    row = lambda v: v.reshape(1, -1)

    half = HEAD_DIM // 2
    inv_freq = ROPE_THETA ** (-jnp.arange(half, dtype=F32) / half)
    freq = jnp.tile(inv_freq, LANES // half).reshape(1, LANES)
    pos = jnp.broadcast_to(positions.reshape(m, 1), (m, LANES))
    cos, sin = _rope_tables(pos, freq)
    decay, kw, qw, cd, avg = _retention_tables()

    o_fox = 2 * LRU_WIDTH
    o_ff = o_fox + 3 * FOX_WIDTH
    o_ret = o_ff + FOX_HEADS

    for l in range(depth):
        wl = w_in[l, :, :o_fox].astype(BF16)
        wf = w_in[l, :, o_fox:o_ff].astype(BF16)
        wff = jnp.pad(w_in[l, :, o_ff:o_ret], ((0, 0), (0, LANES - FOX_HEADS))).astype(BF16)
        wr = w_in[l, :, o_ret:].astype(BF16)
        lru, fox, ret, ff = _inproj(xf, row(pre_mix_g[l]), wl, wf, wr, wff)

        bf = jnp.pad(fox_b_f[l], (0, LANES - FOX_HEADS)).reshape(1, LANES)
        ccol = _fox_cumsum(ff, bf, bsz)
        fox_o = _fox_attention(*_fox_prep(fox, ccol, bsz))

        wg = jnp.concatenate([_block_diag(w_rg[l]), _block_diag(w_ig[l])], axis=1).astype(BF16)
        bg = jnp.concatenate([b_rg[l], b_ig[l]]).reshape(1, -1)
        lru_o = _rg_lru(lru, conv_w[l], row(conv_b[l]), wg, bg, row(lru_lambda[l]), bsz)

        ret_o = _retention(ret, cos, sin, decay, kw, qw, cd, avg, bsz)

        xf = _outproj(xf, lru_o, fox_o, ret_o, w_out[l].astype(BF16), row(post_mix_g[l]))

        wkv = jnp.concatenate([w_ck[l], w_cv[l]], axis=1).astype(BF16)
        kv = _norm_matmul(mem.reshape(bsz * mem_len, d), row(mem_norm_g), wkv, mem_len)
        kt = kv[:, :d].reshape(bsz, mem_len, d).transpose(0, 2, 1)
        vv = kv[:, d:].reshape(bsz, mem_len, d)
        xf = _cross_attention(xf, row(pre_cross_g[l]), w_cq[l].astype(BF16), kt, vv, w_co[l].astype(BF16), row(post_cross_g[l]), bsz)

        xf = _ffn(xf, row(pre_ffn_g[l]), w_gate[l].astype(BF16), w_up[l].astype(BF16), w_down[l].astype(BF16), row(post_ffn_g[l]))
    return xf.reshape(bsz, seq, d)
```

```python
import functools

import jax
import jax.numpy as jnp
from jax import lax
from jax.experimental import pallas as pl
from jax.experimental.pallas import tpu as pltpu

F32 = jnp.float32
BF16 = jnp.bfloat16

HEAD_DIM = 64
LRU_WIDTH = 256
LRU_BLOCKS = 4
CONV_WIDTH = 4
LRU_C = 8.0
FOX_HEADS = 8
FOX_WIDTH = FOX_HEADS * HEAD_DIM
RET_HEADS = 4
RET_WIDTH = RET_HEADS * HEAD_DIM
CROSS_HEADS = 4
RET_CHUNK = 128
ROPE_THETA = 10000.0
EPS = 1e-6

LANES = 128
SUBLANES = 8
VMEM_LIMIT = 48 * 1024 * 1024

ROW_TILE = 512
FOX_TILE = 512
SCAN_TILE = 256
NEG_BIG = -1e30
LOG2E = 1.4426950408889634
SKIP_LOG2 = 160.0
SKIP_REL = 2.0**-8


def _params(sem):
    return pltpu.CompilerParams(dimension_semantics=sem, vmem_limit_bytes=VMEM_LIMIT)


def _rms(x, g):
    ms = jnp.mean(x * x, axis=-1, keepdims=True)
    return x * lax.rsqrt(ms + EPS) * g


def _dot(a, b):
    return jnp.dot(a, b, preferred_element_type=F32)


def _dot_nt(a, b):
    return lax.dot_general(a, b, (((1,), (1,)), ((), ())), preferred_element_type=F32)


def _dot_tn(a, b):
    return lax.dot_general(a, b, (((0,), (0,)), ((), ())), preferred_element_type=F32)


def _full(shape):
    nd = len(shape)
    return pl.BlockSpec(shape, lambda *_: (0,) * nd)


def _scan_rows(a, b):
    n = b.shape[0]
    row = lax.broadcasted_iota(jnp.int32, b.shape, 0)
    k = 1
    while k < n:
        keep = row >= k
        b_prev = jnp.where(keep, pltpu.roll(b, k, 0), 0.0)
        if a is None:
            b = b + b_prev
        else:
            a_prev = jnp.where(keep, pltpu.roll(a, k, 0), 1.0)
            b = a * b_prev + b
            a = a * a_prev
        k *= 2
    return a, b


def _inproj_kernel(x_ref, g_ref, wl_ref, wf_ref, wr_ref, wff_ref, lru_ref, fox_ref, ret_ref, ff_ref):
    h = _rms(x_ref[...], g_ref[...]).astype(BF16)
    lru_ref[...] = _dot(h, wl_ref[...])
    fox_ref[:, :FOX_WIDTH] = (_dot(h, wf_ref[:, :FOX_WIDTH]) * (HEAD_DIM**-0.5 * LOG2E)).astype(BF16)
    fox_ref[:, FOX_WIDTH:] = _dot(h, wf_ref[:, FOX_WIDTH:]).astype(BF16)
    ret_ref[...] = _dot(h, wr_ref[...])
    ff_ref[...] = _dot(h, wff_ref[...])


def _inproj(x, g, wl, wf, wr, wff):
    m, d = x.shape
    tm = ROW_TILE
    row = lambda n: pl.BlockSpec((tm, n), lambda i: (i, 0))
    return pl.pallas_call(
        _inproj_kernel,
        grid=(m // tm,),
        in_specs=[row(d), _full(g.shape), _full(wl.shape), _full(wf.shape), _full(wr.shape), _full(wff.shape)],
        out_specs=[row(wl.shape[1]), row(wf.shape[1]), row(wr.shape[1]), row(wff.shape[1])],
        out_shape=[
            jax.ShapeDtypeStruct((m, wl.shape[1]), F32),
            jax.ShapeDtypeStruct((m, wf.shape[1]), BF16),
            jax.ShapeDtypeStruct((m, wr.shape[1]), F32),
            jax.ShapeDtypeStruct((m, wff.shape[1]), F32),
        ],
        compiler_params=_params(("parallel",)),
        name="mixer_inproj",
    )(x, g, wl, wf, wr, wff)


def _cumsum_kernel(ff_ref, bf_ref, c_ref, carry_ref):
    @pl.when(pl.program_id(1) == 0)
    def _():
        carry_ref[...] = jnp.zeros_like(carry_ref)

    log_f = jax.nn.log_sigmoid(ff_ref[...] + bf_ref[...])
    _, c = _scan_rows(None, log_f)
    c = c + carry_ref[...]
    c_ref[...] = c
    carry_ref[...] = c[-1:, :]


def _fox_cumsum(ff, bf, bsz):
    m, n = ff.shape
    ts = SCAN_TILE
    nt = m // bsz // ts
    return pl.pallas_call(
        _cumsum_kernel,
        grid=(bsz, nt),
        in_specs=[pl.BlockSpec((ts, n), lambda b, i: (b * nt + i, 0)), _full(bf.shape)],
        out_specs=pl.BlockSpec((ts, n), lambda b, i: (b * nt + i, 0)),
        out_shape=jax.ShapeDtypeStruct((m, n), F32),
        scratch_shapes=[pltpu.VMEM((1, n), F32)],
        compiler_params=_params(("parallel", "arbitrary")),
        name="fox_gate_cumsum",
    )(ff, bf)


def _split3(z):
    hi = z.astype(BF16).astype(F32)
    mid = (z - hi).astype(BF16).astype(F32)
    lo = (z - hi - mid).astype(BF16).astype(F32)
    return hi, mid, lo


def _row_sums(z, ones):
    hi = z.astype(BF16)
    lo = (z - hi.astype(F32)).astype(BF16)
    return _dot(hi, ones) + _dot(lo, ones)


def _fox_prep_kernel(fox_ref, ccol_ref, qt_ref, k_ref, vt_ref, stats_ref):
    ts = fox_ref.shape[0]
    lane = lax.broadcasted_iota(jnp.int32, (ts, HEAD_DIM), 1)
    stat_lane = lax.broadcasted_iota(jnp.int32, (1, LANES), 1)
    ones = jnp.ones((HEAD_DIM, LANES), BF16)
    ccol = ccol_ref[...] * LOG2E
    stats = []
    for h in range(FOX_HEADS):
        hi, mid, lo = _split3(ccol[:, h : h + 1])
        q_extra = jnp.where(lane == 0, hi, jnp.where(lane == 1, mid, jnp.where(lane == 2, lo, jnp.where(lane < 6, 1.0, 0.0))))
        k_extra = jnp.where(lane < 3, 1.0, jnp.where(lane == 3, -hi, jnp.where(lane == 4, -mid, jnp.where(lane == 5, -lo, 0.0))))
        v_extra = jnp.where(lane == 0, 1.0, 0.0)
        cols = slice(h * HEAD_DIM, (h + 1) * HEAD_DIM)
        q = fox_ref[:, cols].astype(F32)
        k = fox_ref[:, FOX_WIDTH + h * HEAD_DIM : FOX_WIDTH + (h + 1) * HEAD_DIM].astype(F32)
        v = fox_ref[:, 2 * FOX_WIDTH + h * HEAD_DIM : 2 * FOX_WIDTH + (h + 1) * HEAD_DIM].astype(F32)
        qt_ref[0, h] = jnp.concatenate([q, q_extra], axis=1).T.astype(BF16)
        k_ref[0, h] = jnp.concatenate([k, k_extra], axis=1).astype(BF16)
        vt_ref[0, h] = jnp.concatenate([v, v_extra], axis=1).T.astype(BF16)
        q_norm = jnp.sqrt(jnp.max(_row_sums(q * q, ones), axis=0, keepdims=True))
        k_norm = jnp.sqrt(jnp.max(_row_sums(k * k, ones), axis=0, keepdims=True))
        diag_min = jnp.min(_row_sums(q * k, ones), axis=0, keepdims=True)
        c_first = ccol[0:1, h : h + 1]
        c_last = ccol[ts - 1 : ts, h : h + 1]
        stats.append(jnp.where(stat_lane == 0, q_norm, jnp.where(stat_lane == 1, k_norm, jnp.where(
            stat_lane == 2, c_first, jnp.where(stat_lane == 3, c_last, diag_min)))))
    stats_ref[0, 0] = jnp.concatenate(stats, axis=0)


def _fox_prep(fox, ccol, bsz):
    m = fox.shape[0]
    s = m // bsz
    ts = FOX_TILE
    nt = s // ts
    return pl.pallas_call(
        _fox_prep_kernel,
        grid=(bsz, nt),
        in_specs=[
            pl.BlockSpec((ts, fox.shape[1]), lambda b, i: (b * nt + i, 0)),
            pl.BlockSpec((ts, LANES), lambda b, i: (b * nt + i, 0)),
        ],
        out_specs=[
            pl.BlockSpec((1, FOX_HEADS, LANES, ts), lambda b, i: (b, 0, 0, i)),
            pl.BlockSpec((1, FOX_HEADS, ts, LANES), lambda b, i: (b, 0, i, 0)),
            pl.BlockSpec((1, FOX_HEADS, LANES, ts), lambda b, i: (b, 0, 0, i)),
            pl.BlockSpec((1, 1, FOX_HEADS, LANES), lambda b, i: (b, i, 0, 0)),
        ],
        out_shape=[
            jax.ShapeDtypeStruct((bsz, FOX_HEADS, LANES, s), BF16),
            jax.ShapeDtypeStruct((bsz, FOX_HEADS, s, LANES), BF16),
            jax.ShapeDtypeStruct((bsz, FOX_HEADS, LANES, s), BF16),
            jax.ShapeDtypeStruct((bsz, nt, FOX_HEADS, LANES), F32),
        ],
        compiler_params=_params(("parallel", "parallel")),
        name="fox_prep",
    )(fox, ccol)


def _fox_kernel(qn_ref, kn_ref, cf_ref, cl_ref, dm_ref, qt_ref, k_ref, vt_ref, o_ref, m_scr, acc_scr):
    t = qt_ref.shape[-1]
    nt = k_ref.shape[2] // t
    qi = pl.program_id(2)

    def first_needed(h):
        base = ((pl.program_id(0) * (FOX_HEADS // 2) + pl.program_id(1)) * 2 + h) * nt
        q_norm = qn_ref[base + qi]
        c_first = cf_ref[base + qi]
        diag_min = dm_ref[base + qi]

        def needed(kb):
            j = base + jnp.maximum(kb, 0)
            dot_bound = q_norm * kn_ref[j]
            slack = SKIP_REL * (dot_bound + jnp.abs(diag_min) + jnp.abs(c_first) + jnp.abs(cl_ref[j]))
            return dot_bound + c_first - cl_ref[j] + slack >= diag_min - SKIP_LOG2

        kb = lax.while_loop(lambda kb: jnp.logical_and(kb >= 0, needed(kb)), lambda kb: kb - 1, qi - 1)
        return kb + 1

    def tile(h, kb, masked):
        start = pl.multiple_of(kb * t, t)
        s = _dot(k_ref[0, h, pl.ds(start, t), :], qt_ref[0, h])
        if masked:
            key = lax.broadcasted_iota(jnp.int32, (t, t), 0)
            qry = lax.broadcasted_iota(jnp.int32, (t, t), 1)
            s = jnp.where(key <= qry, s, NEG_BIG)
        return s, vt_ref[0, h, :, pl.ds(start, t)]

    for h in range(2):
        s, vt = tile(h, qi, True)
        m = jnp.max(s, axis=0, keepdims=True)
        m_scr[h] = m
        acc_scr[h] = _dot(vt, jnp.exp2(s - m).astype(BF16))

    def body(kb, carry):
        for h in range(2):
            s, vt = tile(h, kb, False)
            m_old = m_scr[h]
            m_new = jnp.maximum(m_old, jnp.max(s, axis=0, keepdims=True))
            m_scr[h] = m_new
            acc_scr[h] = jnp.exp2(m_old - m_new) * acc_scr[h] + _dot(vt, jnp.exp2(s - m_new).astype(BF16))
        return carry

    lax.fori_loop(jnp.minimum(first_needed(0), first_needed(1)), qi, body, 0)
    outs = []
    for h in range(2):
        acc = acc_scr[h]
        o = acc / acc[HEAD_DIM : HEAD_DIM + 1, :]
        outs.append(o.T[:, :HEAD_DIM])
    o_ref[...] = jnp.concatenate(outs, axis=1).astype(o_ref.dtype)


def _fox_attention(qt, k, vt, stats):
    bsz, nh, _, s = qt.shape
    t = FOX_TILE
    nq = s // t
    npair = nh // 2
    resident = pl.Buffered(1)
    st = stats[..., :5].transpose(0, 2, 1, 3)
    flat = lambda v: v.reshape(-1)
    scalars = (flat(st[..., 0]), flat(lax.cummax(st[..., 1], axis=2)), flat(st[..., 2]), flat(st[..., 3]), flat(st[..., 4]))
    return pl.pallas_call(
        _fox_kernel,
        grid_spec=pltpu.PrefetchScalarGridSpec(
            num_scalar_prefetch=len(scalars),
            grid=(bsz, npair, nq),
            in_specs=[
                pl.BlockSpec((1, 2, LANES, t), lambda b, p, i, *_: (b, p, 0, i)),
                pl.BlockSpec((1, 2, s, LANES), lambda b, p, i, *_: (b, p, 0, 0), pipeline_mode=resident),
                pl.BlockSpec((1, 2, LANES, s), lambda b, p, i, *_: (b, p, 0, 0), pipeline_mode=resident),
            ],
            out_specs=pl.BlockSpec((t, LANES), lambda b, p, i, *_: (b * nq + i, p)),
            scratch_shapes=[pltpu.VMEM((2, 1, t), F32), pltpu.VMEM((2, LANES, t), F32)],
        ),
        out_shape=jax.ShapeDtypeStruct((bsz * s, nh * HEAD_DIM), BF16),
        compiler_params=_params(("parallel", "parallel", "arbitrary")),
        name="fox_attention",
    )(*scalars, qt, k, vt)


def _lru_kernel(xy_ref, cw_ref, cb_ref, wg_ref, bg_ref, lam_ref, o_ref, tail_scr, h_scr):
    w = LRU_WIDTH

    @pl.when(pl.program_id(1) == 0)
    def _():
        tail_scr[...] = jnp.zeros_like(tail_scr)
        h_scr[...] = jnp.zeros_like(h_scr)

    x = xy_ref[:, :w]
    y = xy_ref[:, w:]
    tail = tail_scr[...]
    row8 = lax.broadcasted_iota(jnp.int32, (SUBLANES, w), 0)
    conv = x * cw_ref[CONV_WIDTH - 1 : CONV_WIDTH, :] + cb_ref[...]
    for j in range(1, CONV_WIDTH):
        xs = pltpu.roll(x, j, 0)
        head = jnp.where(row8 < j, pltpu.roll(tail, j, 0), xs[:SUBLANES])
        xs = jnp.concatenate([head, xs[SUBLANES:]], axis=0)
        conv = conv + xs * cw_ref[CONV_WIDTH - 1 - j : CONV_WIDTH - j, :]
    tail_scr[...] = x[-SUBLANES:]

    gates = jax.nn.sigmoid(_dot(conv.astype(BF16), wg_ref[...]) + bg_ref[...])
    r = gates[:, :w]
    i = gates[:, w:]
    log_a = -LRU_C * r * jax.nn.softplus(-lam_ref[...])
    a = jnp.exp(log_a)
    th = jnp.tanh(log_a)
    u = jnp.sqrt(-2.0 * th / (1.0 - th)) * (i * conv)
    a_run, h = _scan_rows(a, u)
    h = h + a_run * h_scr[...]
    h_scr[...] = h[-1:, :]
    o_ref[...] = (h * jax.nn.gelu(y)).astype(o_ref.dtype)


def _rg_lru(xy, cw, cb, wg, bg, lam, bsz):
    m = xy.shape[0]
    ts = SCAN_TILE
    nt = m // bsz // ts
    w = LRU_WIDTH
    return pl.pallas_call(
        _lru_kernel,
        grid=(bsz, nt),
        in_specs=[
            pl.BlockSpec((ts, 2 * w), lambda b, i: (b * nt + i, 0)),
            _full(cw.shape), _full(cb.shape), _full(wg.shape), _full(bg.shape), _full(lam.shape),
        ],
        out_specs=pl.BlockSpec((ts, w), lambda b, i: (b * nt + i, 0)),
        out_shape=jax.ShapeDtypeStruct((m, w), BF16),
        scratch_shapes=[pltpu.VMEM((SUBLANES, w), F32), pltpu.VMEM((1, w), F32)],
        compiler_params=_params(("parallel", "arbitrary")),
        name="rg_lru",
    )(xy, cw, cb, wg, bg, lam)


def _rope_kernel(pos_ref, freq_ref, cos_ref, sin_ref):
    ang = pos_ref[...].astype(F32) * freq_ref[...]
    lane = lax.broadcasted_iota(jnp.int32, ang.shape, 1)
    first_half = (lane % HEAD_DIM) < HEAD_DIM // 2
    cos_ref[...] = jnp.cos(ang)
    s = jnp.sin(ang)
    sin_ref[...] = jnp.where(first_half, -s, s)


def _rope_tables(pos, freq):
    m = pos.shape[0]
    tm = ROW_TILE
    spec = pl.BlockSpec((tm, LANES), lambda i: (i, 0))
    return pl.pallas_call(
        _rope_kernel,
        grid=(m // tm,),
        in_specs=[spec, _full(freq.shape)],
        out_specs=[spec, spec],
        out_shape=[jax.ShapeDtypeStruct((m, LANES), F32)] * 2,
        compiler_params=_params(("parallel",)),
        name="rope_tables",
    )(pos, freq)


def _group_mean(z, avg):
    hi, mid, lo = _split3(z)
    return _dot(hi.astype(BF16), avg) + _dot(mid.astype(BF16), avg) + _dot(lo.astype(BF16), avg)


def _ret_kernel(x_ref, cos_ref, sin_ref, decay_ref, kw_ref, qw_ref, cd_ref, avg_ref, o_ref, state_scr):
    c = RET_CHUNK
    w = RET_WIDTH
    npair = RET_HEADS // 2

    @pl.when(pl.program_id(1) == 0)
    def _():
        state_scr[...] = jnp.zeros_like(state_scr)

    cos = cos_ref[...]
    sin = sin_ref[...]
    lane = lax.broadcasted_iota(jnp.int32, cos.shape, 1)
    first_half = (lane % HEAD_DIM) < HEAD_DIM // 2
    low_head = lax.broadcasted_iota(jnp.int32, (c, LANES), 1) < HEAD_DIM
    head_mask = (low_head, jnp.logical_not(low_head))

    def rotary(z):
        swapped = jnp.where(first_half, pltpu.roll(z, LANES - HEAD_DIM // 2, 1), pltpu.roll(z, HEAD_DIM // 2, 1))
        return z * cos + swapped * sin

    ys = []
    for p in range(npair):
        cols = slice(p * LANES, (p + 1) * LANES)
        q = rotary(x_ref[:, cols])
        k = rotary(x_ref[:, w + p * LANES : w + (p + 1) * LANES]) * HEAD_DIM**-0.5
        v = x_ref[:, 2 * w + p * LANES : 2 * w + (p + 1) * LANES].astype(BF16)
        state = state_scr[p]
        chunks = []
        for n in range(x_ref.shape[0] // c):
            rows = slice(n * c, (n + 1) * c)
            qn, kn, vn = q[rows], k[rows], v[rows]
            kb = kn.astype(BF16)
            inner = []
            for h in range(2):
                qm = jnp.where(head_mask[h], qn, 0.0).astype(BF16)
                scores = _dot_nt(qm, kb) * decay_ref[2 * p + h]
                inner.append(_dot(scores.astype(BF16), vn))
            y = jnp.where(low_head, inner[0], inner[1])
            y = y + _dot((qn * qw_ref[:, cols]).astype(BF16), state.astype(BF16))
            u = _dot_tn((kn * kw_ref[:, cols]).astype(BF16), vn)
            cd = cd_ref[p]
            state = cd * state + jnp.where(cd > 0.0, u, 0.0)
            chunks.append(y)
        state_scr[p] = state
        ys.append(jnp.concatenate(chunks, axis=0))
    y = jnp.concatenate(ys, axis=1)
    avg = avg_ref[...]
    mu = _group_mean(y, avg)
    d = y - mu
    var = _group_mean(d * d, avg)
    yn = d * lax.rsqrt(var + EPS)
    g = x_ref[:, 3 * w :]
    o_ref[...] = (g * jax.nn.sigmoid(g) * yn).astype(o_ref.dtype)


def _retention(ret, cos, sin, decay, kw, qw, cd, avg, bsz):
    m = ret.shape[0]
    ts = ROW_TILE
    nt = m // bsz // ts
    w = RET_WIDTH
    return pl.pallas_call(
        _ret_kernel,
        grid=(bsz, nt),
        in_specs=[
            pl.BlockSpec((ts, 4 * w), lambda b, i: (b * nt + i, 0)),
            pl.BlockSpec((ts, LANES), lambda b, i: (b * nt + i, 0)),
            pl.BlockSpec((ts, LANES), lambda b, i: (b * nt + i, 0)),
            _full(decay.shape), _full(kw.shape), _full(qw.shape), _full(cd.shape), _full(avg.shape),
        ],
        out_specs=pl.BlockSpec((ts, w), lambda b, i: (b * nt + i, 0)),
        out_shape=jax.ShapeDtypeStruct((m, w), BF16),
        scratch_shapes=[pltpu.VMEM((RET_HEADS // 2, LANES, LANES), F32)],
        compiler_params=_params(("parallel", "arbitrary")),
        name="retention",
    )(ret, cos, sin, decay, kw, qw, cd, avg)


def _outproj_kernel(x_ref, lru_ref, fox_ref, ret_ref, w_ref, g_ref, o_ref):
    a, b = LRU_WIDTH, LRU_WIDTH + FOX_WIDTH
    y = _dot(lru_ref[...], w_ref[:a, :]) + _dot(fox_ref[...], w_ref[a:b, :]) + _dot(ret_ref[...], w_ref[b:, :])
    o_ref[...] = x_ref[...] + _rms(y, g_ref[...])


def _outproj(x, lru, fox, ret, w, g):
    m, d = x.shape
    tm = ROW_TILE
    row = lambda n: pl.BlockSpec((tm, n), lambda i: (i, 0))
    return pl.pallas_call(
        _outproj_kernel,
        grid=(m // tm,),
        in_specs=[row(d), row(lru.shape[1]), row(fox.shape[1]), row(ret.shape[1]), _full(w.shape), _full(g.shape)],
        out_specs=row(d),
        out_shape=jax.ShapeDtypeStruct((m, d), F32),
        compiler_params=_params(("parallel",)),
        name="mixer_outproj",
    )(x, lru, fox, ret, w, g)


def _norm_matmul_kernel(x_ref, g_ref, w_ref, o_ref):
    o_ref[...] = _dot(_rms(x_ref[...], g_ref[...]).astype(BF16), w_ref[...]).astype(o_ref.dtype)


def _norm_matmul(x, g, w, tm):
    m, d = x.shape
    n = w.shape[1]
    return pl.pallas_call(
        _norm_matmul_kernel,
        grid=(m // tm,),
        in_specs=[pl.BlockSpec((tm, d), lambda i: (i, 0)), _full(g.shape), _full(w.shape)],
        out_specs=pl.BlockSpec((tm, n), lambda i: (i, 0)),
        out_shape=jax.ShapeDtypeStruct((m, n), BF16),
        compiler_params=_params(("parallel",)),
        name="memory_kv_proj",
    )(x, g, w)


def _cross_kernel(x_ref, g1_ref, wq_ref, kt_ref, v_ref, wo_ref, g2_ref, o_ref):
    x = x_ref[...]
    d = x.shape[1]
    hd = d // CROSS_HEADS
    q = _dot(_rms(x, g1_ref[...]).astype(BF16), wq_ref[...]).astype(BF16)
    outs = []
    for h in range(CROSS_HEADS):
        cols = slice(h * hd, (h + 1) * hd)
        s = _dot(q[:, cols], kt_ref[0, cols, :]) * hd**-0.5
        e = jnp.exp(s - jnp.max(s, axis=-1, keepdims=True))
        p = e / jnp.sum(e, axis=-1, keepdims=True)
        outs.append(_dot(p.astype(BF16), v_ref[0, :, cols]).astype(BF16))
    o = jnp.concatenate(outs, axis=1)
    o_ref[...] = x + _rms(_dot(o, wo_ref[...]), g2_ref[...])


def _cross_attention(x, g1, wq, kt, v, wo, g2, bsz):
    m, d = x.shape
    tm = ROW_TILE
    nt = m // bsz // tm
    mem_len = v.shape[1]
    return pl.pallas_call(
        _cross_kernel,
        grid=(m // tm,),
        in_specs=[
            pl.BlockSpec((tm, d), lambda i: (i, 0)),
            _full(g1.shape), _full(wq.shape),
            pl.BlockSpec((1, d, mem_len), lambda i: (i // nt, 0, 0)),
            pl.BlockSpec((1, mem_len, d), lambda i: (i // nt, 0, 0)),
            _full(wo.shape), _full(g2.shape),
        ],
        out_specs=pl.BlockSpec((tm, d), lambda i: (i, 0)),
        out_shape=jax.ShapeDtypeStruct((m, d), F32),
        compiler_params=_params(("parallel",)),
        name="memory_cross_attention",
    )(x, g1, wq, kt, v, wo, g2)


def _ffn_kernel(x_ref, g1_ref, wg_ref, wu_ref, wd_ref, g2_ref, o_ref, h_scr, acc_scr):
    j = pl.program_id(1)

    @pl.when(j == 0)
    def _():
        h_scr[...] = _rms(x_ref[...], g1_ref[...]).astype(BF16)
        acc_scr[...] = jnp.zeros_like(acc_scr)

    h = h_scr[...]
    gate = _dot(h, wg_ref[...])
    up = _dot(h, wu_ref[...])
    act = (gate * jax.nn.sigmoid(gate) * up).astype(BF16)
    acc_scr[...] += _dot(act, wd_ref[...])

    @pl.when(j == pl.num_programs(1) - 1)
    def _():
        o_ref[...] = x_ref[...] + _rms(acc_scr[...], g2_ref[...])


def _ffn(x, g1, wg, wu, wd, g2):
    m, d = x.shape
    dff = wg.shape[1]
    tm = ROW_TILE
    nf = 2
    tf = dff // nf
    assert tf * nf == dff and tf % LANES == 0
    return pl.pallas_call(
        _ffn_kernel,
        grid=(m // tm, nf),
        in_specs=[
            pl.BlockSpec((tm, d), lambda i, j: (i, 0)),
            _full(g1.shape),
            pl.BlockSpec((d, tf), lambda i, j: (0, j)),
            pl.BlockSpec((d, tf), lambda i, j: (0, j)),
            pl.BlockSpec((tf, d), lambda i, j: (j, 0)),
            _full(g2.shape),
        ],
        out_specs=pl.BlockSpec((tm, d), lambda i, j: (i, 0)),
        out_shape=jax.ShapeDtypeStruct((m, d), F32),
        scratch_shapes=[pltpu.VMEM((tm, d), BF16), pltpu.VMEM((tm, d), F32)],
        compiler_params=_params(("parallel", "arbitrary")),
        name="swiglu_ffn",
    )(x, g1, wg, wu, wd, g2)


def _block_diag(w):
    n, a, b = w.shape
    out = jnp.zeros((n * a, n * b), w.dtype)
    for i in range(n):
        out = out.at[i * a : (i + 1) * a, i * b : (i + 1) * b].set(w[i])
    return out


def _retention_tables():
    c = RET_CHUNK
    log_gamma = jnp.log1p(-jnp.exp2(-5.0 - jnp.arange(RET_HEADS, dtype=F32)))
    idx = jnp.arange(c, dtype=F32)
    diff = idx[:, None] - idx[None, :]
    decay = jnp.where(diff >= 0, jnp.exp(log_gamma[:, None, None] * jnp.maximum(diff, 0.0)), 0.0)
    k_w = jnp.exp(log_gamma[None, :] * (c - 1.0 - idx)[:, None])
    q_w = jnp.exp(log_gamma[None, :] * (idx + 1.0)[:, None])
    chunk_decay = jnp.exp(log_gamma * c)
    kw = jnp.repeat(k_w, HEAD_DIM, axis=1)
    qw = jnp.repeat(q_w, HEAD_DIM, axis=1)
    eye = jnp.eye(2, dtype=F32)
    cd = jnp.stack([
        jnp.kron(eye * chunk_decay[2 * p : 2 * p + 2][None, :], jnp.ones((HEAD_DIM, HEAD_DIM), F32))
        for p in range(RET_HEADS // 2)
    ])
    avg = jnp.kron(jnp.eye(RET_HEADS, dtype=F32), jnp.full((HEAD_DIM, HEAD_DIM), 1.0 / HEAD_DIM, F32)).astype(BF16)
    return decay, kw, qw, cd, avg


def kernel(x, mem, positions, pre_mix_g, post_mix_g, w_in, conv_w, conv_b, w_rg, b_rg, w_ig, b_ig, lru_lambda, fox_b_f, w_out, pre_cross_g, post_cross_g, mem_norm_g, w_cq, w_ck, w_cv, w_co, pre_ffn_g, post_ffn_g, w_gate, w_up, w_down):
    bsz, seq, d = x.shape
    depth = w_in.shape[0]
    m = bsz * seq
    mem_len = mem.shape[1]
    xf = x.reshape(m, d)
    row = lambda v: v.reshape(1, -1)

    half = HEAD_DIM // 2
    inv_freq = ROPE_THETA ** (-jnp.arange(half, dtype=F32) / half)
    freq = jnp.tile(inv_freq, LANES // half).reshape(1, LANES)
    pos = jnp.broadcast_to(positions.reshape(m, 1), (m, LANES))
    cos, sin = _rope_tables(pos, freq)
    decay, kw, qw, cd, avg = _retention_tables()

    o_fox = 2 * LRU_WIDTH
    o_ff = o_fox + 3 * FOX_WIDTH
    o_ret = o_ff + FOX_HEADS

    for l in range(depth):
        wl = w_in[l, :, :o_fox].astype(BF16)
        wf = w_in[l, :, o_fox:o_ff].astype(BF16)
        wff = jnp.pad(w_in[l, :, o_ff:o_ret], ((0, 0), (0, LANES - FOX_HEADS))).astype(BF16)
        wr = w_in[l, :, o_ret:].astype(BF16)
        lru, fox, ret, ff = _inproj(xf, row(pre_mix_g[l]), wl, wf, wr, wff)

        bf = jnp.pad(fox_b_f[l], (0, LANES - FOX_HEADS)).reshape(1, LANES)
        ccol = _fox_cumsum(ff, bf, bsz)
        fox_o = _fox_attention(*_fox_prep(fox, ccol, bsz))

        wg = jnp.concatenate([_block_diag(w_rg[l]), _block_diag(w_ig[l])], axis=1).astype(BF16)
        bg = jnp.concatenate([b_rg[l], b_ig[l]]).reshape(1, -1)
        lru_o = _rg_lru(lru, conv_w[l], row(conv_b[l]), wg, bg, row(lru_lambda[l]), bsz)

        ret_o = _retention(ret, cos, sin, decay, kw, qw, cd, avg, bsz)

        xf = _outproj(xf, lru_o, fox_o, ret_o, w_out[l].astype(BF16), row(post_mix_g[l]))

        wkv = jnp.concatenate([w_ck[l], w_cv[l]], axis=1).astype(BF16)
        kv = _norm_matmul(mem.reshape(bsz * mem_len, d), row(mem_norm_g), wkv, mem_len)
        kt = kv[:, :d].reshape(bsz, mem_len, d).transpose(0, 2, 1)
        vv = kv[:, d:].reshape(bsz, mem_len, d)
        xf = _cross_attention(xf, row(pre_cross_g[l]), w_cq[l].astype(BF16), kt, vv, w_co[l].astype(BF16), row(post_cross_g[l]), bsz)

        xf = _ffn(xf, row(pre_ffn_g[l]), w_gate[l].astype(BF16), w_up[l].astype(BF16), w_down[l].astype(BF16), row(post_ffn_g[l]))
    return xf.reshape(bsz, seq, d)
```

```python
import functools

import jax
import jax.numpy as jnp
from jax import lax
from jax.experimental import pallas as pl
from jax.experimental.pallas import tpu as pltpu

F32 = jnp.float32
BF16 = jnp.bfloat16

HEAD_DIM = 64
LRU_WIDTH = 256
LRU_BLOCKS = 4
CONV_WIDTH = 4
LRU_C = 8.0
FOX_HEADS = 8
FOX_WIDTH = FOX_HEADS * HEAD_DIM
RET_HEADS = 4
RET_WIDTH = RET_HEADS * HEAD_DIM
CROSS_HEADS = 4
RET_CHUNK = 128
ROPE_THETA = 10000.0
EPS = 1e-6

LANES = 128
SUBLANES = 8
VMEM_LIMIT = 48 * 1024 * 1024

ROW_TILE = 512
FOX_TQ = 1024
FOX_TK = 512
SCAN_TILE = 256
NEG_BIG = -1e30
LOG2E = 1.4426950408889634
SKIP_LOG2 = 160.0
SKIP_REL = 2.0**-8


def _params(sem):
    return pltpu.CompilerParams(dimension_semantics=sem, vmem_limit_bytes=VMEM_LIMIT)


def _rms(x, g):
    ms = jnp.mean(x * x, axis=-1, keepdims=True)
    return x * lax.rsqrt(ms + EPS) * g


def _dot(a, b):
    return jnp.dot(a, b, preferred_element_type=F32)


def _dot_nt(a, b):
    return lax.dot_general(a, b, (((1,), (1,)), ((), ())), preferred_element_type=F32)


def _dot_tn(a, b):
    return lax.dot_general(a, b, (((0,), (0,)), ((), ())), preferred_element_type=F32)


def _full(shape):
    nd = len(shape)
    return pl.BlockSpec(shape, lambda *_: (0,) * nd)


def _scan_rows(a, b):
    n = b.shape[0]
    row = lax.broadcasted_iota(jnp.int32, b.shape, 0)
    k = 1
    while k < n:
        keep = row >= k
        b_prev = jnp.where(keep, pltpu.roll(b, k, 0), 0.0)
        if a is None:
            b = b + b_prev
        else:
            a_prev = jnp.where(keep, pltpu.roll(a, k, 0), 1.0)
            b = a * b_prev + b
            a = a * a_prev
        k *= 2
    return a, b


def _inproj_kernel(x_ref, g_ref, wl_ref, wf_ref, wr_ref, wff_ref, lru_ref, fox_ref, ret_ref, ff_ref):
    h = _rms(x_ref[...], g_ref[...]).astype(BF16)
    lru_ref[...] = _dot(h, wl_ref[...])
    fox_ref[:, :FOX_WIDTH] = (_dot(h, wf_ref[:, :FOX_WIDTH]) * (HEAD_DIM**-0.5 * LOG2E)).astype(BF16)
    fox_ref[:, FOX_WIDTH:] = _dot(h, wf_ref[:, FOX_WIDTH:]).astype(BF16)
    ret_ref[...] = _dot(h, wr_ref[...])
    ff_ref[...] = _dot(h, wff_ref[...])


def _inproj(x, g, wl, wf, wr, wff):
    m, d = x.shape
    tm = ROW_TILE
    row = lambda n: pl.BlockSpec((tm, n), lambda i: (i, 0))
    return pl.pallas_call(
        _inproj_kernel,
        grid=(m // tm,),
        in_specs=[row(d), _full(g.shape), _full(wl.shape), _full(wf.shape), _full(wr.shape), _full(wff.shape)],
        out_specs=[row(wl.shape[1]), row(wf.shape[1]), row(wr.shape[1]), row(wff.shape[1])],
        out_shape=[
            jax.ShapeDtypeStruct((m, wl.shape[1]), F32),
            jax.ShapeDtypeStruct((m, wf.shape[1]), BF16),
            jax.ShapeDtypeStruct((m, wr.shape[1]), F32),
            jax.ShapeDtypeStruct((m, wff.shape[1]), F32),
        ],
        compiler_params=_params(("parallel",)),
        name="mixer_inproj",
    )(x, g, wl, wf, wr, wff)


def _cumsum_kernel(ff_ref, bf_ref, c_ref, carry_ref):
    @pl.when(pl.program_id(1) == 0)
    def _():
        carry_ref[...] = jnp.zeros_like(carry_ref)

    log_f = jax.nn.log_sigmoid(ff_ref[...] + bf_ref[...])
    _, c = _scan_rows(None, log_f)
    c = c + carry_ref[...]
    c_ref[...] = c
    carry_ref[...] = c[-1:, :]


def _fox_cumsum(ff, bf, bsz):
    m, n = ff.shape
    ts = SCAN_TILE
    nt = m // bsz // ts
    return pl.pallas_call(
        _cumsum_kernel,
        grid=(bsz, nt),
        in_specs=[pl.BlockSpec((ts, n), lambda b, i: (b * nt + i, 0)), _full(bf.shape)],
        out_specs=pl.BlockSpec((ts, n), lambda b, i: (b * nt + i, 0)),
        out_shape=jax.ShapeDtypeStruct((m, n), F32),
        scratch_shapes=[pltpu.VMEM((1, n), F32)],
        compiler_params=_params(("parallel", "arbitrary")),
        name="fox_gate_cumsum",
    )(ff, bf)


def _split3(z):
    hi = z.astype(BF16).astype(F32)
    mid = (z - hi).astype(BF16).astype(F32)
    lo = (z - hi - mid).astype(BF16).astype(F32)
    return hi, mid, lo


def _row_sums(z, ones):
    hi = z.astype(BF16)
    lo = (z - hi.astype(F32)).astype(BF16)
    return _dot(hi, ones) + _dot(lo, ones)


def _fox_prep_kernel(fox_ref, ccol_ref, qt_ref, k_ref, vt_ref, stats_ref):
    ts = fox_ref.shape[0]
    lane = lax.broadcasted_iota(jnp.int32, (ts, HEAD_DIM), 1)
    stat_lane = lax.broadcasted_iota(jnp.int32, (1, LANES), 1)
    ones = jnp.ones((HEAD_DIM, LANES), BF16)
    ccol = ccol_ref[...] * LOG2E
    stats = []
    for h in range(FOX_HEADS):
        hi, mid, lo = _split3(ccol[:, h : h + 1])
        q_extra = jnp.where(lane == 0, hi, jnp.where(lane == 1, mid, jnp.where(lane == 2, lo, jnp.where(lane < 6, 1.0, 0.0))))
        k_extra = jnp.where(lane < 3, 1.0, jnp.where(lane == 3, -hi, jnp.where(lane == 4, -mid, jnp.where(lane == 5, -lo, 0.0))))
        v_extra = jnp.where(lane == 0, 1.0, 0.0)
        cols = slice(h * HEAD_DIM, (h + 1) * HEAD_DIM)
        q = fox_ref[:, cols].astype(F32)
        k = fox_ref[:, FOX_WIDTH + h * HEAD_DIM : FOX_WIDTH + (h + 1) * HEAD_DIM].astype(F32)
        v = fox_ref[:, 2 * FOX_WIDTH + h * HEAD_DIM : 2 * FOX_WIDTH + (h + 1) * HEAD_DIM].astype(F32)
        qt_ref[0, h] = jnp.concatenate([q, q_extra], axis=1).T.astype(BF16)
        k_ref[0, h] = jnp.concatenate([k, k_extra], axis=1).astype(BF16)
        vt_ref[0, h] = jnp.concatenate([v, v_extra], axis=1).T.astype(BF16)
        q_norm = jnp.sqrt(jnp.max(_row_sums(q * q, ones), axis=0, keepdims=True))
        k_norm = jnp.sqrt(jnp.max(_row_sums(k * k, ones), axis=0, keepdims=True))
        diag_min = jnp.min(_row_sums(q * k, ones), axis=0, keepdims=True)
        c_first = ccol[0:1, h : h + 1]
        c_last = ccol[ts - 1 : ts, h : h + 1]
        stats.append(jnp.where(stat_lane == 0, q_norm, jnp.where(stat_lane == 1, k_norm, jnp.where(
            stat_lane == 2, c_first, jnp.where(stat_lane == 3, c_last, diag_min)))))
    stats_ref[0, 0] = jnp.concatenate(stats, axis=0)


def _fox_prep(fox, ccol, bsz):
    m = fox.shape[0]
    s = m // bsz
    ts = FOX_TK
    nt = s // ts
    return pl.pallas_call(
        _fox_prep_kernel,
        grid=(bsz, nt),
        in_specs=[
            pl.BlockSpec((ts, fox.shape[1]), lambda b, i: (b * nt + i, 0)),
            pl.BlockSpec((ts, LANES), lambda b, i: (b * nt + i, 0)),
        ],
        out_specs=[
            pl.BlockSpec((1, FOX_HEADS, LANES, ts), lambda b, i: (b, 0, 0, i)),
            pl.BlockSpec((1, FOX_HEADS, ts, LANES), lambda b, i: (b, 0, i, 0)),
            pl.BlockSpec((1, FOX_HEADS, LANES, ts), lambda b, i: (b, 0, 0, i)),
            pl.BlockSpec((1, 1, FOX_HEADS, LANES), lambda b, i: (b, i, 0, 0)),
        ],
        out_shape=[
            jax.ShapeDtypeStruct((bsz, FOX_HEADS, LANES, s), BF16),
            jax.ShapeDtypeStruct((bsz, FOX_HEADS, s, LANES), BF16),
            jax.ShapeDtypeStruct((bsz, FOX_HEADS, LANES, s), BF16),
            jax.ShapeDtypeStruct((bsz, nt, FOX_HEADS, LANES), F32),
        ],
        compiler_params=_params(("parallel", "parallel")),
        name="fox_prep",
    )(fox, ccol)


def _fox_kernel(qn_ref, kn_ref, cf_ref, cl_ref, dm_ref, qt_ref, k_ref, vt_ref, o_ref, m_scr, acc_scr, s_scr, mx_scr):
    tq = qt_ref.shape[-1]
    tk = FOX_TK
    sub = tq // tk
    nk = k_ref.shape[2] // tk
    nq = nk // sub
    qi = pl.program_id(2)

    def first_needed(h):
        head = (pl.program_id(0) * (FOX_HEADS // 2) + pl.program_id(1)) * 2 + h
        q_norm = qn_ref[head * nq + qi]
        c_first = cf_ref[head * nq + qi]
        diag_min = dm_ref[head * nq + qi]

        def needed(kb):
            j = head * nk + jnp.maximum(kb, 0)
            dot_bound = q_norm * kn_ref[j]
            slack = SKIP_REL * (dot_bound + jnp.abs(diag_min) + jnp.abs(c_first) + jnp.abs(cl_ref[j]))
            return dot_bound + c_first - cl_ref[j] + slack >= diag_min - SKIP_LOG2

        kb = lax.while_loop(lambda kb: jnp.logical_and(kb >= 0, needed(kb)), lambda kb: kb - 1, qi * sub - 1)
        return kb + 1

    def logits(h, kb, slot, diag):
        start = pl.multiple_of(kb * tk, tk)
        s = _dot(k_ref[0, h, pl.ds(start, tk), :], qt_ref[0, h])
        if diag is not None:
            key = lax.broadcasted_iota(jnp.int32, (tk, tq), 0) + diag * tk
            qry = lax.broadcasted_iota(jnp.int32, (tk, tq), 1)
            s = jnp.where(key <= qry, s, NEG_BIG)
        s_scr[slot] = s
        mx_scr[slot] = jnp.max(s, axis=0, keepdims=True)

    def accumulate(h, kb, slot):
        start = pl.multiple_of(kb * tk, tk)
        m_old = m_scr[h]
        m_new = jnp.maximum(m_old, mx_scr[slot])
        m_scr[h] = m_new
        p = jnp.exp2(s_scr[slot] - m_new).astype(BF16)
        acc_scr[h] = jnp.exp2(m_old - m_new) * acc_scr[h] + _dot(vt_ref[0, h, :, pl.ds(start, tk)], p)

    m_scr[...] = jnp.full_like(m_scr, NEG_BIG)
    acc_scr[...] = jnp.zeros_like(acc_scr)
    own = qi * sub
    lo = jnp.minimum(first_needed(0), first_needed(1))
    logits(0, own, 0, 0)
    logits(1, own, 1, 0)
    accumulate(0, own, 0)
    for u in range(1, sub):
        logits(0, own + u, 0, u)
        accumulate(1, own + u - 1, 1)
        logits(1, own + u, 1, u)
        accumulate(0, own + u, 0)
    last_own = own + sub - 1

    @pl.when(lo >= own)
    def _():
        accumulate(1, last_own, 1)

    @pl.when(lo < own)
    def _():
        logits(0, lo, 0, None)
        accumulate(1, last_own, 1)

        def body(kb, carry):
            logits(1, kb, 1, None)
            accumulate(0, kb, 0)
            logits(0, kb + 1, 0, None)
            accumulate(1, kb, 1)
            return carry

        lax.fori_loop(lo, own - 1, body, 0)
        logits(1, own - 1, 1, None)
        accumulate(0, own - 1, 0)
        accumulate(1, own - 1, 1)

    outs = []
    for h in range(2):
        acc = acc_scr[h]
        o = acc / acc[HEAD_DIM : HEAD_DIM + 1, :]
        outs.append(o.T[:, :HEAD_DIM])
    o_ref[...] = jnp.concatenate(outs, axis=1).astype(o_ref.dtype)


def _fox_attention(qt, k, vt, stats):
    bsz, nh, _, s = qt.shape
    tq = FOX_TQ
    sub = tq // FOX_TK
    nq = s // tq
    npair = nh // 2
    resident = pl.Buffered(1)
    st = stats[..., :5].transpose(0, 2, 1, 3)
    per_q = lambda v: v.reshape(bsz, nh, nq, sub)
    scalars = (
        per_q(st[..., 0]).max(axis=-1),
        lax.cummax(st[..., 1], axis=2),
        per_q(st[..., 2])[..., 0],
        st[..., 3],
        per_q(st[..., 4]).min(axis=-1),
    )
    scalars = tuple(v.reshape(-1) for v in scalars)
    return pl.pallas_call(
        _fox_kernel,
        grid_spec=pltpu.PrefetchScalarGridSpec(
            num_scalar_prefetch=len(scalars),
            grid=(bsz, npair, nq),
            in_specs=[
                pl.BlockSpec((1, 2, LANES, tq), lambda b, p, i, *_: (b, p, 0, i)),
                pl.BlockSpec((1, 2, s, LANES), lambda b, p, i, *_: (b, p, 0, 0), pipeline_mode=resident),
                pl.BlockSpec((1, 2, LANES, s), lambda b, p, i, *_: (b, p, 0, 0), pipeline_mode=resident),
            ],
            out_specs=pl.BlockSpec((tq, LANES), lambda b, p, i, *_: (b * nq + i, p)),
            scratch_shapes=[
                pltpu.VMEM((2, 1, tq), F32),
                pltpu.VMEM((2, LANES, tq), F32),
                pltpu.VMEM((2, FOX_TK, tq), F32),
                pltpu.VMEM((2, 1, tq), F32),
            ],
        ),
        out_shape=jax.ShapeDtypeStruct((bsz * s, nh * HEAD_DIM), BF16),
        compiler_params=_params(("parallel", "parallel", "arbitrary")),
        name="fox_attention",
    )(*scalars, qt, k, vt)


def _lru_kernel(xy_ref, cw_ref, cb_ref, wg_ref, bg_ref, lam_ref, o_ref, tail_scr, h_scr):
    w = LRU_WIDTH

    @pl.when(pl.program_id(1) == 0)
    def _():
        tail_scr[...] = jnp.zeros_like(tail_scr)
        h_scr[...] = jnp.zeros_like(h_scr)

    x = xy_ref[:, :w]
    y = xy_ref[:, w:]
    tail = tail_scr[...]
    row8 = lax.broadcasted_iota(jnp.int32, (SUBLANES, w), 0)
    conv = x * cw_ref[CONV_WIDTH - 1 : CONV_WIDTH, :] + cb_ref[...]
    for j in range(1, CONV_WIDTH):
        xs = pltpu.roll(x, j, 0)
        head = jnp.where(row8 < j, pltpu.roll(tail, j, 0), xs[:SUBLANES])
        xs = jnp.concatenate([head, xs[SUBLANES:]], axis=0)
        conv = conv + xs * cw_ref[CONV_WIDTH - 1 - j : CONV_WIDTH - j, :]
    tail_scr[...] = x[-SUBLANES:]

    gates = jax.nn.sigmoid(_dot(conv.astype(BF16), wg_ref[...]) + bg_ref[...])
    r = gates[:, :w]
    i = gates[:, w:]
    log_a = -LRU_C * r * jax.nn.softplus(-lam_ref[...])
    a = jnp.exp(log_a)
    th = jnp.tanh(log_a)
    u = jnp.sqrt(-2.0 * th / (1.0 - th)) * (i * conv)
    a_run, h = _scan_rows(a, u)
    h = h + a_run * h_scr[...]
    h_scr[...] = h[-1:, :]
    o_ref[...] = (h * jax.nn.gelu(y)).astype(o_ref.dtype)


def _rg_lru(xy, cw, cb, wg, bg, lam, bsz):
    m = xy.shape[0]
    ts = SCAN_TILE
    nt = m // bsz // ts
    w = LRU_WIDTH
    return pl.pallas_call(
        _lru_kernel,
        grid=(bsz, nt),
        in_specs=[
            pl.BlockSpec((ts, 2 * w), lambda b, i: (b * nt + i, 0)),
            _full(cw.shape), _full(cb.shape), _full(wg.shape), _full(bg.shape), _full(lam.shape),
        ],
        out_specs=pl.BlockSpec((ts, w), lambda b, i: (b * nt + i, 0)),
        out_shape=jax.ShapeDtypeStruct((m, w), BF16),
        scratch_shapes=[pltpu.VMEM((SUBLANES, w), F32), pltpu.VMEM((1, w), F32)],
        compiler_params=_params(("parallel", "arbitrary")),
        name="rg_lru",
    )(xy, cw, cb, wg, bg, lam)


def _rope_kernel(pos_ref, freq_ref, cos_ref, sin_ref):
    ang = pos_ref[...].astype(F32) * freq_ref[...]
    lane = lax.broadcasted_iota(jnp.int32, ang.shape, 1)
    first_half = (lane % HEAD_DIM) < HEAD_DIM // 2
    cos_ref[...] = jnp.cos(ang)
    s = jnp.sin(ang)
    sin_ref[...] = jnp.where(first_half, -s, s)


def _rope_tables(pos, freq):
    m = pos.shape[0]
    tm = ROW_TILE
    spec = pl.BlockSpec((tm, LANES), lambda i: (i, 0))
    return pl.pallas_call(
        _rope_kernel,
        grid=(m // tm,),
        in_specs=[spec, _full(freq.shape)],
        out_specs=[spec, spec],
        out_shape=[jax.ShapeDtypeStruct((m, LANES), F32)] * 2,
        compiler_params=_params(("parallel",)),
        name="rope_tables",
    )(pos, freq)


def _group_mean(z, avg):
    hi, mid, lo = _split3(z)
    return _dot(hi.astype(BF16), avg) + _dot(mid.astype(BF16), avg) + _dot(lo.astype(BF16), avg)


def _ret_kernel(x_ref, cos_ref, sin_ref, decay_ref, kw_ref, qw_ref, cd_ref, avg_ref, o_ref, state_scr):
    c = RET_CHUNK
    w = RET_WIDTH
    npair = RET_HEADS // 2

    @pl.when(pl.program_id(1) == 0)
    def _():
        state_scr[...] = jnp.zeros_like(state_scr)

    cos = cos_ref[...]
    sin = sin_ref[...]
    lane = lax.broadcasted_iota(jnp.int32, cos.shape, 1)
    first_half = (lane % HEAD_DIM) < HEAD_DIM // 2
    low_head = lax.broadcasted_iota(jnp.int32, (c, LANES), 1) < HEAD_DIM
    head_mask = (low_head, jnp.logical_not(low_head))

    def rotary(z):
        swapped = jnp.where(first_half, pltpu.roll(z, LANES - HEAD_DIM // 2, 1), pltpu.roll(z, HEAD_DIM // 2, 1))
        return z * cos + swapped * sin

    ys = []
    for p in range(npair):
        cols = slice(p * LANES, (p + 1) * LANES)
        q = rotary(x_ref[:, cols])
        k = rotary(x_ref[:, w + p * LANES : w + (p + 1) * LANES]) * HEAD_DIM**-0.5
        v = x_ref[:, 2 * w + p * LANES : 2 * w + (p + 1) * LANES].astype(BF16)
        state = state_scr[p]
        chunks = []
        for n in range(x_ref.shape[0] // c):
            rows = slice(n * c, (n + 1) * c)
            qn, kn, vn = q[rows], k[rows], v[rows]
            kb = kn.astype(BF16)
            inner = []
            for h in range(2):
                qm = jnp.where(head_mask[h], qn, 0.0).astype(BF16)
                scores = _dot_nt(qm, kb) * decay_ref[2 * p + h]
                inner.append(_dot(scores.astype(BF16), vn))
            y = jnp.where(low_head, inner[0], inner[1])
            y = y + _dot((qn * qw_ref[:, cols]).astype(BF16), state.astype(BF16))
            u = _dot_tn((kn * kw_ref[:, cols]).astype(BF16), vn)
            cd = cd_ref[p]
            state = cd * state + jnp.where(cd > 0.0, u, 0.0)
            chunks.append(y)
        state_scr[p] = state
        ys.append(jnp.concatenate(chunks, axis=0))
    y = jnp.concatenate(ys, axis=1)
    avg = avg_ref[...]
    mu = _group_mean(y, avg)
    d = y - mu
    var = _group_mean(d * d, avg)
    yn = d * lax.rsqrt(var + EPS)
    g = x_ref[:, 3 * w :]
    o_ref[...] = (g * jax.nn.sigmoid(g) * yn).astype(o_ref.dtype)


def _retention(ret, cos, sin, decay, kw, qw, cd, avg, bsz):
    m = ret.shape[0]
    ts = ROW_TILE
    nt = m // bsz // ts
    w = RET_WIDTH
    return pl.pallas_call(
        _ret_kernel,
        grid=(bsz, nt),
        in_specs=[
            pl.BlockSpec((ts, 4 * w), lambda b, i: (b * nt + i, 0)),
            pl.BlockSpec((ts, LANES), lambda b, i: (b * nt + i, 0)),
            pl.BlockSpec((ts, LANES), lambda b, i: (b * nt + i, 0)),
            _full(decay.shape), _full(kw.shape), _full(qw.shape), _full(cd.shape), _full(avg.shape),
        ],
        out_specs=pl.BlockSpec((ts, w), lambda b, i: (b * nt + i, 0)),
        out_shape=jax.ShapeDtypeStruct((m, w), BF16),
        scratch_shapes=[pltpu.VMEM((RET_HEADS // 2, LANES, LANES), F32)],
        compiler_params=_params(("parallel", "arbitrary")),
        name="retention",
    )(ret, cos, sin, decay, kw, qw, cd, avg)


def _outproj_kernel(x_ref, lru_ref, fox_ref, ret_ref, w_ref, g_ref, o_ref):
    a, b = LRU_WIDTH, LRU_WIDTH + FOX_WIDTH
    y = _dot(lru_ref[...], w_ref[:a, :]) + _dot(fox_ref[...], w_ref[a:b, :]) + _dot(ret_ref[...], w_ref[b:, :])
    o_ref[...] = x_ref[...] + _rms(y, g_ref[...])


def _outproj(x, lru, fox, ret, w, g):
    m, d = x.shape
    tm = ROW_TILE
    row = lambda n: pl.BlockSpec((tm, n), lambda i: (i, 0))
    return pl.pallas_call(
        _outproj_kernel,
        grid=(m // tm,),
        in_specs=[row(d), row(lru.shape[1]), row(fox.shape[1]), row(ret.shape[1]), _full(w.shape), _full(g.shape)],
        out_specs=row(d),
        out_shape=jax.ShapeDtypeStruct((m, d), F32),
        compiler_params=_params(("parallel",)),
        name="mixer_outproj",
    )(x, lru, fox, ret, w, g)


def _norm_matmul_kernel(x_ref, g_ref, w_ref, o_ref):
    o_ref[...] = _dot(_rms(x_ref[...], g_ref[...]).astype(BF16), w_ref[...]).astype(o_ref.dtype)


def _norm_matmul(x, g, w, tm):
    m, d = x.shape
    n = w.shape[1]
    return pl.pallas_call(
        _norm_matmul_kernel,
        grid=(m // tm,),
        in_specs=[pl.BlockSpec((tm, d), lambda i: (i, 0)), _full(g.shape), _full(w.shape)],
        out_specs=pl.BlockSpec((tm, n), lambda i: (i, 0)),
        out_shape=jax.ShapeDtypeStruct((m, n), BF16),
        compiler_params=_params(("parallel",)),
        name="memory_kv_proj",
    )(x, g, w)


def _cross_kernel(x_ref, g1_ref, wq_ref, kt_ref, v_ref, wo_ref, g2_ref, o_ref):
    x = x_ref[...]
    d = x.shape[1]
    hd = d // CROSS_HEADS
    q = _dot(_rms(x, g1_ref[...]).astype(BF16), wq_ref[...]).astype(BF16)
    outs = []
    for h in range(CROSS_HEADS):
        cols = slice(h * hd, (h + 1) * hd)
        s = _dot(q[:, cols], kt_ref[0, cols, :]) * hd**-0.5
        e = jnp.exp(s - jnp.max(s, axis=-1, keepdims=True))
        p = e / jnp.sum(e, axis=-1, keepdims=True)
        outs.append(_dot(p.astype(BF16), v_ref[0, :, cols]).astype(BF16))
    o = jnp.concatenate(outs, axis=1)
    o_ref[...] = x + _rms(_dot(o, wo_ref[...]), g2_ref[...])


def _cross_attention(x, g1, wq, kt, v, wo, g2, bsz):
    m, d = x.shape
    tm = ROW_TILE
    nt = m // bsz // tm
    mem_len = v.shape[1]
    return pl.pallas_call(
        _cross_kernel,
        grid=(m // tm,),
        in_specs=[
            pl.BlockSpec((tm, d), lambda i: (i, 0)),
            _full(g1.shape), _full(wq.shape),
            pl.BlockSpec((1, d, mem_len), lambda i: (i // nt, 0, 0)),
            pl.BlockSpec((1, mem_len, d), lambda i: (i // nt, 0, 0)),
            _full(wo.shape), _full(g2.shape),
        ],
        out_specs=pl.BlockSpec((tm, d), lambda i: (i, 0)),
        out_shape=jax.ShapeDtypeStruct((m, d), F32),
        compiler_params=_params(("parallel",)),
        name="memory_cross_attention",
    )(x, g1, wq, kt, v, wo, g2)


def _ffn_kernel(x_ref, g1_ref, wg_ref, wu_ref, wd_ref, g2_ref, o_ref, h_scr, acc_scr):
    j = pl.program_id(1)

    @pl.when(j == 0)
    def _():
        h_scr[...] = _rms(x_ref[...], g1_ref[...]).astype(BF16)
        acc_scr[...] = jnp.zeros_like(acc_scr)

    h = h_scr[...]
    gate = _dot(h, wg_ref[...])
    up = _dot(h, wu_ref[...])
    act = (gate * jax.nn.sigmoid(gate) * up).astype(BF16)
    acc_scr[...] += _dot(act, wd_ref[...])

    @pl.when(j == pl.num_programs(1) - 1)
    def _():
        o_ref[...] = x_ref[...] + _rms(acc_scr[...], g2_ref[...])


def _ffn(x, g1, wg, wu, wd, g2):
    m, d = x.shape
    dff = wg.shape[1]
    tm = ROW_TILE
    nf = 2
    tf = dff // nf
    assert tf * nf == dff and tf % LANES == 0
    return pl.pallas_call(
        _ffn_kernel,
        grid=(m // tm, nf),
        in_specs=[
            pl.BlockSpec((tm, d), lambda i, j: (i, 0)),
            _full(g1.shape),
            pl.BlockSpec((d, tf), lambda i, j: (0, j)),
            pl.BlockSpec((d, tf), lambda i, j: (0, j)),
            pl.BlockSpec((tf, d), lambda i, j: (j, 0)),
            _full(g2.shape),
        ],
        out_specs=pl.BlockSpec((tm, d), lambda i, j: (i, 0)),
        out_shape=jax.ShapeDtypeStruct((m, d), F32),
        scratch_shapes=[pltpu.VMEM((tm, d), BF16), pltpu.VMEM((tm, d), F32)],
        compiler_params=_params(("parallel", "arbitrary")),
        name="swiglu_ffn",
    )(x, g1, wg, wu, wd, g2)


def _block_diag(w):
    n, a, b = w.shape
    out = jnp.zeros((n * a, n * b), w.dtype)
    for i in range(n):
        out = out.at[i * a : (i + 1) * a, i * b : (i + 1) * b].set(w[i])
    return out


def _retention_tables():
    c = RET_CHUNK
    log_gamma = jnp.log1p(-jnp.exp2(-5.0 - jnp.arange(RET_HEADS, dtype=F32)))
    idx = jnp.arange(c, dtype=F32)
    diff = idx[:, None] - idx[None, :]
    decay = jnp.where(diff >= 0, jnp.exp(log_gamma[:, None, None] * jnp.maximum(diff, 0.0)), 0.0)
    k_w = jnp.exp(log_gamma[None, :] * (c - 1.0 - idx)[:, None])
    q_w = jnp.exp(log_gamma[None, :] * (idx + 1.0)[:, None])
    chunk_decay = jnp.exp(log_gamma * c)
    kw = jnp.repeat(k_w, HEAD_DIM, axis=1)
    qw = jnp.repeat(q_w, HEAD_DIM, axis=1)
    eye = jnp.eye(2, dtype=F32)
    cd = jnp.stack([
        jnp.kron(eye * chunk_decay[2 * p : 2 * p + 2][None, :], jnp.ones((HEAD_DIM, HEAD_DIM), F32))
        for p in range(RET_HEADS // 2)
    ])
    avg = jnp.kron(jnp.eye(RET_HEADS, dtype=F32), jnp.full((HEAD_DIM, HEAD_DIM), 1.0 / HEAD_DIM, F32)).astype(BF16)
    return decay, kw, qw, cd, avg


def kernel(x, mem, positions, pre_mix_g, post_mix_g, w_in, conv_w, conv_b, w_rg, b_rg, w_ig, b_ig, lru_lambda, fox_b_f, w_out, pre_cross_g, post_cross_g, mem_norm_g, w_cq, w_ck, w_cv, w_co, pre_ffn_g, post_ffn_g, w_gate, w_up, w_down):
    bsz, seq, d = x.shape
    depth = w_in.shape[0]
    m = bsz * seq
    mem_len = mem.shape[1]
    xf = x.reshape(m, d)
    row = lambda v: v.reshape(1, -1)

    half = HEAD_DIM // 2
    inv_freq = ROPE_THETA ** (-jnp.arange(half, dtype=F32) / half)
    freq = jnp.tile(inv_freq, LANES // half).reshape(1, LANES)
    pos = jnp.broadcast_to(positions.reshape(m, 1), (m, LANES))
    cos, sin = _rope_tables(pos, freq)
    decay, kw, qw, cd, avg = _retention_tables()

    o_fox = 2 * LRU_WIDTH
    o_ff = o_fox + 3 * FOX_WIDTH
    o_ret = o_ff + FOX_HEADS

    for l in range(depth):
        wl = w_in[l, :, :o_fox].astype(BF16)
        wf = w_in[l, :, o_fox:o_ff].astype(BF16)
        wff = jnp.pad(w_in[l, :, o_ff:o_ret], ((0, 0), (0, LANES - FOX_HEADS))).astype(BF16)
        wr = w_in[l, :, o_ret:].astype(BF16)
        lru, fox, ret, ff = _inproj(xf, row(pre_mix_g[l]), wl, wf, wr, wff)

        bf = jnp.pad(fox_b_f[l], (0, LANES - FOX_HEADS)).reshape(1, LANES)
        ccol = _fox_cumsum(ff, bf, bsz)
        fox_o = _fox_attention(*_fox_prep(fox, ccol, bsz))

        wg = jnp.concatenate([_block_diag(w_rg[l]), _block_diag(w_ig[l])], axis=1).astype(BF16)
        bg = jnp.concatenate([b_rg[l], b_ig[l]]).reshape(1, -1)
        lru_o = _rg_lru(lru, conv_w[l], row(conv_b[l]), wg, bg, row(lru_lambda[l]), bsz)

        ret_o = _retention(ret, cos, sin, decay, kw, qw, cd, avg, bsz)

        xf = _outproj(xf, lru_o, fox_o, ret_o, w_out[l].astype(BF16), row(post_mix_g[l]))

        wkv = jnp.concatenate([w_ck[l], w_cv[l]], axis=1).astype(BF16)
        kv = _norm_matmul(mem.reshape(bsz * mem_len, d), row(mem_norm_g), wkv, mem_len)
        kt = kv[:, :d].reshape(bsz, mem_len, d).transpose(0, 2, 1)
        vv = kv[:, d:].reshape(bsz, mem_len, d)
        xf = _cross_attention(xf, row(pre_cross_g[l]), w_cq[l].astype(BF16), kt, vv, w_co[l].astype(BF16), row(post_cross_g[l]), bsz)

        xf = _ffn(xf, row(pre_ffn_g[l]), w_gate[l].astype(BF16), w_up[l].astype(BF16), w_down[l].astype(BF16), row(post_ffn_g[l]))
    return xf.reshape(bsz, seq, d)
```

```python
import jax
import jax.numpy as jnp
from jax import lax
from jax.experimental import pallas as pl
from jax.experimental.pallas import tpu as pltpu

F32 = jnp.float32
BF16 = jnp.bfloat16

HEAD_DIM = 64
LRU_WIDTH = 256
LRU_BLOCKS = 4
CONV_WIDTH = 4
LRU_C = 8.0
FOX_HEADS = 8
FOX_WIDTH = FOX_HEADS * HEAD_DIM
RET_HEADS = 4
RET_WIDTH = RET_HEADS * HEAD_DIM
CROSS_HEADS = 4
RET_CHUNK = 128
ROPE_THETA = 10000.0
EPS = 1e-6

LANES = 128
SUBLANES = 8
VMEM_LIMIT = 48 * 1024 * 1024

ROW_TILE = 512
FOX_TQ = 1024
FOX_TK = 512
SCAN_TILE = 256
FFN_CHUNK = 1024
NEG_BIG = -1e30
LOG2E = 1.4426950408889634
SKIP_LOG2 = 160.0
SKIP_REL = 2.0**-8
NORM_SAFETY = 1.0 + 2.0**-7


def _params(sem):
    return pltpu.CompilerParams(dimension_semantics=sem, vmem_limit_bytes=VMEM_LIMIT)


def _rms(x, g):
    ms = jnp.mean(x * x, axis=-1, keepdims=True)
    return x * lax.rsqrt(ms + EPS) * g


def _dot(a, b):
    return jnp.dot(a, b, preferred_element_type=F32)


def _dot_nt(a, b):
    return lax.dot_general(a, b, (((1,), (1,)), ((), ())), preferred_element_type=F32)


def _dot_tn(a, b):
    return lax.dot_general(a, b, (((0,), (0,)), ((), ())), preferred_element_type=F32)


def _full(shape):
    nd = len(shape)
    return pl.BlockSpec(shape, lambda *_: (0,) * nd)


def _scan_rows(a, b):
    n = b.shape[0]
    row = lax.broadcasted_iota(jnp.int32, b.shape, 0)
    k = 1
    while k < n:
        keep = row >= k
        b_prev = jnp.where(keep, pltpu.roll(b, k, 0), 0.0)
        if a is None:
            b = b + b_prev
        else:
            a_prev = jnp.where(keep, pltpu.roll(a, k, 0), 1.0)
            b = a * b_prev + b
            a = a * a_prev
        k *= 2
    return a, b


def _inproj_kernel(x_ref, g_ref, wl_ref, wf_ref, wr_ref, wff_ref, lru_ref, fox_ref, ret_ref, ff_ref):
    h = _rms(x_ref[...], g_ref[...]).astype(BF16)
    lru_ref[...] = _dot(h, wl_ref[...])
    fox_ref[:, :FOX_WIDTH] = (_dot(h, wf_ref[:, :FOX_WIDTH]) * (HEAD_DIM**-0.5 * LOG2E)).astype(BF16)
    fox_ref[:, FOX_WIDTH:] = _dot(h, wf_ref[:, FOX_WIDTH:]).astype(BF16)
    ret_ref[...] = _dot(h, wr_ref[...])
    ff_ref[...] = _dot(h, wff_ref[...])


def _inproj(x, g, wl, wf, wr, wff):
    m, d = x.shape
    tm = ROW_TILE
    row = lambda n: pl.BlockSpec((tm, n), lambda i: (i, 0))
    return pl.pallas_call(
        _inproj_kernel,
        grid=(m // tm,),
        in_specs=[row(d), _full(g.shape), _full(wl.shape), _full(wf.shape), _full(wr.shape), _full(wff.shape)],
        out_specs=[row(wl.shape[1]), row(wf.shape[1]), row(wr.shape[1]), row(wff.shape[1])],
        out_shape=[
            jax.ShapeDtypeStruct((m, wl.shape[1]), F32),
            jax.ShapeDtypeStruct((m, wf.shape[1]), BF16),
            jax.ShapeDtypeStruct((m, wr.shape[1]), F32),
            jax.ShapeDtypeStruct((m, wff.shape[1]), F32),
        ],
        compiler_params=_params(("parallel",)),
        name="mixer_inproj",
    )(x, g, wl, wf, wr, wff)


def _cumsum_kernel(ff_ref, bf_ref, c_ref, carry_ref):
    @pl.when(pl.program_id(1) == 0)
    def _():
        carry_ref[...] = jnp.zeros_like(carry_ref)

    log_f = jax.nn.log_sigmoid(ff_ref[...] + bf_ref[...])
    _, c = _scan_rows(None, log_f)
    c = c + carry_ref[...]
    c_ref[...] = c
    carry_ref[...] = c[-1:, :]


def _fox_cumsum(ff, bf, bsz):
    m, n = ff.shape
    ts = SCAN_TILE
    nt = m // bsz // ts
    return pl.pallas_call(
        _cumsum_kernel,
        grid=(bsz, nt),
        in_specs=[pl.BlockSpec((ts, n), lambda b, i: (b * nt + i, 0)), _full(bf.shape)],
        out_specs=pl.BlockSpec((ts, n), lambda b, i: (b * nt + i, 0)),
        out_shape=jax.ShapeDtypeStruct((m, n), F32),
        scratch_shapes=[pltpu.VMEM((1, n), F32)],
        compiler_params=_params(("parallel", "arbitrary")),
        name="fox_gate_cumsum",
    )(ff, bf)


def _split3(z):
    hi = z.astype(BF16).astype(F32)
    mid = (z - hi).astype(BF16).astype(F32)
    lo = (z - hi - mid).astype(BF16).astype(F32)
    return hi, mid, lo


def _fox_prep_kernel(fox_ref, ccol_ref, q_ref, k_ref, vt_ref, stats_ref):
    ts = fox_ref.shape[0]
    lane = lax.broadcasted_iota(jnp.int32, (ts, LANES), 1)
    stat_lane = lax.broadcasted_iota(jnp.int32, (1, LANES), 1)
    same_head = (lax.broadcasted_iota(jnp.int32, (LANES, LANES), 0) < HEAD_DIM) == (
        lax.broadcasted_iota(jnp.int32, (LANES, LANES), 1) < HEAD_DIM)
    head_ones = jnp.where(same_head, 1.0, 0.0).astype(BF16)
    ccol = ccol_ref[...] * LOG2E
    stats = []
    for pair in range(FOX_HEADS // 2):
        group = lambda part: fox_ref[:, part * FOX_WIDTH + pair * LANES : part * FOX_WIDTH + (pair + 1) * LANES]
        gq, gk, gv = group(0), group(1), group(2)
        norms = []
        for g in (gq, gk):
            g32 = g.astype(F32)
            sq = _dot((g32 * g32).astype(BF16), head_ones)
            norms.append(jnp.sqrt(jnp.max(sq, axis=0, keepdims=True)) * NORM_SAFETY)
        for e in range(2):
            h = 2 * pair + e
            data = (lane < HEAD_DIM) if e == 0 else (lane >= HEAD_DIM)
            el = lane - (HEAD_DIM if e == 0 else 0)
            hi, mid, lo = _split3(ccol[:, h : h + 1])
            q_extra = jnp.where(el == 0, hi, jnp.where(el == 1, mid, jnp.where(el == 2, lo, jnp.where((el >= 3) & (el < 6), 1.0, 0.0))))
            k_extra = jnp.where((el >= 0) & (el < 3), 1.0, jnp.where(el == 3, -hi, jnp.where(el == 4, -mid, jnp.where(el == 5, -lo, 0.0))))
            v_extra = jnp.where(el == 0, 1.0, 0.0)
            q_ref[0, h] = jnp.where(data, gq, q_extra.astype(BF16))
            k_ref[0, h] = jnp.where(data, gk, k_extra.astype(BF16))
            vt_ref[0, h] = jnp.where(data, gv, v_extra.astype(BF16)).T
            first = e * HEAD_DIM
            q_norm = norms[0][:, first : first + 1]
            k_norm = norms[1][:, first : first + 1]
            c_first = ccol[0:1, h : h + 1]
            c_last = ccol[ts - 1 : ts, h : h + 1]
            stats.append(jnp.where(stat_lane == 0, q_norm, jnp.where(stat_lane == 1, k_norm, jnp.where(stat_lane == 2, c_first, c_last))))
    stats_ref[0, 0] = jnp.concatenate(stats, axis=0)


def _fox_prep(fox, ccol, bsz):
    m = fox.shape[0]
    s = m // bsz
    ts = FOX_TK
    nt = s // ts
    return pl.pallas_call(
        _fox_prep_kernel,
        grid=(bsz, nt),
        in_specs=[
            pl.BlockSpec((ts, fox.shape[1]), lambda b, i: (b * nt + i, 0)),
            pl.BlockSpec((ts, LANES), lambda b, i: (b * nt + i, 0)),
        ],
        out_specs=[
            pl.BlockSpec((1, FOX_HEADS, ts, LANES), lambda b, i: (b, 0, i, 0)),
            pl.BlockSpec((1, FOX_HEADS, ts, LANES), lambda b, i: (b, 0, i, 0)),
            pl.BlockSpec((1, FOX_HEADS, LANES, ts), lambda b, i: (b, 0, 0, i)),
            pl.BlockSpec((1, 1, FOX_HEADS, LANES), lambda b, i: (b, i, 0, 0)),
        ],
        out_shape=[
            jax.ShapeDtypeStruct((bsz, FOX_HEADS, s, LANES), BF16),
            jax.ShapeDtypeStruct((bsz, FOX_HEADS, s, LANES), BF16),
            jax.ShapeDtypeStruct((bsz, FOX_HEADS, LANES, s), BF16),
            jax.ShapeDtypeStruct((bsz, nt, FOX_HEADS, LANES), F32),
        ],
        compiler_params=_params(("parallel", "parallel")),
        name="fox_prep",
    )(fox, ccol)


def _fox_kernel(qn_ref, kn_ref, cf_ref, cl_ref, q_ref, k_ref, vt_ref, o_ref, m_scr, acc_scr, s_scr, mx_scr):
    tq = q_ref.shape[2]
    tk = FOX_TK
    sub = tq // tk
    nk = k_ref.shape[2] // tk
    nq = nk // sub
    qi = pl.program_id(2)

    def first_needed(h):
        head = (pl.program_id(0) * (FOX_HEADS // 2) + pl.program_id(1)) * 2 + h
        q_norm = qn_ref[head * nq + qi]
        c_first = cf_ref[head * nq + qi]
        floor = -q_norm * kn_ref[head * nk + qi * sub + sub - 1]

        def needed(kb):
            j = head * nk + jnp.maximum(kb, 0)
            dot_bound = q_norm * kn_ref[j]
            slack = SKIP_REL * (dot_bound - floor + jnp.abs(c_first) + jnp.abs(cl_ref[j]))
            return dot_bound + c_first - cl_ref[j] + slack >= floor - SKIP_LOG2

        kb = lax.while_loop(lambda kb: jnp.logical_and(kb >= 0, needed(kb)), lambda kb: kb - 1, qi * sub - 1)
        return kb + 1

    def logits(h, kb, slot, diag):
        start = pl.multiple_of(kb * tk, tk)
        s = _dot_nt(k_ref[0, h, pl.ds(start, tk), :], q_ref[0, h])
        if diag is not None:
            key = lax.broadcasted_iota(jnp.int32, (tk, tq), 0) + diag * tk
            qry = lax.broadcasted_iota(jnp.int32, (tk, tq), 1)
            s = jnp.where(key <= qry, s, NEG_BIG)
        s_scr[slot] = s
        mx_scr[slot] = jnp.max(s, axis=0, keepdims=True)

    def accumulate(h, kb, slot):
        start = pl.multiple_of(kb * tk, tk)
        m_old = m_scr[h]
        m_new = jnp.maximum(m_old, mx_scr[slot])
        m_scr[h] = m_new
        p = jnp.exp2(s_scr[slot] - m_new).astype(BF16)
        acc_scr[h] = jnp.exp2(m_old - m_new) * acc_scr[h] + _dot(vt_ref[0, h, :, pl.ds(start, tk)], p)

    m_scr[...] = jnp.full_like(m_scr, NEG_BIG)
    acc_scr[...] = jnp.zeros_like(acc_scr)
    own = qi * sub
    lo = jnp.minimum(first_needed(0), first_needed(1))
    logits(0, own, 0, 0)
    logits(1, own, 1, 0)
    accumulate(0, own, 0)
    for u in range(1, sub):
        logits(0, own + u, 0, u)
        accumulate(1, own + u - 1, 1)
        logits(1, own + u, 1, u)
        accumulate(0, own + u, 0)
    last_own = own + sub - 1

    @pl.when(lo >= own)
    def _():
        accumulate(1, last_own, 1)

    @pl.when(lo < own)
    def _():
        logits(0, lo, 0, None)
        accumulate(1, last_own, 1)

        def body(kb, carry):
            logits(1, kb, 1, None)
            accumulate(0, kb, 0)
            logits(0, kb + 1, 0, None)
            accumulate(1, kb, 1)
            return carry

        lax.fori_loop(lo, own - 1, body, 0)
        logits(1, own - 1, 1, None)
        accumulate(0, own - 1, 0)
        accumulate(1, own - 1, 1)

    even, odd = acc_scr[0], acc_scr[1]
    row = lax.broadcasted_iota(jnp.int32, even.shape, 0)
    o = jnp.where(row < HEAD_DIM, even / even[HEAD_DIM : HEAD_DIM + 1, :], odd / odd[0:1, :])
    o_ref[...] = o.T.astype(o_ref.dtype)


def _fox_attention(q, k, vt, stats):
    bsz, nh, s, _ = q.shape
    tq = FOX_TQ
    sub = tq // FOX_TK
    nq = s // tq
    npair = nh // 2
    resident = pl.Buffered(1)
    st = stats[..., :4].transpose(0, 2, 1, 3)
    per_q = lambda v: v.reshape(bsz, nh, nq, sub)
    scalars = (
        per_q(st[..., 0]).max(axis=-1),
        lax.cummax(st[..., 1], axis=2),
        per_q(st[..., 2])[..., 0],
        st[..., 3],
    )
    scalars = tuple(v.reshape(-1) for v in scalars)
    return pl.pallas_call(
        _fox_kernel,
        grid_spec=pltpu.PrefetchScalarGridSpec(
            num_scalar_prefetch=len(scalars),
            grid=(bsz, npair, nq),
            in_specs=[
                pl.BlockSpec((1, 2, tq, LANES), lambda b, p, i, *_: (b, p, i, 0)),
                pl.BlockSpec((1, 2, s, LANES), lambda b, p, i, *_: (b, p, 0, 0), pipeline_mode=resident),
                pl.BlockSpec((1, 2, LANES, s), lambda b, p, i, *_: (b, p, 0, 0), pipeline_mode=resident),
            ],
            out_specs=pl.BlockSpec((tq, LANES), lambda b, p, i, *_: (b * nq + i, p)),
            scratch_shapes=[
                pltpu.VMEM((2, 1, tq), F32),
                pltpu.VMEM((2, LANES, tq), F32),
                pltpu.VMEM((2, FOX_TK, tq), F32),
                pltpu.VMEM((2, 1, tq), F32),
            ],
        ),
        out_shape=jax.ShapeDtypeStruct((bsz * s, nh * HEAD_DIM), BF16),
        compiler_params=_params(("parallel", "parallel", "arbitrary")),
        name="fox_attention",
    )(*scalars, q, k, vt)


def _lru_kernel(xy_ref, cw_ref, cb_ref, wg_ref, bg_ref, lam_ref, o_ref, tail_scr, h_scr):
    w = LRU_WIDTH

    @pl.when(pl.program_id(1) == 0)
    def _():
        tail_scr[...] = jnp.zeros_like(tail_scr)
        h_scr[...] = jnp.zeros_like(h_scr)

    x = xy_ref[:, :w]
    y = xy_ref[:, w:]
    tail = tail_scr[...]
    row8 = lax.broadcasted_iota(jnp.int32, (SUBLANES, w), 0)
    conv = x * cw_ref[CONV_WIDTH - 1 : CONV_WIDTH, :] + cb_ref[...]
    for j in range(1, CONV_WIDTH):
        xs = pltpu.roll(x, j, 0)
        head = jnp.where(row8 < j, pltpu.roll(tail, j, 0), xs[:SUBLANES])
        xs = jnp.concatenate([head, xs[SUBLANES:]], axis=0)
        conv = conv + xs * cw_ref[CONV_WIDTH - 1 - j : CONV_WIDTH - j, :]
    tail_scr[...] = x[-SUBLANES:]

    gates = jax.nn.sigmoid(_dot(conv.astype(BF16), wg_ref[...]) + bg_ref[...])
    r = gates[:, :w]
    i = gates[:, w:]
    log_a = -LRU_C * r * jax.nn.softplus(-lam_ref[...])
    a = jnp.exp(log_a)
    th = jnp.tanh(log_a)
    u = jnp.sqrt(-2.0 * th / (1.0 - th)) * (i * conv)
    a_run, h = _scan_rows(a, u)
    h = h + a_run * h_scr[...]
    h_scr[...] = h[-1:, :]
    o_ref[...] = (h * jax.nn.gelu(y)).astype(o_ref.dtype)


def _rg_lru(xy, cw, cb, wg, bg, lam, bsz):
    m = xy.shape[0]
    ts = SCAN_TILE
    nt = m // bsz // ts
    w = LRU_WIDTH
    return pl.pallas_call(
        _lru_kernel,
        grid=(bsz, nt),
        in_specs=[
            pl.BlockSpec((ts, 2 * w), lambda b, i: (b * nt + i, 0)),
            _full(cw.shape), _full(cb.shape), _full(wg.shape), _full(bg.shape), _full(lam.shape),
        ],
        out_specs=pl.BlockSpec((ts, w), lambda b, i: (b * nt + i, 0)),
        out_shape=jax.ShapeDtypeStruct((m, w), BF16),
        scratch_shapes=[pltpu.VMEM((SUBLANES, w), F32), pltpu.VMEM((1, w), F32)],
        compiler_params=_params(("parallel", "arbitrary")),
        name="rg_lru",
    )(xy, cw, cb, wg, bg, lam)


def _rope_kernel(pos_ref, freq_ref, cos_ref, sin_ref):
    ang = pos_ref[...].astype(F32) * freq_ref[...]
    lane = lax.broadcasted_iota(jnp.int32, ang.shape, 1)
    first_half = (lane % HEAD_DIM) < HEAD_DIM // 2
    cos_ref[...] = jnp.cos(ang)
    s = jnp.sin(ang)
    sin_ref[...] = jnp.where(first_half, -s, s)


def _rope_tables(pos, freq):
    m = pos.shape[0]
    tm = ROW_TILE
    spec = pl.BlockSpec((tm, LANES), lambda i: (i, 0))
    return pl.pallas_call(
        _rope_kernel,
        grid=(m // tm,),
        in_specs=[spec, _full(freq.shape)],
        out_specs=[spec, spec],
        out_shape=[jax.ShapeDtypeStruct((m, LANES), F32)] * 2,
        compiler_params=_params(("parallel",)),
        name="rope_tables",
    )(pos, freq)


def _group_mean(z, avg):
    hi, mid, lo = _split3(z)
    return _dot(hi.astype(BF16), avg) + _dot(mid.astype(BF16), avg) + _dot(lo.astype(BF16), avg)


def _ret_kernel(x_ref, cos_ref, sin_ref, decay_ref, kw_ref, qw_ref, cd_ref, avg_ref, o_ref, state_scr):
    c = RET_CHUNK
    w = RET_WIDTH
    npair = RET_HEADS // 2

    @pl.when(pl.program_id(1) == 0)
    def _():
        state_scr[...] = jnp.zeros_like(state_scr)

    cos = cos_ref[...]
    sin = sin_ref[...]
    lane = lax.broadcasted_iota(jnp.int32, cos.shape, 1)
    first_half = (lane % HEAD_DIM) < HEAD_DIM // 2
    tile_head = (lane < HEAD_DIM, lane >= HEAD_DIM)
    low_head = lax.broadcasted_iota(jnp.int32, (c, LANES), 1) < HEAD_DIM
    chunk_rows = [slice(n * c, (n + 1) * c) for n in range(x_ref.shape[0] // c)]

    def rotary(z):
        swapped = jnp.where(first_half, pltpu.roll(z, LANES - HEAD_DIM // 2, 1), pltpu.roll(z, HEAD_DIM // 2, 1))
        return z * cos + swapped * sin

    ys = []
    for p in range(npair):
        cols = slice(p * LANES, (p + 1) * LANES)
        q = rotary(x_ref[:, cols])
        k = rotary(x_ref[:, w + p * LANES : w + (p + 1) * LANES]) * HEAD_DIM**-0.5
        v = x_ref[:, 2 * w + p * LANES : 2 * w + (p + 1) * LANES].astype(BF16)
        kb = k.astype(BF16)
        q_head = [jnp.where(tile_head[h], q, 0.0).astype(BF16) for h in range(2)]
        q_cross = (q * qw_ref[:, cols]).astype(BF16)
        k_state = (k * kw_ref[:, cols]).astype(BF16)
        cd = cd_ref[p]
        incs = [jnp.where(cd > 0.0, _dot_tn(k_state[r], v[r]), 0.0) for r in chunk_rows]
        states = [state_scr[p]]
        for inc in incs:
            states.append(cd * states[-1] + inc)
        state_scr[p] = states[-1]
        chunks = []
        for n, r in enumerate(chunk_rows):
            inner = [
                _dot((_dot_nt(q_head[h][r], kb[r]) * decay_ref[2 * p + h]).astype(BF16), v[r]) for h in range(2)
            ]
            chunks.append(jnp.where(low_head, inner[0], inner[1]) + _dot(q_cross[r], states[n].astype(BF16)))
        ys.append(jnp.concatenate(chunks, axis=0))
    y = jnp.concatenate(ys, axis=1)
    avg = avg_ref[...]
    mu = _group_mean(y, avg)
    d = y - mu
    var = _group_mean(d * d, avg)
    yn = d * lax.rsqrt(var + EPS)
    g = x_ref[:, 3 * w :]
    o_ref[...] = (g * jax.nn.sigmoid(g) * yn).astype(o_ref.dtype)


def _retention(ret, cos, sin, decay, kw, qw, cd, avg, bsz):
    m = ret.shape[0]
    ts = ROW_TILE
    nt = m // bsz // ts
    w = RET_WIDTH
    return pl.pallas_call(
        _ret_kernel,
        grid=(bsz, nt),
        in_specs=[
            pl.BlockSpec((ts, 4 * w), lambda b, i: (b * nt + i, 0)),
            pl.BlockSpec((ts, LANES), lambda b, i: (b * nt + i, 0)),
            pl.BlockSpec((ts, LANES), lambda b, i: (b * nt + i, 0)),
            _full(decay.shape), _full(kw.shape), _full(qw.shape), _full(cd.shape), _full(avg.shape),
        ],
        out_specs=pl.BlockSpec((ts, w), lambda b, i: (b * nt + i, 0)),
        out_shape=jax.ShapeDtypeStruct((m, w), BF16),
        scratch_shapes=[pltpu.VMEM((RET_HEADS // 2, LANES, LANES), F32)],
        compiler_params=_params(("parallel", "arbitrary")),
        name="retention",
    )(ret, cos, sin, decay, kw, qw, cd, avg)


def _outproj_kernel(x_ref, lru_ref, fox_ref, ret_ref, w_ref, g_ref, o_ref):
    a, b = LRU_WIDTH, LRU_WIDTH + FOX_WIDTH
    y = _dot(lru_ref[...], w_ref[:a, :]) + _dot(fox_ref[...], w_ref[a:b, :]) + _dot(ret_ref[...], w_ref[b:, :])
    o_ref[...] = x_ref[...] + _rms(y, g_ref[...])


def _outproj(x, lru, fox, ret, w, g):
    m, d = x.shape
    tm = ROW_TILE
    row = lambda n: pl.BlockSpec((tm, n), lambda i: (i, 0))
    return pl.pallas_call(
        _outproj_kernel,
        grid=(m // tm,),
        in_specs=[row(d), row(lru.shape[1]), row(fox.shape[1]), row(ret.shape[1]), _full(w.shape), _full(g.shape)],
        out_specs=row(d),
        out_shape=jax.ShapeDtypeStruct((m, d), F32),
        compiler_params=_params(("parallel",)),
        name="mixer_outproj",
    )(x, lru, fox, ret, w, g)


def _norm_matmul_kernel(x_ref, g_ref, w_ref, o_ref):
    o_ref[...] = _dot(_rms(x_ref[...], g_ref[...]).astype(BF16), w_ref[...]).astype(o_ref.dtype)


def _norm_matmul(x, g, w, tm):
    m, d = x.shape
    n = w.shape[1]
    return pl.pallas_call(
        _norm_matmul_kernel,
        grid=(m // tm,),
        in_specs=[pl.BlockSpec((tm, d), lambda i: (i, 0)), _full(g.shape), _full(w.shape)],
        out_specs=pl.BlockSpec((tm, n), lambda i: (i, 0)),
        out_shape=jax.ShapeDtypeStruct((m, n), BF16),
        compiler_params=_params(("parallel",)),
        name="memory_kv_proj",
    )(x, g, w)


def _cross_kernel(x_ref, g1_ref, wq_ref, kt_ref, v_ref, wo_ref, g2_ref, o_ref):
    x = x_ref[...]
    d = x.shape[1]
    hd = d // CROSS_HEADS
    q = _dot(_rms(x, g1_ref[...]).astype(BF16), wq_ref[...]).astype(BF16)
    outs = []
    for h in range(CROSS_HEADS):
        cols = slice(h * hd, (h + 1) * hd)
        s = _dot(q[:, cols], kt_ref[0, cols, :]) * hd**-0.5
        e = jnp.exp(s - jnp.max(s, axis=-1, keepdims=True))
        p = e / jnp.sum(e, axis=-1, keepdims=True)
        outs.append(_dot(p.astype(BF16), v_ref[0, :, cols]).astype(BF16))
    o = jnp.concatenate(outs, axis=1)
    o_ref[...] = x + _rms(_dot(o, wo_ref[...]), g2_ref[...])


def _cross_attention(x, g1, wq, kt, v, wo, g2, bsz):
    m, d = x.shape
    tm = ROW_TILE
    nt = m // bsz // tm
    mem_len = v.shape[1]
    return pl.pallas_call(
        _cross_kernel,
        grid=(m // tm,),
        in_specs=[
            pl.BlockSpec((tm, d), lambda i: (i, 0)),
            _full(g1.shape), _full(wq.shape),
            pl.BlockSpec((1, d, mem_len), lambda i: (i // nt, 0, 0)),
            pl.BlockSpec((1, mem_len, d), lambda i: (i // nt, 0, 0)),
            _full(wo.shape), _full(g2.shape),
        ],
        out_specs=pl.BlockSpec((tm, d), lambda i: (i, 0)),
        out_shape=jax.ShapeDtypeStruct((m, d), F32),
        compiler_params=_params(("parallel",)),
        name="memory_cross_attention",
    )(x, g1, wq, kt, v, wo, g2)


def _ffn_kernel(x_ref, g1_ref, wg_ref, wu_ref, wd_ref, g2_ref, o_ref):
    x = x_ref[...]
    h = _rms(x, g1_ref[...]).astype(BF16)
    dff = wg_ref.shape[1]
    y = None
    for lo in range(0, dff, FFN_CHUNK):
        hi = min(lo + FFN_CHUNK, dff)
        gate = _dot(h, wg_ref[:, lo:hi])
        up = _dot(h, wu_ref[:, lo:hi])
        act = (gate * jax.nn.sigmoid(gate) * up).astype(BF16)
        part = _dot(act, wd_ref[lo:hi, :])
        y = part if y is None else y + part
    o_ref[...] = x + _rms(y, g2_ref[...])


def _ffn(x, g1, wg, wu, wd, g2):
    m, d = x.shape
    tm = ROW_TILE
    resident = lambda shape: pl.BlockSpec(shape, lambda i: (0, 0), pipeline_mode=pl.Buffered(1))
    return pl.pallas_call(
        _ffn_kernel,
        grid=(m // tm,),
        in_specs=[
            pl.BlockSpec((tm, d), lambda i: (i, 0)),
            _full(g1.shape), resident(wg.shape), resident(wu.shape), resident(wd.shape), _full(g2.shape),
        ],
        out_specs=pl.BlockSpec((tm, d), lambda i: (i, 0)),
        out_shape=jax.ShapeDtypeStruct((m, d), F32),
        compiler_params=_params(("parallel",)),
        name="swiglu_ffn",
    )(x, g1, wg, wu, wd, g2)


def _block_diag(w):
    n, a, b = w.shape
    out = jnp.zeros((n * a, n * b), w.dtype)
    for i in range(n):
        out = out.at[i * a : (i + 1) * a, i * b : (i + 1) * b].set(w[i])
    return out


def _retention_tables():
    c = RET_CHUNK
    log_gamma = jnp.log1p(-jnp.exp2(-5.0 - jnp.arange(RET_HEADS, dtype=F32)))
    idx = jnp.arange(c, dtype=F32)
    diff = idx[:, None] - idx[None, :]
    decay = jnp.where(diff >= 0, jnp.exp(log_gamma[:, None, None] * jnp.maximum(diff, 0.0)), 0.0)
    k_w = jnp.exp(log_gamma[None, :] * (c - 1.0 - idx)[:, None])
    q_w = jnp.exp(log_gamma[None, :] * (idx + 1.0)[:, None])
    chunk_decay = jnp.exp(log_gamma * c)
    reps = ROW_TILE // c
    kw = jnp.tile(jnp.repeat(k_w, HEAD_DIM, axis=1), (reps, 1))
    qw = jnp.tile(jnp.repeat(q_w, HEAD_DIM, axis=1), (reps, 1))
    eye = jnp.eye(2, dtype=F32)
    cd = jnp.stack([
        jnp.kron(eye * chunk_decay[2 * p : 2 * p + 2][None, :], jnp.ones((HEAD_DIM, HEAD_DIM), F32))
        for p in range(RET_HEADS // 2)
    ])
    avg = jnp.kron(jnp.eye(RET_HEADS, dtype=F32), jnp.full((HEAD_DIM, HEAD_DIM), 1.0 / HEAD_DIM, F32)).astype(BF16)
    return decay, kw, qw, cd, avg


def kernel(x, mem, positions, pre_mix_g, post_mix_g, w_in, conv_w, conv_b, w_rg, b_rg, w_ig, b_ig, lru_lambda, fox_b_f, w_out, pre_cross_g, post_cross_g, mem_norm_g, w_cq, w_ck, w_cv, w_co, pre_ffn_g, post_ffn_g, w_gate, w_up, w_down):
    bsz, seq, d = x.shape
    depth = w_in.shape[0]
    m = bsz * seq
    mem_len = mem.shape[1]
    xf = x.reshape(m, d)
    row = lambda v: v.reshape(1, -1)

    half = HEAD_DIM // 2
    inv_freq = ROPE_THETA ** (-jnp.arange(half, dtype=F32) / half)
    freq = jnp.tile(inv_freq, LANES // half).reshape(1, LANES)
    pos = jnp.broadcast_to(positions.reshape(m, 1), (m, LANES))
    cos, sin = _rope_tables(pos, freq)
    decay, kw, qw, cd, avg = _retention_tables()

    o_fox = 2 * LRU_WIDTH
    o_ff = o_fox + 3 * FOX_WIDTH
    o_ret = o_ff + FOX_HEADS

    for l in range(depth):
        wl = w_in[l, :, :o_fox].astype(BF16)
        wf = w_in[l, :, o_fox:o_ff].astype(BF16)
        wff = jnp.pad(w_in[l, :, o_ff:o_ret], ((0, 0), (0, LANES - FOX_HEADS))).astype(BF16)
        wr = w_in[l, :, o_ret:].astype(BF16)
        lru, fox, ret, ff = _inproj(xf, row(pre_mix_g[l]), wl, wf, wr, wff)

        bf = jnp.pad(fox_b_f[l], (0, LANES - FOX_HEADS)).reshape(1, LANES)
        ccol = _fox_cumsum(ff, bf, bsz)
        fox_o = _fox_attention(*_fox_prep(fox, ccol, bsz))

        wg = jnp.concatenate([_block_diag(w_rg[l]), _block_diag(w_ig[l])], axis=1).astype(BF16)
        bg = jnp.concatenate([b_rg[l], b_ig[l]]).reshape(1, -1)
        lru_o = _rg_lru(lru, conv_w[l], row(conv_b[l]), wg, bg, row(lru_lambda[l]), bsz)

        ret_o = _retention(ret, cos, sin, decay, kw, qw, cd, avg, bsz)

        xf = _outproj(xf, lru_o, fox_o, ret_o, w_out[l].astype(BF16), row(post_mix_g[l]))

        wkv = jnp.concatenate([w_ck[l], w_cv[l]], axis=1).astype(BF16)
        kv = _norm_matmul(mem.reshape(bsz * mem_len, d), row(mem_norm_g), wkv, mem_len)
        kt = kv[:, :d].reshape(bsz, mem_len, d).transpose(0, 2, 1)
        vv = kv[:, d:].reshape(bsz, mem_len, d)
        xf = _cross_attention(xf, row(pre_cross_g[l]), w_cq[l].astype(BF16), kt, vv, w_co[l].astype(BF16), row(post_cross_g[l]), bsz)

        xf = _ffn(xf, row(pre_ffn_g[l]), w_gate[l].astype(BF16), w_up[l].astype(BF16), w_down[l].astype(BF16), row(post_ffn_g[l]))
    return xf.reshape(bsz, seq, d)
```

```python
import jax
import jax.numpy as jnp
from jax import lax
from jax.experimental import pallas as pl
from jax.experimental.pallas import tpu as pltpu

F32 = jnp.float32
BF16 = jnp.bfloat16

HEAD_DIM = 64
LRU_WIDTH = 256
LRU_BLOCKS = 4
CONV_WIDTH = 4
LRU_C = 8.0
FOX_HEADS = 8
FOX_WIDTH = FOX_HEADS * HEAD_DIM
RET_HEADS = 4
RET_WIDTH = RET_HEADS * HEAD_DIM
CROSS_HEADS = 4
RET_CHUNK = 128
ROPE_THETA = 10000.0
EPS = 1e-6

LANES = 128
SUBLANES = 8
VMEM_LIMIT = 48 * 1024 * 1024

ROW_TILE = 512
FOX_TQ = 1024
FOX_TK = 512
SCAN_TILE = 256
FFN_CHUNK = 1024
NEG_BIG = -1e30
LOG2E = 1.4426950408889634
SKIP_LOG2 = 160.0
SKIP_REL = 2.0**-8
NORM_SAFETY = 1.0 + 2.0**-7


def _params(sem):
    return pltpu.CompilerParams(dimension_semantics=sem, vmem_limit_bytes=VMEM_LIMIT)


def _rms(x, g):
    ms = jnp.mean(x * x, axis=-1, keepdims=True)
    return x * lax.rsqrt(ms + EPS) * g


def _dot(a, b):
    return jnp.dot(a, b, preferred_element_type=F32)


def _dot_nt(a, b):
    return lax.dot_general(a, b, (((1,), (1,)), ((), ())), preferred_element_type=F32)


def _dot_tn(a, b):
    return lax.dot_general(a, b, (((0,), (0,)), ((), ())), preferred_element_type=F32)


def _full(shape):
    nd = len(shape)
    return pl.BlockSpec(shape, lambda *_: (0,) * nd)


def _scan_rows(a, b, carry):
    n = b.shape[0]
    sub = lax.broadcasted_iota(jnp.int32, b.shape, 0) % SUBLANES
    k = 1
    while k < SUBLANES:
        keep = sub >= k
        b_prev = jnp.where(keep, pltpu.roll(b, k, 0), 0.0)
        if a is None:
            b = b + b_prev
        else:
            a_prev = jnp.where(keep, pltpu.roll(a, k, 0), 1.0)
            b = a * b_prev + b
            a = a * a_prev
        k *= 2
    groups = []
    for g in range(n // SUBLANES):
        rows = slice(g * SUBLANES, (g + 1) * SUBLANES)
        h = b[rows] + carry if a is None else b[rows] + a[rows] * carry
        carry = h[SUBLANES - 1 :, :]
        groups.append(h)
    return jnp.concatenate(groups, axis=0)


def _inproj_kernel(x_ref, g_ref, wl_ref, wf_ref, wr_ref, wff_ref, lru_ref, fox_ref, ret_ref, ff_ref):
    h = _rms(x_ref[...], g_ref[...]).astype(BF16)
    lru_ref[...] = _dot(h, wl_ref[...])
    fox_ref[:, :FOX_WIDTH] = (_dot(h, wf_ref[:, :FOX_WIDTH]) * (HEAD_DIM**-0.5 * LOG2E)).astype(BF16)
    fox_ref[:, FOX_WIDTH:] = _dot(h, wf_ref[:, FOX_WIDTH:]).astype(BF16)
    ret_ref[...] = _dot(h, wr_ref[...])
    ff_ref[...] = _dot(h, wff_ref[...])


def _inproj(x, g, wl, wf, wr, wff):
    m, d = x.shape
    tm = ROW_TILE
    row = lambda n: pl.BlockSpec((tm, n), lambda i: (i, 0))
    return pl.pallas_call(
        _inproj_kernel,
        grid=(m // tm,),
        in_specs=[row(d), _full(g.shape), _full(wl.shape), _full(wf.shape), _full(wr.shape), _full(wff.shape)],
        out_specs=[row(wl.shape[1]), row(wf.shape[1]), row(wr.shape[1]), row(wff.shape[1])],
        out_shape=[
            jax.ShapeDtypeStruct((m, wl.shape[1]), F32),
            jax.ShapeDtypeStruct((m, wf.shape[1]), BF16),
            jax.ShapeDtypeStruct((m, wr.shape[1]), F32),
            jax.ShapeDtypeStruct((m, wff.shape[1]), F32),
        ],
        compiler_params=_params(("parallel",)),
        name="mixer_inproj",
    )(x, g, wl, wf, wr, wff)


def _split3(z):
    hi = z.astype(BF16).astype(F32)
    mid = (z - hi).astype(BF16).astype(F32)
    lo = (z - hi - mid).astype(BF16).astype(F32)
    return hi, mid, lo


def _fox_prep_kernel(fox_ref, ff_ref, bf_ref, place_ref, q_ref, k_ref, vt_ref, stats_ref, carry_ref):
    ts = fox_ref.shape[0]

    @pl.when(pl.program_id(1) == 0)
    def _():
        carry_ref[...] = jnp.zeros_like(carry_ref)

    c_nat = _scan_rows(None, jax.nn.log_sigmoid(ff_ref[...] + bf_ref[...]), carry_ref[...])
    carry_ref[...] = c_nat[-1:, :]
    ccol = c_nat * LOG2E
    terms = jnp.concatenate(_split3(ccol), axis=1).astype(BF16)
    placed = _dot(terms, place_ref[...])

    lane = lax.broadcasted_iota(jnp.int32, (ts, LANES), 1)
    stat_lane = lax.broadcasted_iota(jnp.int32, (1, LANES), 1)
    same_head = (lax.broadcasted_iota(jnp.int32, (LANES, LANES), 0) < HEAD_DIM) == (
        lax.broadcasted_iota(jnp.int32, (LANES, LANES), 1) < HEAD_DIM)
    head_ones = jnp.where(same_head, 1.0, 0.0).astype(BF16)
    stats = []
    for pair in range(FOX_HEADS // 2):
        group = lambda part: fox_ref[:, part * FOX_WIDTH + pair * LANES : part * FOX_WIDTH + (pair + 1) * LANES]
        gq, gk, gv = group(0), group(1), group(2)
        norms = []
        for g in (gq, gk):
            g32 = g.astype(F32)
            sq = _dot((g32 * g32).astype(BF16), head_ones)
            norms.append(jnp.sqrt(jnp.max(sq, axis=0, keepdims=True)) * NORM_SAFETY)
        for e in range(2):
            h = 2 * pair + e
            data = (lane < HEAD_DIM) if e == 0 else (lane >= HEAD_DIM)
            el = lane - (HEAD_DIM if e == 0 else 0)
            extras = placed[:, h * LANES : (h + 1) * LANES]
            q_extra = jnp.where((el >= 3) & (el < 6), 1.0, extras)
            k_extra = jnp.where((el >= 0) & (el < 3), 1.0, extras)
            v_extra = jnp.where(el == 0, 1.0, 0.0)
            q_ref[0, h] = jnp.where(data, gq, q_extra.astype(BF16))
            k_ref[0, h] = jnp.where(data, gk, k_extra.astype(BF16))
            vt_ref[0, h] = jnp.where(data, gv, v_extra.astype(BF16)).T
            first = e * HEAD_DIM
            q_norm = norms[0][:, first : first + 1]
            k_norm = norms[1][:, first : first + 1]
            c_first = ccol[0:1, h : h + 1]
            c_last = ccol[ts - 1 : ts, h : h + 1]
            stats.append(jnp.where(stat_lane == 0, q_norm, jnp.where(stat_lane == 1, k_norm, jnp.where(stat_lane == 2, c_first, c_last))))
    stats_ref[0, 0] = jnp.concatenate(stats, axis=0)


def _fox_placement():
    place = jnp.zeros((3 * LANES, FOX_HEADS * LANES), F32)
    for h in range(FOX_HEADS):
        first = h * LANES + (HEAD_DIM if h % 2 == 0 else 0)
        for t in range(3):
            place = place.at[t * LANES + h, first + t].set(1.0).at[t * LANES + h, first + 3 + t].set(-1.0)
    return place.astype(BF16)


def _fox_prep(fox, ff, bf, bsz):
    m = fox.shape[0]
    s = m // bsz
    ts = FOX_TK
    nt = s // ts
    place = _fox_placement()
    return pl.pallas_call(
        _fox_prep_kernel,
        grid=(bsz, nt),
        in_specs=[
            pl.BlockSpec((ts, fox.shape[1]), lambda b, i: (b * nt + i, 0)),
            pl.BlockSpec((ts, LANES), lambda b, i: (b * nt + i, 0)),
            _full(bf.shape),
            _full(place.shape),
        ],
        out_specs=[
            pl.BlockSpec((1, FOX_HEADS, ts, LANES), lambda b, i: (b, 0, i, 0)),
            pl.BlockSpec((1, FOX_HEADS, ts, LANES), lambda b, i: (b, 0, i, 0)),
            pl.BlockSpec((1, FOX_HEADS, LANES, ts), lambda b, i: (b, 0, 0, i)),
            pl.BlockSpec((1, 1, FOX_HEADS, LANES), lambda b, i: (b, i, 0, 0)),
        ],
        out_shape=[
            jax.ShapeDtypeStruct((bsz, FOX_HEADS, s, LANES), BF16),
            jax.ShapeDtypeStruct((bsz, FOX_HEADS, s, LANES), BF16),
            jax.ShapeDtypeStruct((bsz, FOX_HEADS, LANES, s), BF16),
            jax.ShapeDtypeStruct((bsz, nt, FOX_HEADS, LANES), F32),
        ],
        scratch_shapes=[pltpu.VMEM((1, LANES), F32)],
        compiler_params=_params(("parallel", "arbitrary")),
        name="fox_prep",
    )(fox, ff, bf, place)


def _fox_kernel(qn_ref, kn_ref, cf_ref, cl_ref, q_ref, k_ref, vt_ref, o_ref, m_scr, acc_scr, s_scr, mx_scr):
    tq = q_ref.shape[2]
    tk = FOX_TK
    sub = tq // tk
    nk = k_ref.shape[2] // tk
    nq = nk // sub
    qi = pl.program_id(2)

    def first_needed(h):
        head = (pl.program_id(0) * (FOX_HEADS // 2) + pl.program_id(1)) * 2 + h
        q_norm = qn_ref[head * nq + qi]
        c_first = cf_ref[head * nq + qi]
        floor = -q_norm * kn_ref[head * nk + qi * sub + sub - 1]

        def needed(kb):
            j = head * nk + jnp.maximum(kb, 0)
            dot_bound = q_norm * kn_ref[j]
            slack = SKIP_REL * (dot_bound - floor + jnp.abs(c_first) + jnp.abs(cl_ref[j]))
            return dot_bound + c_first - cl_ref[j] + slack >= floor - SKIP_LOG2

        kb = lax.while_loop(lambda kb: jnp.logical_and(kb >= 0, needed(kb)), lambda kb: kb - 1, qi * sub - 1)
        return kb + 1

    def logits(h, kb, slot, diag):
        start = pl.multiple_of(kb * tk, tk)
        q0 = 0 if diag is None else diag * tk
        s = _dot_nt(k_ref[0, h, pl.ds(start, tk), :], q_ref[0, h, q0:, :])
        if diag is not None:
            key = lax.broadcasted_iota(jnp.int32, s.shape, 0)
            qry = lax.broadcasted_iota(jnp.int32, s.shape, 1)
            s = jnp.where(key <= qry, s, NEG_BIG)
        s_scr[slot, :, q0:] = s
        mx_scr[slot, :, q0:] = jnp.max(s, axis=0, keepdims=True)

    def accumulate(h, kb, slot, diag=None):
        start = pl.multiple_of(kb * tk, tk)
        q0 = 0 if diag is None else diag * tk
        m_old = m_scr[h, :, q0:]
        m_new = jnp.maximum(m_old, mx_scr[slot, :, q0:])
        m_scr[h, :, q0:] = m_new
        p = jnp.exp2(s_scr[slot, :, q0:] - m_new).astype(BF16)
        acc_scr[h, :, q0:] = jnp.exp2(m_old - m_new) * acc_scr[h, :, q0:] + _dot(vt_ref[0, h, :, pl.ds(start, tk)], p)

    m_scr[...] = jnp.full_like(m_scr, NEG_BIG)
    acc_scr[...] = jnp.zeros_like(acc_scr)
    own = qi * sub
    lo = jnp.minimum(first_needed(0), first_needed(1))
    logits(0, own, 0, 0)
    logits(1, own, 1, 0)
    accumulate(0, own, 0, 0)
    for u in range(1, sub):
        logits(0, own + u, 0, u)
        accumulate(1, own + u - 1, 1, u - 1)
        logits(1, own + u, 1, u)
        accumulate(0, own + u, 0, u)
    last_own = own + sub - 1

    @pl.when(lo >= own)
    def _():
        accumulate(1, last_own, 1, sub - 1)

    @pl.when(lo < own)
    def _():
        logits(0, lo, 0, None)
        accumulate(1, last_own, 1, sub - 1)

        def body(kb, carry):
            logits(1, kb, 1, None)
            accumulate(0, kb, 0)
            logits(0, kb + 1, 0, None)
            accumulate(1, kb, 1)
            return carry

        lax.fori_loop(lo, own - 1, body, 0)
        logits(1, own - 1, 1, None)
        accumulate(0, own - 1, 0)
        accumulate(1, own - 1, 1)

    even, odd = acc_scr[0], acc_scr[1]
    row = lax.broadcasted_iota(jnp.int32, even.shape, 0)
    o = jnp.where(row < HEAD_DIM, even / even[HEAD_DIM : HEAD_DIM + 1, :], odd / odd[0:1, :])
    o_ref[...] = o.T.astype(o_ref.dtype)


def _fox_attention(q, k, vt, stats):
    bsz, nh, s, _ = q.shape
    tq = FOX_TQ
    sub = tq // FOX_TK
    nq = s // tq
    npair = nh // 2
    resident = pl.Buffered(1)
    st = stats[..., :4].transpose(0, 2, 1, 3)
    per_q = lambda v: v.reshape(bsz, nh, nq, sub)
    scalars = (
        per_q(st[..., 0]).max(axis=-1),
        lax.cummax(st[..., 1], axis=2),
        per_q(st[..., 2])[..., 0],
        st[..., 3],
    )
    scalars = tuple(v.reshape(-1) for v in scalars)
    return pl.pallas_call(
        _fox_kernel,
        grid_spec=pltpu.PrefetchScalarGridSpec(
            num_scalar_prefetch=len(scalars),
            grid=(bsz, npair, nq),
            in_specs=[
                pl.BlockSpec((1, 2, tq, LANES), lambda b, p, i, *_: (b, p, i, 0)),
                pl.BlockSpec((1, 2, s, LANES), lambda b, p, i, *_: (b, p, 0, 0), pipeline_mode=resident),
                pl.BlockSpec((1, 2, LANES, s), lambda b, p, i, *_: (b, p, 0, 0), pipeline_mode=resident),
            ],
            out_specs=pl.BlockSpec((tq, LANES), lambda b, p, i, *_: (b * nq + i, p)),
            scratch_shapes=[
                pltpu.VMEM((2, 1, tq), F32),
                pltpu.VMEM((2, LANES, tq), F32),
                pltpu.VMEM((2, FOX_TK, tq), F32),
                pltpu.VMEM((2, 1, tq), F32),
            ],
        ),
        out_shape=jax.ShapeDtypeStruct((bsz * s, nh * HEAD_DIM), BF16),
        compiler_params=_params(("parallel", "parallel", "arbitrary")),
        name="fox_attention",
    )(*scalars, q, k, vt)


def _lru_kernel(xy_ref, cw_ref, cb_ref, wg_ref, bg_ref, lam_ref, o_ref, tail_scr, h_scr):
    w = LRU_WIDTH

    @pl.when(pl.program_id(1) == 0)
    def _():
        tail_scr[...] = jnp.zeros_like(tail_scr)
        h_scr[...] = jnp.zeros_like(h_scr)

    x = xy_ref[:, :w]
    y = xy_ref[:, w:]
    tail = tail_scr[...]
    row8 = lax.broadcasted_iota(jnp.int32, (SUBLANES, w), 0)
    conv = x * cw_ref[CONV_WIDTH - 1 : CONV_WIDTH, :] + cb_ref[...]
    for j in range(1, CONV_WIDTH):
        xs = pltpu.roll(x, j, 0)
        head = jnp.where(row8 < j, pltpu.roll(tail, j, 0), xs[:SUBLANES])
        xs = jnp.concatenate([head, xs[SUBLANES:]], axis=0)
        conv = conv + xs * cw_ref[CONV_WIDTH - 1 - j : CONV_WIDTH - j, :]
    tail_scr[...] = x[-SUBLANES:]

    gates = jax.nn.sigmoid(_dot(conv.astype(BF16), wg_ref[...]) + bg_ref[...])
    r = gates[:, :w]
    i = gates[:, w:]
    log_a = -LRU_C * r * jax.nn.softplus(-lam_ref[...])
    a = jnp.exp(log_a)
    th = jnp.tanh(log_a)
    u = jnp.sqrt(-2.0 * th / (1.0 - th)) * (i * conv)
    h = _scan_rows(a, u, h_scr[...])
    h_scr[...] = h[-1:, :]
    o_ref[...] = (h * jax.nn.gelu(y)).astype(o_ref.dtype)


def _rg_lru(xy, cw, cb, wg, bg, lam, bsz):
    m = xy.shape[0]
    ts = SCAN_TILE
    nt = m // bsz // ts
    w = LRU_WIDTH
    return pl.pallas_call(
        _lru_kernel,
        grid=(bsz, nt),
        in_specs=[
            pl.BlockSpec((ts, 2 * w), lambda b, i: (b * nt + i, 0)),
            _full(cw.shape), _full(cb.shape), _full(wg.shape), _full(bg.shape), _full(lam.shape),
        ],
        out_specs=pl.BlockSpec((ts, w), lambda b, i: (b * nt + i, 0)),
        out_shape=jax.ShapeDtypeStruct((m, w), BF16),
        scratch_shapes=[pltpu.VMEM((SUBLANES, w), F32), pltpu.VMEM((1, w), F32)],
        compiler_params=_params(("parallel", "arbitrary")),
        name="rg_lru",
    )(xy, cw, cb, wg, bg, lam)


def _rope_kernel(pos_ref, freq_ref, cos_ref, sin_ref):
    ang = pos_ref[...].astype(F32) * freq_ref[...]
    lane = lax.broadcasted_iota(jnp.int32, ang.shape, 1)
    first_half = (lane % HEAD_DIM) < HEAD_DIM // 2
    cos_ref[...] = jnp.cos(ang)
    s = jnp.sin(ang)
    sin_ref[...] = jnp.where(first_half, -s, s)


def _rope_tables(pos, freq):
    m = pos.shape[0]
    tm = ROW_TILE
    spec = pl.BlockSpec((tm, LANES), lambda i: (i, 0))
    return pl.pallas_call(
        _rope_kernel,
        grid=(m // tm,),
        in_specs=[spec, _full(freq.shape)],
        out_specs=[spec, spec],
        out_shape=[jax.ShapeDtypeStruct((m, LANES), F32)] * 2,
        compiler_params=_params(("parallel",)),
        name="rope_tables",
    )(pos, freq)


def _group_mean(z, avg):
    hi, mid, lo = _split3(z)
    return _dot(hi.astype(BF16), avg) + _dot(mid.astype(BF16), avg) + _dot(lo.astype(BF16), avg)


def _ret_kernel(x_ref, cos_ref, sin_ref, decay_ref, kw_ref, qw_ref, cd_ref, avg_ref, o_ref, state_scr):
    c = RET_CHUNK
    w = RET_WIDTH
    npair = RET_HEADS // 2

    @pl.when(pl.program_id(1) == 0)
    def _():
        state_scr[...] = jnp.zeros_like(state_scr)

    cos = cos_ref[...]
    sin = sin_ref[...]
    lane = lax.broadcasted_iota(jnp.int32, cos.shape, 1)
    first_half = (lane % HEAD_DIM) < HEAD_DIM // 2
    tile_head = (lane < HEAD_DIM, lane >= HEAD_DIM)
    low_head = lax.broadcasted_iota(jnp.int32, (c, LANES), 1) < HEAD_DIM
    chunk_rows = [slice(n * c, (n + 1) * c) for n in range(x_ref.shape[0] // c)]

    def rotary(z):
        swapped = jnp.where(first_half, pltpu.roll(z, LANES - HEAD_DIM // 2, 1), pltpu.roll(z, HEAD_DIM // 2, 1))
        return z * cos + swapped * sin

    ys = []
    for p in range(npair):
        cols = slice(p * LANES, (p + 1) * LANES)
        q = rotary(x_ref[:, cols])
        k = rotary(x_ref[:, w + p * LANES : w + (p + 1) * LANES]) * HEAD_DIM**-0.5
        v = x_ref[:, 2 * w + p * LANES : 2 * w + (p + 1) * LANES].astype(BF16)
        kb = k.astype(BF16)
        q_head = [jnp.where(tile_head[h], q, 0.0).astype(BF16) for h in range(2)]
        q_cross = (q * qw_ref[:, cols]).astype(BF16)
        k_state = (k * kw_ref[:, cols]).astype(BF16)
        cd = cd_ref[p]
        incs = [jnp.where(cd > 0.0, _dot_tn(k_state[r], v[r]), 0.0) for r in chunk_rows]
        states = [state_scr[p]]
        for inc in incs:
            states.append(cd * states[-1] + inc)
        state_scr[p] = states[-1]
        chunks = []
        for n, r in enumerate(chunk_rows):
            inner = [
                _dot((_dot_nt(q_head[h][r], kb[r]) * decay_ref[2 * p + h]).astype(BF16), v[r]) for h in range(2)
            ]
            chunks.append(jnp.where(low_head, inner[0], inner[1]) + _dot(q_cross[r], states[n].astype(BF16)))
        ys.append(jnp.concatenate(chunks, axis=0))
    y = jnp.concatenate(ys, axis=1)
    avg = avg_ref[...]
    mu = _group_mean(y, avg)
    d = y - mu
    var = _group_mean(d * d, avg)
    yn = d * lax.rsqrt(var + EPS)
    g = x_ref[:, 3 * w :]
    o_ref[...] = (g * jax.nn.sigmoid(g) * yn).astype(o_ref.dtype)


def _retention(ret, cos, sin, decay, kw, qw, cd, avg, bsz):
    m = ret.shape[0]
    ts = ROW_TILE
    nt = m // bsz // ts
    w = RET_WIDTH
    return pl.pallas_call(
        _ret_kernel,
        grid=(bsz, nt),
        in_specs=[
            pl.BlockSpec((ts, 4 * w), lambda b, i: (b * nt + i, 0)),
            pl.BlockSpec((ts, LANES), lambda b, i: (b * nt + i, 0)),
            pl.BlockSpec((ts, LANES), lambda b, i: (b * nt + i, 0)),
            _full(decay.shape), _full(kw.shape), _full(qw.shape), _full(cd.shape), _full(avg.shape),
        ],
        out_specs=pl.BlockSpec((ts, w), lambda b, i: (b * nt + i, 0)),
        out_shape=jax.ShapeDtypeStruct((m, w), BF16),
        scratch_shapes=[pltpu.VMEM((RET_HEADS // 2, LANES, LANES), F32)],
        compiler_params=_params(("parallel", "arbitrary")),
        name="retention",
    )(ret, cos, sin, decay, kw, qw, cd, avg)


def _outproj_kernel(x_ref, lru_ref, fox_ref, ret_ref, w_ref, g_ref, o_ref):
    a, b = LRU_WIDTH, LRU_WIDTH + FOX_WIDTH
    y = _dot(lru_ref[...], w_ref[:a, :]) + _dot(fox_ref[...], w_ref[a:b, :]) + _dot(ret_ref[...], w_ref[b:, :])
    o_ref[...] = x_ref[...] + _rms(y, g_ref[...])


def _outproj(x, lru, fox, ret, w, g):
    m, d = x.shape
    tm = ROW_TILE
    row = lambda n: pl.BlockSpec((tm, n), lambda i: (i, 0))
    return pl.pallas_call(
        _outproj_kernel,
        grid=(m // tm,),
        in_specs=[row(d), row(lru.shape[1]), row(fox.shape[1]), row(ret.shape[1]), _full(w.shape), _full(g.shape)],
        out_specs=row(d),
        out_shape=jax.ShapeDtypeStruct((m, d), F32),
        compiler_params=_params(("parallel",)),
        name="mixer_outproj",
    )(x, lru, fox, ret, w, g)


def _norm_matmul_kernel(x_ref, g_ref, w_ref, o_ref):
    o_ref[...] = _dot(_rms(x_ref[...], g_ref[...]).astype(BF16), w_ref[...]).astype(o_ref.dtype)


def _norm_matmul(x, g, w, tm):
    m, d = x.shape
    n = w.shape[1]
    return pl.pallas_call(
        _norm_matmul_kernel,
        grid=(m // tm,),
        in_specs=[pl.BlockSpec((tm, d), lambda i: (i, 0)), _full(g.shape), _full(w.shape)],
        out_specs=pl.BlockSpec((tm, n), lambda i: (i, 0)),
        out_shape=jax.ShapeDtypeStruct((m, n), BF16),
        compiler_params=_params(("parallel",)),
        name="memory_kv_proj",
    )(x, g, w)


def _cross_kernel(x_ref, g1_ref, wq_ref, kt_ref, v_ref, wo_ref, g2_ref, o_ref):
    x = x_ref[...]
    d = x.shape[1]
    hd = d // CROSS_HEADS
    q = _dot(_rms(x, g1_ref[...]).astype(BF16), wq_ref[...]).astype(BF16)
    outs = []
    for h in range(CROSS_HEADS):
        cols = slice(h * hd, (h + 1) * hd)
        s = _dot(q[:, cols], kt_ref[0, cols, :]) * hd**-0.5
        e = jnp.exp(s - jnp.max(s, axis=-1, keepdims=True))
        p = e / jnp.sum(e, axis=-1, keepdims=True)
        outs.append(_dot(p.astype(BF16), v_ref[0, :, cols]).astype(BF16))
    o = jnp.concatenate(outs, axis=1)
    o_ref[...] = x + _rms(_dot(o, wo_ref[...]), g2_ref[...])


def _cross_attention(x, g1, wq, kt, v, wo, g2, bsz):
    m, d = x.shape
    tm = ROW_TILE
    nt = m // bsz // tm
    mem_len = v.shape[1]
    return pl.pallas_call(
        _cross_kernel,
        grid=(m // tm,),
        in_specs=[
            pl.BlockSpec((tm, d), lambda i: (i, 0)),
            _full(g1.shape), _full(wq.shape),
            pl.BlockSpec((1, d, mem_len), lambda i: (i // nt, 0, 0)),
            pl.BlockSpec((1, mem_len, d), lambda i: (i // nt, 0, 0)),
            _full(wo.shape), _full(g2.shape),
        ],
        out_specs=pl.BlockSpec((tm, d), lambda i: (i, 0)),
        out_shape=jax.ShapeDtypeStruct((m, d), F32),
        compiler_params=_params(("parallel",)),
        name="memory_cross_attention",
    )(x, g1, wq, kt, v, wo, g2)


def _ffn_kernel(x_ref, g1_ref, wg_ref, wu_ref, wd_ref, g2_ref, o_ref):
    x = x_ref[...]
    h = _rms(x, g1_ref[...]).astype(BF16)
    dff = wg_ref.shape[1]
    y = None
    for lo in range(0, dff, FFN_CHUNK):
        hi = min(lo + FFN_CHUNK, dff)
        gate = _dot(h, wg_ref[:, lo:hi])
        up = _dot(h, wu_ref[:, lo:hi])
        act = (gate * jax.nn.sigmoid(gate) * up).astype(BF16)
        part = _dot(act, wd_ref[lo:hi, :])
        y = part if y is None else y + part
    o_ref[...] = x + _rms(y, g2_ref[...])


def _ffn(x, g1, wg, wu, wd, g2):
    m, d = x.shape
    tm = ROW_TILE
    resident = lambda shape: pl.BlockSpec(shape, lambda i: (0, 0), pipeline_mode=pl.Buffered(1))
    return pl.pallas_call(
        _ffn_kernel,
        grid=(m // tm,),
        in_specs=[
            pl.BlockSpec((tm, d), lambda i: (i, 0)),
            _full(g1.shape), resident(wg.shape), resident(wu.shape), resident(wd.shape), _full(g2.shape),
        ],
        out_specs=pl.BlockSpec((tm, d), lambda i: (i, 0)),
        out_shape=jax.ShapeDtypeStruct((m, d), F32),
        compiler_params=_params(("parallel",)),
        name="swiglu_ffn",
    )(x, g1, wg, wu, wd, g2)


def _block_diag(w):
    n, a, b = w.shape
    out = jnp.zeros((n * a, n * b), w.dtype)
    for i in range(n):
        out = out.at[i * a : (i + 1) * a, i * b : (i + 1) * b].set(w[i])
    return out


def _retention_tables():
    c = RET_CHUNK
    log_gamma = jnp.log1p(-jnp.exp2(-5.0 - jnp.arange(RET_HEADS, dtype=F32)))
    idx = jnp.arange(c, dtype=F32)
    diff = idx[:, None] - idx[None, :]
    decay = jnp.where(diff >= 0, jnp.exp(log_gamma[:, None, None] * jnp.maximum(diff, 0.0)), 0.0)
    k_w = jnp.exp(log_gamma[None, :] * (c - 1.0 - idx)[:, None])
    q_w = jnp.exp(log_gamma[None, :] * (idx + 1.0)[:, None])
    chunk_decay = jnp.exp(log_gamma * c)
    reps = ROW_TILE // c
    kw = jnp.tile(jnp.repeat(k_w, HEAD_DIM, axis=1), (reps, 1))
    qw = jnp.tile(jnp.repeat(q_w, HEAD_DIM, axis=1), (reps, 1))
    eye = jnp.eye(2, dtype=F32)
    cd = jnp.stack([
        jnp.kron(eye * chunk_decay[2 * p : 2 * p + 2][None, :], jnp.ones((HEAD_DIM, HEAD_DIM), F32))
        for p in range(RET_HEADS // 2)
    ])
    avg = jnp.kron(jnp.eye(RET_HEADS, dtype=F32), jnp.full((HEAD_DIM, HEAD_DIM), 1.0 / HEAD_DIM, F32)).astype(BF16)
    return decay, kw, qw, cd, avg


def kernel(x, mem, positions, pre_mix_g, post_mix_g, w_in, conv_w, conv_b, w_rg, b_rg, w_ig, b_ig, lru_lambda, fox_b_f, w_out, pre_cross_g, post_cross_g, mem_norm_g, w_cq, w_ck, w_cv, w_co, pre_ffn_g, post_ffn_g, w_gate, w_up, w_down):
    bsz, seq, d = x.shape
    depth = w_in.shape[0]
    m = bsz * seq
    mem_len = mem.shape[1]
    xf = x.reshape(m, d)
    row = lambda v: v.reshape(1, -1)

    half = HEAD_DIM // 2
    inv_freq = ROPE_THETA ** (-jnp.arange(half, dtype=F32) / half)
    freq = jnp.tile(inv_freq, LANES // half).reshape(1, LANES)
    pos = jnp.broadcast_to(positions.reshape(m, 1), (m, LANES))
    cos, sin = _rope_tables(pos, freq)
    decay, kw, qw, cd, avg = _retention_tables()

    o_fox = 2 * LRU_WIDTH
    o_ff = o_fox + 3 * FOX_WIDTH
    o_ret = o_ff + FOX_HEADS

    for l in range(depth):
        wl = w_in[l, :, :o_fox].astype(BF16)
        wf = w_in[l, :, o_fox:o_ff].astype(BF16)
        wff = jnp.pad(w_in[l, :, o_ff:o_ret], ((0, 0), (0, LANES - FOX_HEADS))).astype(BF16)
        wr = w_in[l, :, o_ret:].astype(BF16)
        lru, fox, ret, ff = _inproj(xf, row(pre_mix_g[l]), wl, wf, wr, wff)

        bf = jnp.pad(fox_b_f[l], (0, LANES - FOX_HEADS)).reshape(1, LANES)
        fox_o = _fox_attention(*_fox_prep(fox, ff, bf, bsz))

        wg = jnp.concatenate([_block_diag(w_rg[l]), _block_diag(w_ig[l])], axis=1).astype(BF16)
        bg = jnp.concatenate([b_rg[l], b_ig[l]]).reshape(1, -1)
        lru_o = _rg_lru(lru, conv_w[l], row(conv_b[l]), wg, bg, row(lru_lambda[l]), bsz)

        ret_o = _retention(ret, cos, sin, decay, kw, qw, cd, avg, bsz)

        xf = _outproj(xf, lru_o, fox_o, ret_o, w_out[l].astype(BF16), row(post_mix_g[l]))

        wkv = jnp.concatenate([w_ck[l], w_cv[l]], axis=1).astype(BF16)
        kv = _norm_matmul(mem.reshape(bsz * mem_len, d), row(mem_norm_g), wkv, mem_len)
        kt = kv[:, :d].reshape(bsz, mem_len, d).transpose(0, 2, 1)
        vv = kv[:, d:].reshape(bsz, mem_len, d)
        xf = _cross_attention(xf, row(pre_cross_g[l]), w_cq[l].astype(BF16), kt, vv, w_co[l].astype(BF16), row(post_cross_g[l]), bsz)

        xf = _ffn(xf, row(pre_ffn_g[l]), w_gate[l].astype(BF16), w_up[l].astype(BF16), w_down[l].astype(BF16), row(post_ffn_g[l]))
    return xf.reshape(bsz, seq, d)
```

```python
import jax
import jax.numpy as jnp
from jax import lax
from jax.experimental import pallas as pl
from jax.experimental.pallas import tpu as pltpu

F32 = jnp.float32
BF16 = jnp.bfloat16

HEAD_DIM = 64
LRU_WIDTH = 256
LRU_BLOCKS = 4
CONV_WIDTH = 4
LRU_C = 8.0
FOX_HEADS = 8
FOX_WIDTH = FOX_HEADS * HEAD_DIM
RET_HEADS = 4
RET_WIDTH = RET_HEADS * HEAD_DIM
CROSS_HEADS = 4
RET_CHUNK = 128
ROPE_THETA = 10000.0
EPS = 1e-6

LANES = 128
SUBLANES = 8
VMEM_LIMIT = 48 * 1024 * 1024
TAIL_VMEM_LIMIT = 56 * 1024 * 1024

ROW_TILE = 512
FOX_TQ = 1024
FOX_TK = 512
SCAN_TILE = 256
FFN_CHUNK = 1024
NEG_BIG = -1e30
LOG2E = 1.4426950408889634
SKIP_LOG2 = 160.0
SKIP_REL = 2.0**-8
NORM_SAFETY = 1.0 + 2.0**-7


def _params(sem):
    return pltpu.CompilerParams(dimension_semantics=sem, vmem_limit_bytes=VMEM_LIMIT)


def _rms(x, g):
    ms = jnp.mean(x * x, axis=-1, keepdims=True)
    return x * lax.rsqrt(ms + EPS) * g


def _dot(a, b):
    return jnp.dot(a, b, preferred_element_type=F32)


def _dot_nt(a, b):
    return lax.dot_general(a, b, (((1,), (1,)), ((), ())), preferred_element_type=F32)


def _dot_tn(a, b):
    return lax.dot_general(a, b, (((0,), (0,)), ((), ())), preferred_element_type=F32)


def _full(shape):
    nd = len(shape)
    return pl.BlockSpec(shape, lambda *_: (0,) * nd)


def _scan_rows(a, b, carry):
    n = b.shape[0]
    sub = lax.broadcasted_iota(jnp.int32, b.shape, 0) % SUBLANES
    k = 1
    while k < SUBLANES:
        keep = sub >= k
        b_prev = jnp.where(keep, pltpu.roll(b, k, 0), 0.0)
        if a is None:
            b = b + b_prev
        else:
            a_prev = jnp.where(keep, pltpu.roll(a, k, 0), 1.0)
            b = a * b_prev + b
            a = a * a_prev
        k *= 2
    groups = []
    for g in range(n // SUBLANES):
        rows = slice(g * SUBLANES, (g + 1) * SUBLANES)
        h = b[rows] + carry if a is None else b[rows] + a[rows] * carry
        carry = h[SUBLANES - 1 :, :]
        groups.append(h)
    return jnp.concatenate(groups, axis=0)


def _inproj_kernel(x_ref, g_ref, wl_ref, wf_ref, wr_ref, wff_ref, lru_ref, fox_ref, ret_ref, ff_ref):
    h = _rms(x_ref[...], g_ref[...]).astype(BF16)
    lru_ref[...] = _dot(h, wl_ref[...])
    fox_ref[:, :FOX_WIDTH] = (_dot(h, wf_ref[:, :FOX_WIDTH]) * (HEAD_DIM**-0.5 * LOG2E)).astype(BF16)
    fox_ref[:, FOX_WIDTH:] = _dot(h, wf_ref[:, FOX_WIDTH:]).astype(BF16)
    ret_ref[...] = _dot(h, wr_ref[...])
    ff_ref[...] = _dot(h, wff_ref[...])


def _inproj(x, g, wl, wf, wr, wff):
    m, d = x.shape
    tm = ROW_TILE
    row = lambda n: pl.BlockSpec((tm, n), lambda i: (i, 0))
    return pl.pallas_call(
        _inproj_kernel,
        grid=(m // tm,),
        in_specs=[row(d), _full(g.shape), _full(wl.shape), _full(wf.shape), _full(wr.shape), _full(wff.shape)],
        out_specs=[row(wl.shape[1]), row(wf.shape[1]), row(wr.shape[1]), row(wff.shape[1])],
        out_shape=[
            jax.ShapeDtypeStruct((m, wl.shape[1]), F32),
            jax.ShapeDtypeStruct((m, wf.shape[1]), BF16),
            jax.ShapeDtypeStruct((m, wr.shape[1]), F32),
            jax.ShapeDtypeStruct((m, wff.shape[1]), F32),
        ],
        compiler_params=_params(("parallel",)),
        name="mixer_inproj",
    )(x, g, wl, wf, wr, wff)


def _split3(z):
    hi = z.astype(BF16).astype(F32)
    mid = (z - hi).astype(BF16).astype(F32)
    lo = (z - hi - mid).astype(BF16).astype(F32)
    return hi, mid, lo


def _fox_prep_kernel(fox_ref, ff_ref, bf_ref, place_ref, q_ref, k_ref, vt_ref, stats_ref, carry_ref):
    ts = fox_ref.shape[0]

    @pl.when(pl.program_id(1) == 0)
    def _():
        carry_ref[...] = jnp.zeros_like(carry_ref)

    c_nat = _scan_rows(None, jax.nn.log_sigmoid(ff_ref[...] + bf_ref[...]), carry_ref[...])
    carry_ref[...] = c_nat[-1:, :]
    ccol = c_nat * LOG2E
    terms = jnp.concatenate(_split3(ccol), axis=1).astype(BF16)
    placed = _dot(terms, place_ref[...])

    lane = lax.broadcasted_iota(jnp.int32, (ts, LANES), 1)
    stat_lane = lax.broadcasted_iota(jnp.int32, (1, LANES), 1)
    same_head = (lax.broadcasted_iota(jnp.int32, (LANES, LANES), 0) < HEAD_DIM) == (
        lax.broadcasted_iota(jnp.int32, (LANES, LANES), 1) < HEAD_DIM)
    head_ones = jnp.where(same_head, 1.0, 0.0).astype(BF16)
    stats = []
    for pair in range(FOX_HEADS // 2):
        group = lambda part: fox_ref[:, part * FOX_WIDTH + pair * LANES : part * FOX_WIDTH + (pair + 1) * LANES]
        gq, gk, gv = group(0), group(1), group(2)
        norms = []
        for g in (gq, gk):
            g32 = g.astype(F32)
            sq = _dot((g32 * g32).astype(BF16), head_ones)
            norms.append(jnp.sqrt(jnp.max(sq, axis=0, keepdims=True)) * NORM_SAFETY)
        for e in range(2):
            h = 2 * pair + e
            data = (lane < HEAD_DIM) if e == 0 else (lane >= HEAD_DIM)
            el = lane - (HEAD_DIM if e == 0 else 0)
            extras = placed[:, h * LANES : (h + 1) * LANES]
            q_extra = jnp.where((el >= 3) & (el < 6), 1.0, extras)
            k_extra = jnp.where((el >= 0) & (el < 3), 1.0, extras)
            v_extra = jnp.where(el == 0, 1.0, 0.0)
            q_ref[0, h] = jnp.where(data, gq, q_extra.astype(BF16))
            k_ref[0, h] = jnp.where(data, gk, k_extra.astype(BF16))
            vt_ref[0, h] = jnp.where(data, gv, v_extra.astype(BF16)).T
            first = e * HEAD_DIM
            q_norm = norms[0][:, first : first + 1]
            k_norm = norms[1][:, first : first + 1]
            c_first = ccol[0:1, h : h + 1]
            c_last = ccol[ts - 1 : ts, h : h + 1]
            stats.append(jnp.where(stat_lane == 0, q_norm, jnp.where(stat_lane == 1, k_norm, jnp.where(stat_lane == 2, c_first, c_last))))
    stats_ref[0, 0] = jnp.concatenate(stats, axis=0)


def _fox_placement():
    place = jnp.zeros((3 * LANES, FOX_HEADS * LANES), F32)
    for h in range(FOX_HEADS):
        first = h * LANES + (HEAD_DIM if h % 2 == 0 else 0)
        for t in range(3):
            place = place.at[t * LANES + h, first + t].set(1.0).at[t * LANES + h, first + 3 + t].set(-1.0)
    return place.astype(BF16)


def _fox_prep(fox, ff, bf, bsz):
    m = fox.shape[0]
    s = m // bsz
    ts = FOX_TK
    nt = s // ts
    place = _fox_placement()
    return pl.pallas_call(
        _fox_prep_kernel,
        grid=(bsz, nt),
        in_specs=[
            pl.BlockSpec((ts, fox.shape[1]), lambda b, i: (b * nt + i, 0)),
            pl.BlockSpec((ts, LANES), lambda b, i: (b * nt + i, 0)),
            _full(bf.shape),
            _full(place.shape),
        ],
        out_specs=[
            pl.BlockSpec((1, FOX_HEADS, ts, LANES), lambda b, i: (b, 0, i, 0)),
            pl.BlockSpec((1, FOX_HEADS, ts, LANES), lambda b, i: (b, 0, i, 0)),
            pl.BlockSpec((1, FOX_HEADS, LANES, ts), lambda b, i: (b, 0, 0, i)),
            pl.BlockSpec((1, 1, FOX_HEADS, LANES), lambda b, i: (b, i, 0, 0)),
        ],
        out_shape=[
            jax.ShapeDtypeStruct((bsz, FOX_HEADS, s, LANES), BF16),
            jax.ShapeDtypeStruct((bsz, FOX_HEADS, s, LANES), BF16),
            jax.ShapeDtypeStruct((bsz, FOX_HEADS, LANES, s), BF16),
            jax.ShapeDtypeStruct((bsz, nt, FOX_HEADS, LANES), F32),
        ],
        scratch_shapes=[pltpu.VMEM((1, LANES), F32)],
        compiler_params=_params(("parallel", "arbitrary")),
        name="fox_prep",
    )(fox, ff, bf, place)


def _fox_kernel(qn_ref, kn_ref, cf_ref, cl_ref, q_ref, k_ref, vt_ref, o_ref, m_scr, acc_scr, s_scr, mx_scr):
    tq = q_ref.shape[2]
    tk = FOX_TK
    sub = tq // tk
    nk = k_ref.shape[2] // tk
    nq = nk // sub
    qi = pl.program_id(2)

    def first_needed(h):
        head = (pl.program_id(0) * (FOX_HEADS // 2) + pl.program_id(1)) * 2 + h
        q_norm = qn_ref[head * nq + qi]
        c_first = cf_ref[head * nq + qi]
        floor = -q_norm * kn_ref[head * nk + qi * sub + sub - 1]

        def needed(kb):
            j = head * nk + jnp.maximum(kb, 0)
            dot_bound = q_norm * kn_ref[j]
            slack = SKIP_REL * (dot_bound - floor + jnp.abs(c_first) + jnp.abs(cl_ref[j]))
            return dot_bound + c_first - cl_ref[j] + slack >= floor - SKIP_LOG2

        kb = lax.while_loop(lambda kb: jnp.logical_and(kb >= 0, needed(kb)), lambda kb: kb - 1, qi * sub - 1)
        return kb + 1

    def logits(h, kb, slot, diag):
        start = pl.multiple_of(kb * tk, tk)
        q0 = 0 if diag is None else diag * tk
        s = _dot_nt(k_ref[0, h, pl.ds(start, tk), :], q_ref[0, h, q0:, :])
        if diag is not None:
            key = lax.broadcasted_iota(jnp.int32, s.shape, 0)
            qry = lax.broadcasted_iota(jnp.int32, s.shape, 1)
            s = jnp.where(key <= qry, s, NEG_BIG)
        s_scr[slot, :, q0:] = s
        mx_scr[slot, :, q0:] = jnp.max(s, axis=0, keepdims=True)

    def accumulate(h, kb, slot, diag=None):
        start = pl.multiple_of(kb * tk, tk)
        q0 = 0 if diag is None else diag * tk
        m_old = m_scr[h, :, q0:]
        m_new = jnp.maximum(m_old, mx_scr[slot, :, q0:])
        m_scr[h, :, q0:] = m_new
        p = jnp.exp2(s_scr[slot, :, q0:] - m_new).astype(BF16)
        acc_scr[h, :, q0:] = jnp.exp2(m_old - m_new) * acc_scr[h, :, q0:] + _dot(vt_ref[0, h, :, pl.ds(start, tk)], p)

    m_scr[...] = jnp.full_like(m_scr, NEG_BIG)
    acc_scr[...] = jnp.zeros_like(acc_scr)
    own = qi * sub
    lo = jnp.minimum(first_needed(0), first_needed(1))
    logits(0, own, 0, 0)
    logits(1, own, 1, 0)
    accumulate(0, own, 0, 0)
    for u in range(1, sub):
        logits(0, own + u, 0, u)
        accumulate(1, own + u - 1, 1, u - 1)
        logits(1, own + u, 1, u)
        accumulate(0, own + u, 0, u)
    last_own = own + sub - 1

    @pl.when(lo >= own)
    def _():
        accumulate(1, last_own, 1, sub - 1)

    @pl.when(lo < own)
    def _():
        logits(0, lo, 0, None)
        accumulate(1, last_own, 1, sub - 1)

        def body(kb, carry):
            logits(1, kb, 1, None)
            accumulate(0, kb, 0)
            logits(0, kb + 1, 0, None)
            accumulate(1, kb, 1)
            return carry

        lax.fori_loop(lo, own - 1, body, 0)
        logits(1, own - 1, 1, None)
        accumulate(0, own - 1, 0)
        accumulate(1, own - 1, 1)

    even, odd = acc_scr[0], acc_scr[1]
    row = lax.broadcasted_iota(jnp.int32, even.shape, 0)
    o = jnp.where(row < HEAD_DIM, even / even[HEAD_DIM : HEAD_DIM + 1, :], odd / odd[0:1, :])
    o_ref[...] = o.T.astype(o_ref.dtype)


def _fox_attention(q, k, vt, stats):
    bsz, nh, s, _ = q.shape
    tq = FOX_TQ
    sub = tq // FOX_TK
    nq = s // tq
    npair = nh // 2
    resident = pl.Buffered(1)
    st = stats[..., :4].transpose(0, 2, 1, 3)
    per_q = lambda v: v.reshape(bsz, nh, nq, sub)
    scalars = (
        per_q(st[..., 0]).max(axis=-1),
        lax.cummax(st[..., 1], axis=2),
        per_q(st[..., 2])[..., 0],
        st[..., 3],
    )
    scalars = tuple(v.reshape(-1) for v in scalars)
    return pl.pallas_call(
        _fox_kernel,
        grid_spec=pltpu.PrefetchScalarGridSpec(
            num_scalar_prefetch=len(scalars),
            grid=(bsz, npair, nq),
            in_specs=[
                pl.BlockSpec((1, 2, tq, LANES), lambda b, p, i, *_: (b, p, i, 0)),
                pl.BlockSpec((1, 2, s, LANES), lambda b, p, i, *_: (b, p, 0, 0), pipeline_mode=resident),
                pl.BlockSpec((1, 2, LANES, s), lambda b, p, i, *_: (b, p, 0, 0), pipeline_mode=resident),
            ],
            out_specs=pl.BlockSpec((tq, LANES), lambda b, p, i, *_: (b * nq + i, p)),
            scratch_shapes=[
                pltpu.VMEM((2, 1, tq), F32),
                pltpu.VMEM((2, LANES, tq), F32),
                pltpu.VMEM((2, FOX_TK, tq), F32),
                pltpu.VMEM((2, 1, tq), F32),
            ],
        ),
        out_shape=jax.ShapeDtypeStruct((bsz * s, nh * HEAD_DIM), BF16),
        compiler_params=_params(("parallel", "parallel", "arbitrary")),
        name="fox_attention",
    )(*scalars, q, k, vt)


def _lru_kernel(xy_ref, cw_ref, cb_ref, wg_ref, bg_ref, lam_ref, o_ref, tail_scr, h_scr):
    w = LRU_WIDTH

    @pl.when(pl.program_id(1) == 0)
    def _():
        tail_scr[...] = jnp.zeros_like(tail_scr)
        h_scr[...] = jnp.zeros_like(h_scr)

    x = xy_ref[:, :w]
    y = xy_ref[:, w:]
    tail = tail_scr[...]
    row8 = lax.broadcasted_iota(jnp.int32, (SUBLANES, w), 0)
    conv = x * cw_ref[CONV_WIDTH - 1 : CONV_WIDTH, :] + cb_ref[...]
    for j in range(1, CONV_WIDTH):
        xs = pltpu.roll(x, j, 0)
        head = jnp.where(row8 < j, pltpu.roll(tail, j, 0), xs[:SUBLANES])
        xs = jnp.concatenate([head, xs[SUBLANES:]], axis=0)
        conv = conv + xs * cw_ref[CONV_WIDTH - 1 - j : CONV_WIDTH - j, :]
    tail_scr[...] = x[-SUBLANES:]

    gates = jax.nn.sigmoid(_dot(conv.astype(BF16), wg_ref[...]) + bg_ref[...])
    r = gates[:, :w]
    i = gates[:, w:]
    log_a = -LRU_C * r * jax.nn.softplus(-lam_ref[...])
    a = jnp.exp(log_a)
    th = jnp.tanh(log_a)
    u = jnp.sqrt(-2.0 * th / (1.0 - th)) * (i * conv)
    h = _scan_rows(a, u, h_scr[...])
    h_scr[...] = h[-1:, :]
    o_ref[...] = (h * jax.nn.gelu(y)).astype(o_ref.dtype)


def _rg_lru(xy, cw, cb, wg, bg, lam, bsz):
    m = xy.shape[0]
    ts = SCAN_TILE
    nt = m // bsz // ts
    w = LRU_WIDTH
    return pl.pallas_call(
        _lru_kernel,
        grid=(bsz, nt),
        in_specs=[
            pl.BlockSpec((ts, 2 * w), lambda b, i: (b * nt + i, 0)),
            _full(cw.shape), _full(cb.shape), _full(wg.shape), _full(bg.shape), _full(lam.shape),
        ],
        out_specs=pl.BlockSpec((ts, w), lambda b, i: (b * nt + i, 0)),
        out_shape=jax.ShapeDtypeStruct((m, w), BF16),
        scratch_shapes=[pltpu.VMEM((SUBLANES, w), F32), pltpu.VMEM((1, w), F32)],
        compiler_params=_params(("parallel", "arbitrary")),
        name="rg_lru",
    )(xy, cw, cb, wg, bg, lam)


def _rope_kernel(pos_ref, freq_ref, cos_ref, sin_ref):
    ang = pos_ref[...].astype(F32) * freq_ref[...]
    lane = lax.broadcasted_iota(jnp.int32, ang.shape, 1)
    first_half = (lane % HEAD_DIM) < HEAD_DIM // 2
    cos_ref[...] = jnp.cos(ang)
    s = jnp.sin(ang)
    sin_ref[...] = jnp.where(first_half, -s, s)


def _rope_tables(pos, freq):
    m = pos.shape[0]
    tm = ROW_TILE
    spec = pl.BlockSpec((tm, LANES), lambda i: (i, 0))
    return pl.pallas_call(
        _rope_kernel,
        grid=(m // tm,),
        in_specs=[spec, _full(freq.shape)],
        out_specs=[spec, spec],
        out_shape=[jax.ShapeDtypeStruct((m, LANES), F32)] * 2,
        compiler_params=_params(("parallel",)),
        name="rope_tables",
    )(pos, freq)


def _group_mean(z, avg):
    hi, mid, lo = _split3(z)
    return _dot(hi.astype(BF16), avg) + _dot(mid.astype(BF16), avg) + _dot(lo.astype(BF16), avg)


def _ret_kernel(x_ref, cos_ref, sin_ref, decay_ref, kw_ref, qw_ref, cd_ref, avg_ref, o_ref, state_scr):
    c = RET_CHUNK
    w = RET_WIDTH
    npair = RET_HEADS // 2

    @pl.when(pl.program_id(1) == 0)
    def _():
        state_scr[...] = jnp.zeros_like(state_scr)

    cos = cos_ref[...]
    sin = sin_ref[...]
    lane = lax.broadcasted_iota(jnp.int32, cos.shape, 1)
    first_half = (lane % HEAD_DIM) < HEAD_DIM // 2
    tile_head = (lane < HEAD_DIM, lane >= HEAD_DIM)
    low_head = lax.broadcasted_iota(jnp.int32, (c, LANES), 1) < HEAD_DIM
    chunk_rows = [slice(n * c, (n + 1) * c) for n in range(x_ref.shape[0] // c)]

    def rotary(z):
        swapped = jnp.where(first_half, pltpu.roll(z, LANES - HEAD_DIM // 2, 1), pltpu.roll(z, HEAD_DIM // 2, 1))
        return z * cos + swapped * sin

    ys = []
    for p in range(npair):
        cols = slice(p * LANES, (p + 1) * LANES)
        q = rotary(x_ref[:, cols])
        k = rotary(x_ref[:, w + p * LANES : w + (p + 1) * LANES]) * HEAD_DIM**-0.5
        v = x_ref[:, 2 * w + p * LANES : 2 * w + (p + 1) * LANES].astype(BF16)
        kb = k.astype(BF16)
        q_head = [jnp.where(tile_head[h], q, 0.0).astype(BF16) for h in range(2)]
        q_cross = (q * qw_ref[:, cols]).astype(BF16)
        k_state = (k * kw_ref[:, cols]).astype(BF16)
        cd = cd_ref[p]
        incs = [jnp.where(cd > 0.0, _dot_tn(k_state[r], v[r]), 0.0) for r in chunk_rows]
        states = [state_scr[p]]
        for inc in incs:
            states.append(cd * states[-1] + inc)
        state_scr[p] = states[-1]
        chunks = []
        for n, r in enumerate(chunk_rows):
            inner = [
                _dot((_dot_nt(q_head[h][r], kb[r]) * decay_ref[2 * p + h]).astype(BF16), v[r]) for h in range(2)
            ]
            chunks.append(jnp.where(low_head, inner[0], inner[1]) + _dot(q_cross[r], states[n].astype(BF16)))
        ys.append(jnp.concatenate(chunks, axis=0))
    y = jnp.concatenate(ys, axis=1)
    avg = avg_ref[...]
    mu = _group_mean(y, avg)
    d = y - mu
    var = _group_mean(d * d, avg)
    yn = d * lax.rsqrt(var + EPS)
    g = x_ref[:, 3 * w :]
    o_ref[...] = (g * jax.nn.sigmoid(g) * yn).astype(o_ref.dtype)


def _retention(ret, cos, sin, decay, kw, qw, cd, avg, bsz):
    m = ret.shape[0]
    ts = ROW_TILE
    nt = m // bsz // ts
    w = RET_WIDTH
    return pl.pallas_call(
        _ret_kernel,
        grid=(bsz, nt),
        in_specs=[
            pl.BlockSpec((ts, 4 * w), lambda b, i: (b * nt + i, 0)),
            pl.BlockSpec((ts, LANES), lambda b, i: (b * nt + i, 0)),
            pl.BlockSpec((ts, LANES), lambda b, i: (b * nt + i, 0)),
            _full(decay.shape), _full(kw.shape), _full(qw.shape), _full(cd.shape), _full(avg.shape),
        ],
        out_specs=pl.BlockSpec((ts, w), lambda b, i: (b * nt + i, 0)),
        out_shape=jax.ShapeDtypeStruct((m, w), BF16),
        scratch_shapes=[pltpu.VMEM((RET_HEADS // 2, LANES, LANES), F32)],
        compiler_params=_params(("parallel", "arbitrary")),
        name="retention",
    )(ret, cos, sin, decay, kw, qw, cd, avg)


def _norm_matmul_kernel(x_ref, g_ref, w_ref, o_ref):
    o_ref[...] = _dot(_rms(x_ref[...], g_ref[...]).astype(BF16), w_ref[...]).astype(o_ref.dtype)


def _norm_matmul(x, g, w, tm):
    m, d = x.shape
    n = w.shape[1]
    return pl.pallas_call(
        _norm_matmul_kernel,
        grid=(m // tm,),
        in_specs=[pl.BlockSpec((tm, d), lambda i: (i, 0)), _full(g.shape), _full(w.shape)],
        out_specs=pl.BlockSpec((tm, n), lambda i: (i, 0)),
        out_shape=jax.ShapeDtypeStruct((m, n), BF16),
        compiler_params=_params(("parallel",)),
        name="memory_kv_proj",
    )(x, g, w)


def _tail_kernel(x_ref, lru_ref, fox_ref, ret_ref, wm_ref, gm_ref, g1_ref, wq_ref, kt_ref, v_ref, wo_ref, g2_ref,
                 g3_ref, wg_ref, wu_ref, wd_ref, g4_ref, o_ref):
    d = x_ref.shape[1]
    a, b = LRU_WIDTH, LRU_WIDTH + FOX_WIDTH
    mix = _dot(lru_ref[...], wm_ref[:a, :]) + _dot(fox_ref[...], wm_ref[a:b, :]) + _dot(ret_ref[...], wm_ref[b:, :])
    x = x_ref[...] + _rms(mix, gm_ref[...])

    hd = d // CROSS_HEADS
    q = _dot(_rms(x, g1_ref[...]).astype(BF16), wq_ref[...]).astype(BF16)
    outs = []
    for h in range(CROSS_HEADS):
        cols = slice(h * hd, (h + 1) * hd)
        s = _dot(q[:, cols], kt_ref[0, cols, :]) * hd**-0.5
        e = jnp.exp(s - jnp.max(s, axis=-1, keepdims=True))
        p = e / jnp.sum(e, axis=-1, keepdims=True)
        outs.append(_dot(p.astype(BF16), v_ref[0, :, cols]).astype(BF16))
    x = x + _rms(_dot(jnp.concatenate(outs, axis=1), wo_ref[...]), g2_ref[...])

    h = _rms(x, g3_ref[...]).astype(BF16)
    dff = wg_ref.shape[1]
    y = None
    for lo in range(0, dff, FFN_CHUNK):
        hi = min(lo + FFN_CHUNK, dff)
        gate = _dot(h, wg_ref[:, lo:hi])
        up = _dot(h, wu_ref[:, lo:hi])
        act = (gate * jax.nn.sigmoid(gate) * up).astype(BF16)
        part = _dot(act, wd_ref[lo:hi, :])
        y = part if y is None else y + part
    o_ref[...] = x + _rms(y, g4_ref[...])


def _layer_tail(x, lru, fox, ret, wm, gm, g1, wq, kt, v, wo, g2, g3, wg, wu, wd, g4, bsz):
    m, d = x.shape
    tm = ROW_TILE
    nt = m // bsz // tm
    mem_len = v.shape[1]
    row = lambda n: pl.BlockSpec((tm, n), lambda i: (i, 0))
    resident = lambda arr: pl.BlockSpec(arr.shape, lambda i: (0,) * arr.ndim, pipeline_mode=pl.Buffered(1))
    return pl.pallas_call(
        _tail_kernel,
        grid=(m // tm,),
        in_specs=[
            row(d), row(lru.shape[1]), row(fox.shape[1]), row(ret.shape[1]),
            resident(wm), resident(gm), resident(g1), resident(wq),
            pl.BlockSpec((1, d, mem_len), lambda i: (i // nt, 0, 0)),
            pl.BlockSpec((1, mem_len, d), lambda i: (i // nt, 0, 0)),
            resident(wo), resident(g2), resident(g3), resident(wg), resident(wu), resident(wd), resident(g4),
        ],
        out_specs=row(d),
        out_shape=jax.ShapeDtypeStruct((m, d), F32),
        compiler_params=pltpu.CompilerParams(dimension_semantics=("parallel",), vmem_limit_bytes=TAIL_VMEM_LIMIT),
        name="layer_tail",
    )(x, lru, fox, ret, wm, gm, g1, wq, kt, v, wo, g2, g3, wg, wu, wd, g4)


def _block_diag(w):
    n, a, b = w.shape
    out = jnp.zeros((n * a, n * b), w.dtype)
    for i in range(n):
        out = out.at[i * a : (i + 1) * a, i * b : (i + 1) * b].set(w[i])
    return out


def _retention_tables():
    c = RET_CHUNK
    log_gamma = jnp.log1p(-jnp.exp2(-5.0 - jnp.arange(RET_HEADS, dtype=F32)))
    idx = jnp.arange(c, dtype=F32)
    diff = idx[:, None] - idx[None, :]
    decay = jnp.where(diff >= 0, jnp.exp(log_gamma[:, None, None] * jnp.maximum(diff, 0.0)), 0.0)
    k_w = jnp.exp(log_gamma[None, :] * (c - 1.0 - idx)[:, None])
    q_w = jnp.exp(log_gamma[None, :] * (idx + 1.0)[:, None])
    chunk_decay = jnp.exp(log_gamma * c)
    reps = ROW_TILE // c
    kw = jnp.tile(jnp.repeat(k_w, HEAD_DIM, axis=1), (reps, 1))
    qw = jnp.tile(jnp.repeat(q_w, HEAD_DIM, axis=1), (reps, 1))
    eye = jnp.eye(2, dtype=F32)
    cd = jnp.stack([
        jnp.kron(eye * chunk_decay[2 * p : 2 * p + 2][None, :], jnp.ones((HEAD_DIM, HEAD_DIM), F32))
        for p in range(RET_HEADS // 2)
    ])
    avg = jnp.kron(jnp.eye(RET_HEADS, dtype=F32), jnp.full((HEAD_DIM, HEAD_DIM), 1.0 / HEAD_DIM, F32)).astype(BF16)
    return decay, kw, qw, cd, avg


def kernel(x, mem, positions, pre_mix_g, post_mix_g, w_in, conv_w, conv_b, w_rg, b_rg, w_ig, b_ig, lru_lambda, fox_b_f, w_out, pre_cross_g, post_cross_g, mem_norm_g, w_cq, w_ck, w_cv, w_co, pre_ffn_g, post_ffn_g, w_gate, w_up, w_down):
    bsz, seq, d = x.shape
    depth = w_in.shape[0]
    m = bsz * seq
    mem_len = mem.shape[1]
    xf = x.reshape(m, d)
    row = lambda v: v.reshape(1, -1)

    half = HEAD_DIM // 2
    inv_freq = ROPE_THETA ** (-jnp.arange(half, dtype=F32) / half)
    freq = jnp.tile(inv_freq, LANES // half).reshape(1, LANES)
    pos = jnp.broadcast_to(positions.reshape(m, 1), (m, LANES))
    cos, sin = _rope_tables(pos, freq)
    decay, kw, qw, cd, avg = _retention_tables()

    o_fox = 2 * LRU_WIDTH
    o_ff = o_fox + 3 * FOX_WIDTH
    o_ret = o_ff + FOX_HEADS

    for l in range(depth):
        wl = w_in[l, :, :o_fox].astype(BF16)
        wf = w_in[l, :, o_fox:o_ff].astype(BF16)
        wff = jnp.pad(w_in[l, :, o_ff:o_ret], ((0, 0), (0, LANES - FOX_HEADS))).astype(BF16)
        wr = w_in[l, :, o_ret:].astype(BF16)
        lru, fox, ret, ff = _inproj(xf, row(pre_mix_g[l]), wl, wf, wr, wff)

        bf = jnp.pad(fox_b_f[l], (0, LANES - FOX_HEADS)).reshape(1, LANES)
        fox_o = _fox_attention(*_fox_prep(fox, ff, bf, bsz))

        wg = jnp.concatenate([_block_diag(w_rg[l]), _block_diag(w_ig[l])], axis=1).astype(BF16)
        bg = jnp.concatenate([b_rg[l], b_ig[l]]).reshape(1, -1)
        lru_o = _rg_lru(lru, conv_w[l], row(conv_b[l]), wg, bg, row(lru_lambda[l]), bsz)

        ret_o = _retention(ret, cos, sin, decay, kw, qw, cd, avg, bsz)

        wkv = jnp.concatenate([w_ck[l], w_cv[l]], axis=1).astype(BF16)
        kv = _norm_matmul(mem.reshape(bsz * mem_len, d), row(mem_norm_g), wkv, mem_len)
        kt = kv[:, :d].reshape(bsz, mem_len, d).transpose(0, 2, 1)
        vv = kv[:, d:].reshape(bsz, mem_len, d)
        xf = _layer_tail(
            xf, lru_o, fox_o, ret_o, w_out[l].astype(BF16), row(post_mix_g[l]),
            row(pre_cross_g[l]), w_cq[l].astype(BF16), kt, vv, w_co[l].astype(BF16), row(post_cross_g[l]),
            row(pre_ffn_g[l]), w_gate[l].astype(BF16), w_up[l].astype(BF16), w_down[l].astype(BF16), row(post_ffn_g[l]), bsz)
    return xf.reshape(bsz, seq, d)
```

```python
import jax
import jax.numpy as jnp
from jax import lax
from jax.experimental import pallas as pl
from jax.experimental.pallas import tpu as pltpu

F32 = jnp.float32
BF16 = jnp.bfloat16

HEAD_DIM = 64
LRU_WIDTH = 256
LRU_BLOCKS = 4
CONV_WIDTH = 4
LRU_C = 8.0
FOX_HEADS = 8
FOX_WIDTH = FOX_HEADS * HEAD_DIM
RET_HEADS = 4
RET_WIDTH = RET_HEADS * HEAD_DIM
CROSS_HEADS = 4
RET_CHUNK = 128
ROPE_THETA = 10000.0
EPS = 1e-6

LANES = 128
SUBLANES = 8
VMEM_LIMIT = 48 * 1024 * 1024
TAIL_VMEM_LIMIT = 56 * 1024 * 1024

ROW_TILE = 512
FOX_TQ = 1024
FOX_TK = 512
SCAN_TILE = 256
FFN_CHUNK = 1024
NEG_BIG = -1e30
LOG2E = 1.4426950408889634
SKIP_LOG2 = 152.0
SKIP_REL = 2.0**-8
SKIP_REL_C = 2.0**-18
NORM_SAFETY = 1.0 + 2.0**-7


def _params(sem):
    return pltpu.CompilerParams(dimension_semantics=sem, vmem_limit_bytes=VMEM_LIMIT)


def _rms(x, g):
    ms = jnp.mean(x * x, axis=-1, keepdims=True)
    return x * lax.rsqrt(ms + EPS) * g


def _dot(a, b):
    return jnp.dot(a, b, preferred_element_type=F32)


def _dot_nt(a, b):
    return lax.dot_general(a, b, (((1,), (1,)), ((), ())), preferred_element_type=F32)


def _dot_tn(a, b):
    return lax.dot_general(a, b, (((0,), (0,)), ((), ())), preferred_element_type=F32)


def _full(shape):
    nd = len(shape)
    return pl.BlockSpec(shape, lambda *_: (0,) * nd)


def _scan_rows(a, b, carry):
    n = b.shape[0]
    sub = lax.broadcasted_iota(jnp.int32, b.shape, 0) % SUBLANES
    k = 1
    while k < SUBLANES:
        keep = sub >= k
        b_prev = jnp.where(keep, pltpu.roll(b, k, 0), 0.0)
        if a is None:
            b = b + b_prev
        else:
            a_prev = jnp.where(keep, pltpu.roll(a, k, 0), 1.0)
            b = a * b_prev + b
            a = a * a_prev
        k *= 2
    groups = []
    for g in range(n // SUBLANES):
        rows = slice(g * SUBLANES, (g + 1) * SUBLANES)
        h = b[rows] + carry if a is None else b[rows] + a[rows] * carry
        carry = h[SUBLANES - 1 :, :]
        groups.append(h)
    return jnp.concatenate(groups, axis=0)


def _split3(z):
    hi = z.astype(BF16).astype(F32)
    mid = (z - hi).astype(BF16).astype(F32)
    lo = (z - hi - mid).astype(BF16).astype(F32)
    return hi, mid, lo


def _inproj_kernel(x_ref, g_ref, wl_ref, wf_ref, wr_ref, wff_ref, bf_ref, place_ref,
                   lru_ref, ret_ref, q_ref, k_ref, vt_ref, stats_ref, carry_ref):
    ts = x_ref.shape[0]

    @pl.when(pl.program_id(1) == 0)
    def _():
        carry_ref[...] = jnp.zeros_like(carry_ref)

    h_in = _rms(x_ref[...], g_ref[...]).astype(BF16)
    lru_ref[...] = _dot(h_in, wl_ref[...])
    ret_ref[...] = _dot(h_in, wr_ref[...])
    q_all = (_dot(h_in, wf_ref[:, :FOX_WIDTH]) * (HEAD_DIM**-0.5 * LOG2E)).astype(BF16)
    k_all = _dot(h_in, wf_ref[:, FOX_WIDTH : 2 * FOX_WIDTH]).astype(BF16)
    v_all = _dot(h_in, wf_ref[:, 2 * FOX_WIDTH :]).astype(BF16)

    log_f = jax.nn.log_sigmoid(_dot(h_in, wff_ref[...]) + bf_ref[...])
    c_nat = _scan_rows(None, log_f, carry_ref[...])
    carry_ref[...] = c_nat[-1:, :]
    ccol = c_nat * LOG2E
    terms = jnp.concatenate(_split3(ccol), axis=1).astype(BF16)
    placed = _dot(terms, place_ref[...])

    lane = lax.broadcasted_iota(jnp.int32, (ts, LANES), 1)
    stat_lane = lax.broadcasted_iota(jnp.int32, (1, LANES), 1)
    same_head = (lax.broadcasted_iota(jnp.int32, (LANES, LANES), 0) < HEAD_DIM) == (
        lax.broadcasted_iota(jnp.int32, (LANES, LANES), 1) < HEAD_DIM)
    head_ones = jnp.where(same_head, 1.0, 0.0).astype(BF16)
    stats = []
    for pair in range(FOX_HEADS // 2):
        cols = slice(pair * LANES, (pair + 1) * LANES)
        gq, gk, gv = q_all[:, cols], k_all[:, cols], v_all[:, cols]
        norms = []
        for g in (gq, gk):
            g32 = g.astype(F32)
            sq = _dot((g32 * g32).astype(BF16), head_ones)
            norms.append(jnp.sqrt(jnp.max(sq, axis=0, keepdims=True)) * NORM_SAFETY)
        for e in range(2):
            h = 2 * pair + e
            data = (lane < HEAD_DIM) if e == 0 else (lane >= HEAD_DIM)
            el = lane - (HEAD_DIM if e == 0 else 0)
            extras = placed[:, h * LANES : (h + 1) * LANES]
            q_extra = jnp.where((el >= 3) & (el < 6), 1.0, extras)
            k_extra = jnp.where((el >= 0) & (el < 3), 1.0, extras)
            v_extra = jnp.where(el == 0, 1.0, 0.0)
            q_ref[0, h] = jnp.where(data, gq, q_extra.astype(BF16))
            k_ref[0, h] = jnp.where(data, gk, k_extra.astype(BF16))
            vt_ref[0, h] = jnp.where(data, gv, v_extra.astype(BF16)).T
            first = e * HEAD_DIM
            q_norm = norms[0][:, first : first + 1]
            k_norm = norms[1][:, first : first + 1]
            c_first = ccol[0:1, h : h + 1]
            c_last = ccol[ts - 1 : ts, h : h + 1]
            stats.append(jnp.where(stat_lane == 0, q_norm, jnp.where(stat_lane == 1, k_norm, jnp.where(stat_lane == 2, c_first, c_last))))
    stats_ref[0, 0] = jnp.concatenate(stats, axis=0)


def _fox_placement():
    place = jnp.zeros((3 * LANES, FOX_HEADS * LANES), F32)
    for h in range(FOX_HEADS):
        first = h * LANES + (HEAD_DIM if h % 2 == 0 else 0)
        for t in range(3):
            place = place.at[t * LANES + h, first + t].set(1.0).at[t * LANES + h, first + 3 + t].set(-1.0)
    return place.astype(BF16)


def _inproj(x, g, wl, wf, wr, wff, bf, bsz):
    m, d = x.shape
    s = m // bsz
    ts = FOX_TK
    nt = s // ts
    place = _fox_placement()
    row = lambda n: pl.BlockSpec((ts, n), lambda b, i: (b * nt + i, 0))
    per_head = pl.BlockSpec((1, FOX_HEADS, ts, LANES), lambda b, i: (b, 0, i, 0))
    return pl.pallas_call(
        _inproj_kernel,
        grid=(bsz, nt),
        in_specs=[row(d), _full(g.shape), _full(wl.shape), _full(wf.shape), _full(wr.shape), _full(wff.shape),
                  _full(bf.shape), _full(place.shape)],
        out_specs=[
            row(wl.shape[1]), row(wr.shape[1]), per_head, per_head,
            pl.BlockSpec((1, FOX_HEADS, LANES, ts), lambda b, i: (b, 0, 0, i)),
            pl.BlockSpec((1, 1, FOX_HEADS, LANES), lambda b, i: (b, i, 0, 0)),
        ],
        out_shape=[
            jax.ShapeDtypeStruct((m, wl.shape[1]), F32),
            jax.ShapeDtypeStruct((m, wr.shape[1]), F32),
            jax.ShapeDtypeStruct((bsz, FOX_HEADS, s, LANES), BF16),
            jax.ShapeDtypeStruct((bsz, FOX_HEADS, s, LANES), BF16),
            jax.ShapeDtypeStruct((bsz, FOX_HEADS, LANES, s), BF16),
            jax.ShapeDtypeStruct((bsz, nt, FOX_HEADS, LANES), F32),
        ],
        scratch_shapes=[pltpu.VMEM((1, LANES), F32)],
        compiler_params=_params(("parallel", "arbitrary")),
        name="mixer_inproj",
    )(x, g, wl, wf, wr, wff, bf, place)


def _fox_kernel(qn_ref, kn_ref, cf_ref, cl_ref, q_ref, k_ref, vt_ref, o_ref, m_scr, acc_scr, s_scr, mx_scr):
    tq = q_ref.shape[2]
    tk = FOX_TK
    sub = tq // tk
    nk = k_ref.shape[2] // tk
    nq = nk // sub
    qi = pl.program_id(2)

    def first_needed(h):
        head = (pl.program_id(0) * (FOX_HEADS // 2) + pl.program_id(1)) * 2 + h
        q_norm = qn_ref[head * nq + qi]
        c_first = cf_ref[head * nq + qi]
        floor = -q_norm * kn_ref[head * nk + qi * sub + sub - 1]

        def needed(kb):
            j = head * nk + jnp.maximum(kb, 0)
            dot_bound = q_norm * kn_ref[j]
            slack = SKIP_REL * (dot_bound - floor) + SKIP_REL_C * (jnp.abs(c_first) + jnp.abs(cl_ref[j]))
            return dot_bound + c_first - cl_ref[j] + slack >= floor - SKIP_LOG2

        kb = lax.while_loop(lambda kb: jnp.logical_and(kb >= 0, needed(kb)), lambda kb: kb - 1, qi * sub - 1)
        return kb + 1

    def logits(h, kb, slot, diag):
        start = pl.multiple_of(kb * tk, tk)
        q0 = 0 if diag is None else diag * tk
        s = _dot_nt(k_ref[0, h, pl.ds(start, tk), :], q_ref[0, h, q0:, :])
        if diag is not None:
            key = lax.broadcasted_iota(jnp.int32, s.shape, 0)
            qry = lax.broadcasted_iota(jnp.int32, s.shape, 1)
            s = jnp.where(key <= qry, s, NEG_BIG)
        s_scr[slot, :, q0:] = s
        mx_scr[slot, :, q0:] = jnp.max(s, axis=0, keepdims=True)

    def accumulate(h, kb, slot, diag=None):
        start = pl.multiple_of(kb * tk, tk)
        q0 = 0 if diag is None else diag * tk
        m_old = m_scr[h, :, q0:]
        m_new = jnp.maximum(m_old, mx_scr[slot, :, q0:])
        m_scr[h, :, q0:] = m_new
        p = jnp.exp2(s_scr[slot, :, q0:] - m_new).astype(BF16)
        acc_scr[h, :, q0:] = jnp.exp2(m_old - m_new) * acc_scr[h, :, q0:] + _dot(vt_ref[0, h, :, pl.ds(start, tk)], p)

    m_scr[...] = jnp.full_like(m_scr, NEG_BIG)
    acc_scr[...] = jnp.zeros_like(acc_scr)
    own = qi * sub
    lo = jnp.minimum(first_needed(0), first_needed(1))
    logits(0, own, 0, 0)
    logits(1, own, 1, 0)
    accumulate(0, own, 0, 0)
    for u in range(1, sub):
        logits(0, own + u, 0, u)
        accumulate(1, own + u - 1, 1, u - 1)
        logits(1, own + u, 1, u)
        accumulate(0, own + u, 0, u)
    last_own = own + sub - 1

    @pl.when(lo >= own)
    def _():
        accumulate(1, last_own, 1, sub - 1)

    @pl.when(lo < own)
    def _():
        logits(0, lo, 0, None)
        accumulate(1, last_own, 1, sub - 1)

        def body(kb, carry):
            logits(1, kb, 1, None)
            accumulate(0, kb, 0)
            logits(0, kb + 1, 0, None)
            accumulate(1, kb, 1)
            return carry

        lax.fori_loop(lo, own - 1, body, 0)
        logits(1, own - 1, 1, None)
        accumulate(0, own - 1, 0)
        accumulate(1, own - 1, 1)

    even, odd = acc_scr[0], acc_scr[1]
    row = lax.broadcasted_iota(jnp.int32, even.shape, 0)
    o = jnp.where(row < HEAD_DIM, even / even[HEAD_DIM : HEAD_DIM + 1, :], odd / odd[0:1, :])
    o_ref[...] = o.T.astype(o_ref.dtype)


def _fox_attention(q, k, vt, stats):
    bsz, nh, s, _ = q.shape
    tq = FOX_TQ
    sub = tq // FOX_TK
    nq = s // tq
    npair = nh // 2
    resident = pl.Buffered(1)
    st = stats[..., :4].transpose(0, 2, 1, 3)
    per_q = lambda v: v.reshape(bsz, nh, nq, sub)
    scalars = (
        per_q(st[..., 0]).max(axis=-1),
        lax.cummax(st[..., 1], axis=2),
        per_q(st[..., 2])[..., 0],
        st[..., 3],
    )
    scalars = tuple(v.reshape(-1) for v in scalars)
    return pl.pallas_call(
        _fox_kernel,
        grid_spec=pltpu.PrefetchScalarGridSpec(
            num_scalar_prefetch=len(scalars),
            grid=(bsz, npair, nq),
            in_specs=[
                pl.BlockSpec((1, 2, tq, LANES), lambda b, p, i, *_: (b, p, i, 0)),
                pl.BlockSpec((1, 2, s, LANES), lambda b, p, i, *_: (b, p, 0, 0), pipeline_mode=resident),
                pl.BlockSpec((1, 2, LANES, s), lambda b, p, i, *_: (b, p, 0, 0), pipeline_mode=resident),
            ],
            out_specs=pl.BlockSpec((tq, LANES), lambda b, p, i, *_: (b * nq + i, p)),
            scratch_shapes=[
                pltpu.VMEM((2, 1, tq), F32),
                pltpu.VMEM((2, LANES, tq), F32),
                pltpu.VMEM((2, FOX_TK, tq), F32),
                pltpu.VMEM((2, 1, tq), F32),
            ],
        ),
        out_shape=jax.ShapeDtypeStruct((bsz * s, nh * HEAD_DIM), BF16),
        compiler_params=_params(("parallel", "parallel", "arbitrary")),
        name="fox_attention",
    )(*scalars, q, k, vt)


def _lru_kernel(xy_ref, cw_ref, cb_ref, wg_ref, bg_ref, lam_ref, o_ref, tail_scr, h_scr):
    w = LRU_WIDTH

    @pl.when(pl.program_id(1) == 0)
    def _():
        tail_scr[...] = jnp.zeros_like(tail_scr)
        h_scr[...] = jnp.zeros_like(h_scr)

    x = xy_ref[:, :w]
    y = xy_ref[:, w:]
    tail = tail_scr[...]
    row8 = lax.broadcasted_iota(jnp.int32, (SUBLANES, w), 0)
    conv = x * cw_ref[CONV_WIDTH - 1 : CONV_WIDTH, :] + cb_ref[...]
    for j in range(1, CONV_WIDTH):
        xs = pltpu.roll(x, j, 0)
        head = jnp.where(row8 < j, pltpu.roll(tail, j, 0), xs[:SUBLANES])
        xs = jnp.concatenate([head, xs[SUBLANES:]], axis=0)
        conv = conv + xs * cw_ref[CONV_WIDTH - 1 - j : CONV_WIDTH - j, :]
    tail_scr[...] = x[-SUBLANES:]

    gates = jax.nn.sigmoid(_dot(conv.astype(BF16), wg_ref[...]) + bg_ref[...])
    r = gates[:, :w]
    i = gates[:, w:]
    log_a = -LRU_C * r * jax.nn.softplus(-lam_ref[...])
    a = jnp.exp(log_a)
    th = jnp.tanh(log_a)
    u = jnp.sqrt(-2.0 * th / (1.0 - th)) * (i * conv)
    h = _scan_rows(a, u, h_scr[...])
    h_scr[...] = h[-1:, :]
    o_ref[...] = (h * jax.nn.gelu(y)).astype(o_ref.dtype)


def _rg_lru(xy, cw, cb, wg, bg, lam, bsz):
    m = xy.shape[0]
    ts = SCAN_TILE
    nt = m // bsz // ts
    w = LRU_WIDTH
    return pl.pallas_call(
        _lru_kernel,
        grid=(bsz, nt),
        in_specs=[
            pl.BlockSpec((ts, 2 * w), lambda b, i: (b * nt + i, 0)),
            _full(cw.shape), _full(cb.shape), _full(wg.shape), _full(bg.shape), _full(lam.shape),
        ],
        out_specs=pl.BlockSpec((ts, w), lambda b, i: (b * nt + i, 0)),
        out_shape=jax.ShapeDtypeStruct((m, w), BF16),
        scratch_shapes=[pltpu.VMEM((SUBLANES, w), F32), pltpu.VMEM((1, w), F32)],
        compiler_params=_params(("parallel", "arbitrary")),
        name="rg_lru",
    )(xy, cw, cb, wg, bg, lam)


def _rope_kernel(pos_ref, freq_ref, cos_ref, sin_ref):
    ang = pos_ref[...].astype(F32) * freq_ref[...]
    lane = lax.broadcasted_iota(jnp.int32, ang.shape, 1)
    first_half = (lane % HEAD_DIM) < HEAD_DIM // 2
    cos_ref[...] = jnp.cos(ang)
    s = jnp.sin(ang)
    sin_ref[...] = jnp.where(first_half, -s, s)


def _rope_tables(pos, freq):
    m = pos.shape[0]
    tm = ROW_TILE
    spec = pl.BlockSpec((tm, LANES), lambda i: (i, 0))
    return pl.pallas_call(
        _rope_kernel,
        grid=(m // tm,),
        in_specs=[spec, _full(freq.shape)],
        out_specs=[spec, spec],
        out_shape=[jax.ShapeDtypeStruct((m, LANES), F32)] * 2,
        compiler_params=_params(("parallel",)),
        name="rope_tables",
    )(pos, freq)


def _group_mean(z, avg):
    hi, mid, lo = _split3(z)
    return _dot(hi.astype(BF16), avg) + _dot(mid.astype(BF16), avg) + _dot(lo.astype(BF16), avg)


def _ret_kernel(x_ref, cos_ref, sin_ref, decay_ref, kw_ref, qw_ref, cd_ref, avg_ref, o_ref, state_scr):
    c = RET_CHUNK
    w = RET_WIDTH
    npair = RET_HEADS // 2

    @pl.when(pl.program_id(1) == 0)
    def _():
        state_scr[...] = jnp.zeros_like(state_scr)

    cos = cos_ref[...]
    sin = sin_ref[...]
    lane = lax.broadcasted_iota(jnp.int32, cos.shape, 1)
    first_half = (lane % HEAD_DIM) < HEAD_DIM // 2
    tile_head = (lane < HEAD_DIM, lane >= HEAD_DIM)
    low_head = lax.broadcasted_iota(jnp.int32, (c, LANES), 1) < HEAD_DIM
    chunk_rows = [slice(n * c, (n + 1) * c) for n in range(x_ref.shape[0] // c)]

    def rotary(z):
        swapped = jnp.where(first_half, pltpu.roll(z, LANES - HEAD_DIM // 2, 1), pltpu.roll(z, HEAD_DIM // 2, 1))
        return z * cos + swapped * sin

    ys = []
    for p in range(npair):
        cols = slice(p * LANES, (p + 1) * LANES)
        q = rotary(x_ref[:, cols])
        k = rotary(x_ref[:, w + p * LANES : w + (p + 1) * LANES]) * HEAD_DIM**-0.5
        v = x_ref[:, 2 * w + p * LANES : 2 * w + (p + 1) * LANES].astype(BF16)
        kb = k.astype(BF16)
        q_head = [jnp.where(tile_head[h], q, 0.0).astype(BF16) for h in range(2)]
        q_cross = (q * qw_ref[:, cols]).astype(BF16)
        k_state = (k * kw_ref[:, cols]).astype(BF16)
        cd = cd_ref[p]
        incs = [jnp.where(cd > 0.0, _dot_tn(k_state[r], v[r]), 0.0) for r in chunk_rows]
        states = [state_scr[p]]
        for inc in incs:
            states.append(cd * states[-1] + inc)
        state_scr[p] = states[-1]
        chunks = []
        for n, r in enumerate(chunk_rows):
            inner = [
                _dot((_dot_nt(q_head[h][r], kb[r]) * decay_ref[2 * p + h]).astype(BF16), v[r]) for h in range(2)
            ]
            chunks.append(jnp.where(low_head, inner[0], inner[1]) + _dot(q_cross[r], states[n].astype(BF16)))
        ys.append(jnp.concatenate(chunks, axis=0))
    y = jnp.concatenate(ys, axis=1)
    avg = avg_ref[...]
    mu = _group_mean(y, avg)
    d = y - mu
    var = _group_mean(d * d, avg)
    yn = d * lax.rsqrt(var + EPS)
    g = x_ref[:, 3 * w :]
    o_ref[...] = (g * jax.nn.sigmoid(g) * yn).astype(o_ref.dtype)


def _retention(ret, cos, sin, decay, kw, qw, cd, avg, bsz):
    m = ret.shape[0]
    ts = ROW_TILE
    nt = m // bsz // ts
    w = RET_WIDTH
    return pl.pallas_call(
        _ret_kernel,
        grid=(bsz, nt),
        in_specs=[
            pl.BlockSpec((ts, 4 * w), lambda b, i: (b * nt + i, 0)),
            pl.BlockSpec((ts, LANES), lambda b, i: (b * nt + i, 0)),
            pl.BlockSpec((ts, LANES), lambda b, i: (b * nt + i, 0)),
            _full(decay.shape), _full(kw.shape), _full(qw.shape), _full(cd.shape), _full(avg.shape),
        ],
        out_specs=pl.BlockSpec((ts, w), lambda b, i: (b * nt + i, 0)),
        out_shape=jax.ShapeDtypeStruct((m, w), BF16),
        scratch_shapes=[pltpu.VMEM((RET_HEADS // 2, LANES, LANES), F32)],
        compiler_params=_params(("parallel", "arbitrary")),
        name="retention",
    )(ret, cos, sin, decay, kw, qw, cd, avg)


def _norm_matmul_kernel(x_ref, g_ref, w_ref, o_ref):
    o_ref[...] = _dot(_rms(x_ref[...], g_ref[...]).astype(BF16), w_ref[...]).astype(o_ref.dtype)


def _norm_matmul(x, g, w, tm):
    m, d = x.shape
    n = w.shape[1]
    return pl.pallas_call(
        _norm_matmul_kernel,
        grid=(m // tm,),
        in_specs=[pl.BlockSpec((tm, d), lambda i: (i, 0)), _full(g.shape), _full(w.shape)],
        out_specs=pl.BlockSpec((tm, n), lambda i: (i, 0)),
        out_shape=jax.ShapeDtypeStruct((m, n), BF16),
        compiler_params=_params(("parallel",)),
        name="memory_kv_proj",
    )(x, g, w)


def _tail_kernel(x_ref, lru_ref, fox_ref, ret_ref, wm_ref, gm_ref, g1_ref, wq_ref, kt_ref, v_ref, wo_ref, g2_ref,
                 g3_ref, wg_ref, wu_ref, wd_ref, g4_ref, o_ref):
    d = x_ref.shape[1]
    a, b = LRU_WIDTH, LRU_WIDTH + FOX_WIDTH
    mix = _dot(lru_ref[...], wm_ref[:a, :]) + _dot(fox_ref[...], wm_ref[a:b, :]) + _dot(ret_ref[...], wm_ref[b:, :])
    x = x_ref[...] + _rms(mix, gm_ref[...])

    hd = d // CROSS_HEADS
    q = _dot(_rms(x, g1_ref[...]).astype(BF16), wq_ref[...]).astype(BF16)
    outs = []
    for h in range(CROSS_HEADS):
        cols = slice(h * hd, (h + 1) * hd)
        s = _dot(q[:, cols], kt_ref[0, cols, :]) * hd**-0.5
        e = jnp.exp(s - jnp.max(s, axis=-1, keepdims=True))
        p = e / jnp.sum(e, axis=-1, keepdims=True)
        outs.append(_dot(p.astype(BF16), v_ref[0, :, cols]).astype(BF16))
    x = x + _rms(_dot(jnp.concatenate(outs, axis=1), wo_ref[...]), g2_ref[...])

    h = _rms(x, g3_ref[...]).astype(BF16)
    dff = wg_ref.shape[1]
    y = None
    for lo in range(0, dff, FFN_CHUNK):
        hi = min(lo + FFN_CHUNK, dff)
        gate = _dot(h, wg_ref[:, lo:hi])
        up = _dot(h, wu_ref[:, lo:hi])
        act = (gate * jax.nn.sigmoid(gate) * up).astype(BF16)
        part = _dot(act, wd_ref[lo:hi, :])
        y = part if y is None else y + part
    o_ref[...] = x + _rms(y, g4_ref[...])


def _layer_tail(x, lru, fox, ret, wm, gm, g1, wq, kt, v, wo, g2, g3, wg, wu, wd, g4, bsz):
    m, d = x.shape
    tm = ROW_TILE
    nt = m // bsz // tm
    mem_len = v.shape[1]
    row = lambda n: pl.BlockSpec((tm, n), lambda i: (i, 0))
    resident = lambda arr: pl.BlockSpec(arr.shape, lambda i: (0,) * arr.ndim, pipeline_mode=pl.Buffered(1))
    return pl.pallas_call(
        _tail_kernel,
        grid=(m // tm,),
        in_specs=[
            row(d), row(lru.shape[1]), row(fox.shape[1]), row(ret.shape[1]),
            resident(wm), resident(gm), resident(g1), resident(wq),
            pl.BlockSpec((1, d, mem_len), lambda i: (i // nt, 0, 0)),
            pl.BlockSpec((1, mem_len, d), lambda i: (i // nt, 0, 0)),
            resident(wo), resident(g2), resident(g3), resident(wg), resident(wu), resident(wd), resident(g4),
        ],
        out_specs=row(d),
        out_shape=jax.ShapeDtypeStruct((m, d), F32),
        compiler_params=pltpu.CompilerParams(dimension_semantics=("parallel",), vmem_limit_bytes=TAIL_VMEM_LIMIT),
        name="layer_tail",
    )(x, lru, fox, ret, wm, gm, g1, wq, kt, v, wo, g2, g3, wg, wu, wd, g4)


def _block_diag(w):
    n, a, b = w.shape
    out = jnp.zeros((n * a, n * b), w.dtype)
    for i in range(n):
        out = out.at[i * a : (i + 1) * a, i * b : (i + 1) * b].set(w[i])
    return out


def _retention_tables():
    c = RET_CHUNK
    log_gamma = jnp.log1p(-jnp.exp2(-5.0 - jnp.arange(RET_HEADS, dtype=F32)))
    idx = jnp.arange(c, dtype=F32)
    diff = idx[:, None] - idx[None, :]
    decay = jnp.where(diff >= 0, jnp.exp(log_gamma[:, None, None] * jnp.maximum(diff, 0.0)), 0.0)
    k_w = jnp.exp(log_gamma[None, :] * (c - 1.0 - idx)[:, None])
    q_w = jnp.exp(log_gamma[None, :] * (idx + 1.0)[:, None])
    chunk_decay = jnp.exp(log_gamma * c)
    reps = ROW_TILE // c
    kw = jnp.tile(jnp.repeat(k_w, HEAD_DIM, axis=1), (reps, 1))
    qw = jnp.tile(jnp.repeat(q_w, HEAD_DIM, axis=1), (reps, 1))
    eye = jnp.eye(2, dtype=F32)
    cd = jnp.stack([
        jnp.kron(eye * chunk_decay[2 * p : 2 * p + 2][None, :], jnp.ones((HEAD_DIM, HEAD_DIM), F32))
        for p in range(RET_HEADS // 2)
    ])
    avg = jnp.kron(jnp.eye(RET_HEADS, dtype=F32), jnp.full((HEAD_DIM, HEAD_DIM), 1.0 / HEAD_DIM, F32)).astype(BF16)
    return decay, kw, qw, cd, avg


def kernel(x, mem, positions, pre_mix_g, post_mix_g, w_in, conv_w, conv_b, w_rg, b_rg, w_ig, b_ig, lru_lambda, fox_b_f, w_out, pre_cross_g, post_cross_g, mem_norm_g, w_cq, w_ck, w_cv, w_co, pre_ffn_g, post_ffn_g, w_gate, w_up, w_down):
    bsz, seq, d = x.shape
    depth = w_in.shape[0]
    m = bsz * seq
    mem_len = mem.shape[1]
    xf = x.reshape(m, d)
    row = lambda v: v.reshape(1, -1)

    half = HEAD_DIM // 2
    inv_freq = ROPE_THETA ** (-jnp.arange(half, dtype=F32) / half)
    freq = jnp.tile(inv_freq, LANES // half).reshape(1, LANES)
    pos = jnp.broadcast_to(positions.reshape(m, 1), (m, LANES))
    cos, sin = _rope_tables(pos, freq)
    decay, kw, qw, cd, avg = _retention_tables()

    o_fox = 2 * LRU_WIDTH
    o_ff = o_fox + 3 * FOX_WIDTH
    o_ret = o_ff + FOX_HEADS

    for l in range(depth):
        order = jnp.argsort(fox_b_f[l])
        by_head = lambda w: w.reshape(d, FOX_HEADS, HEAD_DIM)[:, order].reshape(d, FOX_WIDTH)
        wl = w_in[l, :, :o_fox].astype(BF16)
        wf = jnp.concatenate(
            [by_head(w_in[l, :, o_fox + i * FOX_WIDTH : o_fox + (i + 1) * FOX_WIDTH]) for i in range(3)], axis=1
        ).astype(BF16)
        wff = jnp.pad(w_in[l, :, o_ff:o_ret][:, order], ((0, 0), (0, LANES - FOX_HEADS))).astype(BF16)
        wr = w_in[l, :, o_ret:].astype(BF16)
        bf = jnp.pad(fox_b_f[l][order], (0, LANES - FOX_HEADS)).reshape(1, LANES)
        fox_rows = w_out[l, LRU_WIDTH : LRU_WIDTH + FOX_WIDTH].reshape(FOX_HEADS, HEAD_DIM, d)[order].reshape(FOX_WIDTH, d)
        wm = jnp.concatenate([w_out[l, :LRU_WIDTH], fox_rows, w_out[l, LRU_WIDTH + FOX_WIDTH :]], axis=0).astype(BF16)
        lru, ret, *fox_operands = _inproj(xf, row(pre_mix_g[l]), wl, wf, wr, wff, bf, bsz)
        fox_o = _fox_attention(*fox_operands)

        wg = jnp.concatenate([_block_diag(w_rg[l]), _block_diag(w_ig[l])], axis=1).astype(BF16)
        bg = jnp.concatenate([b_rg[l], b_ig[l]]).reshape(1, -1)
        lru_o = _rg_lru(lru, conv_w[l], row(conv_b[l]), wg, bg, row(lru_lambda[l]), bsz)

        ret_o = _retention(ret, cos, sin, decay, kw, qw, cd, avg, bsz)

        wkv = jnp.concatenate([w_ck[l], w_cv[l]], axis=1).astype(BF16)
        kv = _norm_matmul(mem.reshape(bsz * mem_len, d), row(mem_norm_g), wkv, mem_len)
        kt = kv[:, :d].reshape(bsz, mem_len, d).transpose(0, 2, 1)
        vv = kv[:, d:].reshape(bsz, mem_len, d)
        xf = _layer_tail(
            xf, lru_o, fox_o, ret_o, wm, row(post_mix_g[l]),
            row(pre_cross_g[l]), w_cq[l].astype(BF16), kt, vv, w_co[l].astype(BF16), row(post_cross_g[l]),
            row(pre_ffn_g[l]), w_gate[l].astype(BF16), w_up[l].astype(BF16), w_down[l].astype(BF16), row(post_ffn_g[l]), bsz)
    return xf.reshape(bsz, seq, d)
```

```python
import jax
import jax.numpy as jnp
import numpy as np
from jax import lax
from jax.experimental import pallas as pl
from jax.experimental.pallas import tpu as pltpu

F32 = jnp.float32
BF16 = jnp.bfloat16

HEAD_DIM = 64
LRU_WIDTH = 256
LRU_BLOCKS = 4
CONV_WIDTH = 4
LRU_C = 8.0
FOX_HEADS = 8
FOX_WIDTH = FOX_HEADS * HEAD_DIM
RET_HEADS = 4
RET_WIDTH = RET_HEADS * HEAD_DIM
CROSS_HEADS = 4
RET_CHUNK = 128
RET_GROUP = 2
ROPE_THETA = 10000.0
EPS = 1e-6

LANES = 128
SUBLANES = 8
VMEM_LIMIT = 48 * 1024 * 1024
TAIL_VMEM_LIMIT = 56 * 1024 * 1024

ROW_TILE = 512
FOX_TQ = 1024
FOX_TK = 512
SCAN_TILE = 256
FFN_CHUNK = 1024
NEG_BIG = -1e30
LOG2E = 1.4426950408889634
SKIP_LOG2 = 152.0
SKIP_REL = 2.0**-8
SKIP_REL_C = 2.0**-18
NORM_SAFETY = 1.0 + 2.0**-7


def _params(sem):
    return pltpu.CompilerParams(dimension_semantics=sem, vmem_limit_bytes=VMEM_LIMIT)


def _rms(x, g):
    ms = jnp.mean(x * x, axis=-1, keepdims=True)
    return x * lax.rsqrt(ms + EPS) * g


def _dot(a, b):
    return jnp.dot(a, b, preferred_element_type=F32)


def _dot_nt(a, b):
    return lax.dot_general(a, b, (((1,), (1,)), ((), ())), preferred_element_type=F32)


def _dot_tn(a, b):
    return lax.dot_general(a, b, (((0,), (0,)), ((), ())), preferred_element_type=F32)


def _full(shape):
    nd = len(shape)
    return pl.BlockSpec(shape, lambda *_: (0,) * nd)


def _scan_rows(a, b, carry):
    n = b.shape[0]
    sub = lax.broadcasted_iota(jnp.int32, b.shape, 0) % SUBLANES
    k = 1
    while k < SUBLANES:
        keep = sub >= k
        b_prev = jnp.where(keep, pltpu.roll(b, k, 0), 0.0)
        if a is None:
            b = b + b_prev
        else:
            a_prev = jnp.where(keep, pltpu.roll(a, k, 0), 1.0)
            b = a * b_prev + b
            a = a * a_prev
        k *= 2
    groups = []
    for g in range(n // SUBLANES):
        rows = slice(g * SUBLANES, (g + 1) * SUBLANES)
        h = b[rows] + carry if a is None else b[rows] + a[rows] * carry
        carry = h[SUBLANES - 1 :, :]
        groups.append(h)
    return jnp.concatenate(groups, axis=0)


def _split3(z):
    hi = z.astype(BF16).astype(F32)
    mid = (z - hi).astype(BF16).astype(F32)
    lo = (z - hi - mid).astype(BF16).astype(F32)
    return hi, mid, lo


def _inproj_kernel(x_ref, g_ref, wl_ref, wf_ref, wr_ref, wff_ref, bf_ref, place_ref,
                   lru_ref, ret_ref, q_ref, k_ref, vt_ref, stats_ref, carry_ref):
    ts = x_ref.shape[0]

    @pl.when(pl.program_id(1) == 0)
    def _():
        carry_ref[...] = jnp.zeros_like(carry_ref)

    h_in = _rms(x_ref[...], g_ref[...]).astype(BF16)
    lru_ref[...] = _dot(h_in, wl_ref[...])
    ret_ref[...] = _dot(h_in, wr_ref[...])
    q_all = (_dot(h_in, wf_ref[:, :FOX_WIDTH]) * (HEAD_DIM**-0.5 * LOG2E)).astype(BF16)
    k_all = _dot(h_in, wf_ref[:, FOX_WIDTH : 2 * FOX_WIDTH]).astype(BF16)
    v_all = _dot(h_in, wf_ref[:, 2 * FOX_WIDTH :]).astype(BF16)

    log_f = jax.nn.log_sigmoid(_dot(h_in, wff_ref[...]) + bf_ref[...])
    c_nat = _scan_rows(None, log_f, carry_ref[...])
    carry_ref[...] = c_nat[-1:, :]
    ccol = c_nat * LOG2E
    terms = jnp.concatenate(_split3(ccol), axis=1).astype(BF16)
    placed = _dot(terms, place_ref[...])

    lane = lax.broadcasted_iota(jnp.int32, (ts, LANES), 1)
    stat_lane = lax.broadcasted_iota(jnp.int32, (1, LANES), 1)
    same_head = (lax.broadcasted_iota(jnp.int32, (LANES, LANES), 0) < HEAD_DIM) == (
        lax.broadcasted_iota(jnp.int32, (LANES, LANES), 1) < HEAD_DIM)
    head_ones = jnp.where(same_head, 1.0, 0.0).astype(BF16)
    stats = []
    for pair in range(FOX_HEADS // 2):
        cols = slice(pair * LANES, (pair + 1) * LANES)
        gq, gk, gv = q_all[:, cols], k_all[:, cols], v_all[:, cols]
        norms = []
        for g in (gq, gk):
            g32 = g.astype(F32)
            sq = _dot((g32 * g32).astype(BF16), head_ones)
            norms.append(jnp.sqrt(jnp.max(sq, axis=0, keepdims=True)) * NORM_SAFETY)
        for e in range(2):
            h = 2 * pair + e
            data = (lane < HEAD_DIM) if e == 0 else (lane >= HEAD_DIM)
            el = lane - (HEAD_DIM if e == 0 else 0)
            extras = placed[:, h * LANES : (h + 1) * LANES]
            q_extra = jnp.where((el >= 3) & (el < 6), 1.0, extras)
            k_extra = jnp.where((el >= 0) & (el < 3), 1.0, extras)
            v_extra = jnp.where(el == 0, 1.0, 0.0)
            q_ref[0, h] = jnp.where(data, gq, q_extra.astype(BF16))
            k_ref[0, h] = jnp.where(data, gk, k_extra.astype(BF16))
            vt_ref[0, h] = jnp.where(data, gv, v_extra.astype(BF16)).T
            first = e * HEAD_DIM
            q_norm = norms[0][:, first : first + 1]
            k_norm = norms[1][:, first : first + 1]
            c_first = ccol[0:1, h : h + 1]
            c_last = ccol[ts - 1 : ts, h : h + 1]
            stats.append(jnp.where(stat_lane == 0, q_norm, jnp.where(stat_lane == 1, k_norm, jnp.where(stat_lane == 2, c_first, c_last))))
    stats_ref[0, 0] = jnp.concatenate(stats, axis=0)


def _fox_placement():
    place = np.zeros((3 * LANES, FOX_HEADS * LANES), np.float32)
    for h in range(FOX_HEADS):
        first = h * LANES + (HEAD_DIM if h % 2 == 0 else 0)
        for t in range(3):
            place[t * LANES + h, first + t] = 1.0
            place[t * LANES + h, first + 3 + t] = -1.0
    return jnp.asarray(place, dtype=BF16)


def _inproj(x, g, wl, wf, wr, wff, bf, bsz):
    m, d = x.shape
    s = m // bsz
    ts = FOX_TK
    nt = s // ts
    place = _fox_placement()
    row = lambda n: pl.BlockSpec((ts, n), lambda b, i: (b * nt + i, 0))
    per_head = pl.BlockSpec((1, FOX_HEADS, ts, LANES), lambda b, i: (b, 0, i, 0))
    return pl.pallas_call(
        _inproj_kernel,
        grid=(bsz, nt),
        in_specs=[row(d), _full(g.shape), _full(wl.shape), _full(wf.shape), _full(wr.shape), _full(wff.shape),
                  _full(bf.shape), _full(place.shape)],
        out_specs=[
            row(wl.shape[1]), row(wr.shape[1]), per_head, per_head,
            pl.BlockSpec((1, FOX_HEADS, LANES, ts), lambda b, i: (b, 0, 0, i)),
            pl.BlockSpec((1, 1, FOX_HEADS, LANES), lambda b, i: (b, i, 0, 0)),
        ],
        out_shape=[
            jax.ShapeDtypeStruct((m, wl.shape[1]), F32),
            jax.ShapeDtypeStruct((m, wr.shape[1]), F32),
            jax.ShapeDtypeStruct((bsz, FOX_HEADS, s, LANES), BF16),
            jax.ShapeDtypeStruct((bsz, FOX_HEADS, s, LANES), BF16),
            jax.ShapeDtypeStruct((bsz, FOX_HEADS, LANES, s), BF16),
            jax.ShapeDtypeStruct((bsz, nt, FOX_HEADS, LANES), F32),
        ],
        scratch_shapes=[pltpu.VMEM((1, LANES), F32)],
        compiler_params=_params(("parallel", "arbitrary")),
        name="mixer_inproj",
    )(x, g, wl, wf, wr, wff, bf, place)


def _fox_kernel(qn_ref, kn_ref, cf_ref, cl_ref, q_ref, k_ref, vt_ref, o_ref, m_scr, acc_scr, s_scr, mx_scr):
    tq = q_ref.shape[2]
    tk = FOX_TK
    sub = tq // tk
    nk = k_ref.shape[2] // tk
    nq = nk // sub
    qi = pl.program_id(2)

    def first_needed(h):
        head = (pl.program_id(0) * (FOX_HEADS // 2) + pl.program_id(1)) * 2 + h
        q_norm = qn_ref[head * nq + qi]
        c_first = cf_ref[head * nq + qi]
        floor = -q_norm * kn_ref[head * nk + qi * sub + sub - 1]

        def needed(kb):
            j = head * nk + jnp.maximum(kb, 0)
            dot_bound = q_norm * kn_ref[j]
            slack = SKIP_REL * (dot_bound - floor) + SKIP_REL_C * (jnp.abs(c_first) + jnp.abs(cl_ref[j]))
            return dot_bound + c_first - cl_ref[j] + slack >= floor - SKIP_LOG2

        kb = lax.while_loop(lambda kb: jnp.logical_and(kb >= 0, needed(kb)), lambda kb: kb - 1, qi * sub - 1)
        return kb + 1

    def logits(h, kb, slot, diag):
        start = pl.multiple_of(kb * tk, tk)
        q0 = 0 if diag is None else diag * tk
        s = _dot_nt(k_ref[0, h, pl.ds(start, tk), :], q_ref[0, h, q0:, :])
        if diag is not None:
            key = lax.broadcasted_iota(jnp.int32, s.shape, 0)
            qry = lax.broadcasted_iota(jnp.int32, s.shape, 1)
            s = jnp.where(key <= qry, s, NEG_BIG)
        s_scr[slot, :, q0:] = s
        mx_scr[slot, :, q0:] = jnp.max(s, axis=0, keepdims=True)

    def accumulate(h, kb, slot, diag=None):
        start = pl.multiple_of(kb * tk, tk)
        q0 = 0 if diag is None else diag * tk
        m_old = m_scr[h, :, q0:]
        m_new = jnp.maximum(m_old, mx_scr[slot, :, q0:])
        m_scr[h, :, q0:] = m_new
        p = jnp.exp2(s_scr[slot, :, q0:] - m_new).astype(BF16)
        acc_scr[h, :, q0:] = jnp.exp2(m_old - m_new) * acc_scr[h, :, q0:] + _dot(vt_ref[0, h, :, pl.ds(start, tk)], p)

    m_scr[...] = jnp.full_like(m_scr, NEG_BIG)
    acc_scr[...] = jnp.zeros_like(acc_scr)
    own = qi * sub
    lo = jnp.minimum(first_needed(0), first_needed(1))
    logits(0, own, 0, 0)
    logits(1, own, 1, 0)
    accumulate(0, own, 0, 0)
    for u in range(1, sub):
        logits(0, own + u, 0, u)
        accumulate(1, own + u - 1, 1, u - 1)
        logits(1, own + u, 1, u)
        accumulate(0, own + u, 0, u)
    last_own = own + sub - 1

    @pl.when(lo >= own)
    def _():
        accumulate(1, last_own, 1, sub - 1)

    @pl.when(lo < own)
    def _():
        logits(0, lo, 0, None)
        accumulate(1, last_own, 1, sub - 1)

        def body(kb, carry):
            logits(1, kb, 1, None)
            accumulate(0, kb, 0)
            logits(0, kb + 1, 0, None)
            accumulate(1, kb, 1)
            return carry

        lax.fori_loop(lo, own - 1, body, 0)
        logits(1, own - 1, 1, None)
        accumulate(0, own - 1, 0)
        accumulate(1, own - 1, 1)

    even, odd = acc_scr[0], acc_scr[1]
    row = lax.broadcasted_iota(jnp.int32, even.shape, 0)
    o = jnp.where(row < HEAD_DIM, even / even[HEAD_DIM : HEAD_DIM + 1, :], odd / odd[0:1, :])
    o_ref[...] = o.T.astype(o_ref.dtype)


def _fox_attention(q, k, vt, stats):
    bsz, nh, s, _ = q.shape
    tq = FOX_TQ
    sub = tq // FOX_TK
    nq = s // tq
    npair = nh // 2
    resident = pl.Buffered(1)
    st = stats[..., :4].transpose(0, 2, 1, 3)
    per_q = lambda v: v.reshape(bsz, nh, nq, sub)
    scalars = (
        per_q(st[..., 0]).max(axis=-1),
        lax.cummax(st[..., 1], axis=2),
        per_q(st[..., 2])[..., 0],
        st[..., 3],
    )
    scalars = tuple(v.reshape(-1) for v in scalars)
    return pl.pallas_call(
        _fox_kernel,
        grid_spec=pltpu.PrefetchScalarGridSpec(
            num_scalar_prefetch=len(scalars),
            grid=(bsz, npair, nq),
            in_specs=[
                pl.BlockSpec((1, 2, tq, LANES), lambda b, p, i, *_: (b, p, i, 0)),
                pl.BlockSpec((1, 2, s, LANES), lambda b, p, i, *_: (b, p, 0, 0), pipeline_mode=resident),
                pl.BlockSpec((1, 2, LANES, s), lambda b, p, i, *_: (b, p, 0, 0), pipeline_mode=resident),
            ],
            out_specs=pl.BlockSpec((tq, LANES), lambda b, p, i, *_: (b * nq + i, p)),
            scratch_shapes=[
                pltpu.VMEM((2, 1, tq), F32),
                pltpu.VMEM((2, LANES, tq), F32),
                pltpu.VMEM((2, FOX_TK, tq), F32),
                pltpu.VMEM((2, 1, tq), F32),
            ],
        ),
        out_shape=jax.ShapeDtypeStruct((bsz * s, nh * HEAD_DIM), BF16),
        compiler_params=_params(("parallel", "parallel", "arbitrary")),
        name="fox_attention",
    )(*scalars, q, k, vt)


def _lru_kernel(xy_ref, cw_ref, cb_ref, wg_ref, bg_ref, lam_ref, o_ref, tail_scr, h_scr):
    w = LRU_WIDTH

    @pl.when(pl.program_id(1) == 0)
    def _():
        tail_scr[...] = jnp.zeros_like(tail_scr)
        h_scr[...] = jnp.zeros_like(h_scr)

    x = xy_ref[:, :w]
    y = xy_ref[:, w:]
    tail = tail_scr[...]
    row8 = lax.broadcasted_iota(jnp.int32, (SUBLANES, w), 0)
    conv = x * cw_ref[CONV_WIDTH - 1 : CONV_WIDTH, :] + cb_ref[...]
    for j in range(1, CONV_WIDTH):
        xs = pltpu.roll(x, j, 0)
        head = jnp.where(row8 < j, pltpu.roll(tail, j, 0), xs[:SUBLANES])
        xs = jnp.concatenate([head, xs[SUBLANES:]], axis=0)
        conv = conv + xs * cw_ref[CONV_WIDTH - 1 - j : CONV_WIDTH - j, :]
    tail_scr[...] = x[-SUBLANES:]

    gates = jax.nn.sigmoid(_dot(conv.astype(BF16), wg_ref[...]) + bg_ref[...])
    r = gates[:, :w]
    i = gates[:, w:]
    log_a = -LRU_C * r * jax.nn.softplus(-lam_ref[...])
    a = jnp.exp(log_a)
    th = jnp.tanh(log_a)
    u = jnp.sqrt(-2.0 * th / (1.0 - th)) * (i * conv)
    h = _scan_rows(a, u, h_scr[...])
    h_scr[...] = h[-1:, :]
    o_ref[...] = (h * jax.nn.gelu(y)).astype(o_ref.dtype)


def _rg_lru(xy, cw, cb, wg, bg, lam, bsz):
    m = xy.shape[0]
    ts = SCAN_TILE
    nt = m // bsz // ts
    w = LRU_WIDTH
    return pl.pallas_call(
        _lru_kernel,
        grid=(bsz, nt),
        in_specs=[
            pl.BlockSpec((ts, 2 * w), lambda b, i: (b * nt + i, 0)),
            _full(cw.shape), _full(cb.shape), _full(wg.shape), _full(bg.shape), _full(lam.shape),
        ],
        out_specs=pl.BlockSpec((ts, w), lambda b, i: (b * nt + i, 0)),
        out_shape=jax.ShapeDtypeStruct((m, w), BF16),
        scratch_shapes=[pltpu.VMEM((SUBLANES, w), F32), pltpu.VMEM((1, w), F32)],
        compiler_params=_params(("parallel", "arbitrary")),
        name="rg_lru",
    )(xy, cw, cb, wg, bg, lam)


def _rope_kernel(pos_ref, freq_ref, cos_ref, sin_ref):
    ang = pos_ref[...].astype(F32) * freq_ref[...]
    lane = lax.broadcasted_iota(jnp.int32, ang.shape, 1)
    first_half = (lane % HEAD_DIM) < HEAD_DIM // 2
    cos_ref[...] = jnp.cos(ang)
    s = jnp.sin(ang)
    sin_ref[...] = jnp.where(first_half, -s, s)


def _rope_tables(pos, freq):
    m = pos.shape[0]
    tm = ROW_TILE
    spec = pl.BlockSpec((tm, LANES), lambda i: (i, 0))
    return pl.pallas_call(
        _rope_kernel,
        grid=(m // tm,),
        in_specs=[spec, _full(freq.shape)],
        out_specs=[spec, spec],
        out_shape=[jax.ShapeDtypeStruct((m, LANES), F32)] * 2,
        compiler_params=_params(("parallel",)),
        name="rope_tables",
    )(pos, freq)


def _group_mean(z, avg):
    hi, mid, lo = _split3(z)
    return _dot(hi.astype(BF16), avg) + _dot(mid.astype(BF16), avg) + _dot(lo.astype(BF16), avg)


def _ret_kernel(x_ref, cos_ref, sin_ref, decay_ref, kw_ref, qw_ref, cd_ref, avg_ref, o_ref, state_scr):
    c = RET_CHUNK
    w = RET_WIDTH
    npair = RET_HEADS // 2

    @pl.when(pl.program_id(1) == 0)
    def _():
        state_scr[...] = jnp.zeros_like(state_scr)

    cos = cos_ref[...]
    sin = sin_ref[...]
    lane = lax.broadcasted_iota(jnp.int32, cos.shape, 1)
    first_half = (lane % HEAD_DIM) < HEAD_DIM // 2

    def rotary(z):
        swapped = jnp.where(first_half, pltpu.roll(z, LANES - HEAD_DIM // 2, 1), pltpu.roll(z, HEAD_DIM // 2, 1))
        return z * cos + swapped * sin

    ts = x_ref.shape[0]
    nchunk = ts // c
    row_chunk = lax.broadcasted_iota(jnp.int32, (ts, LANES), 0) // c
    ys = []
    for p in range(npair):
        cols = slice(p * LANES, (p + 1) * LANES)
        q = rotary(x_ref[:, cols])
        k = rotary(x_ref[:, w + p * LANES : w + (p + 1) * LANES]) * HEAD_DIM**-0.5
        v = x_ref[:, 2 * w + p * LANES : 2 * w + (p + 1) * LANES].astype(BF16)
        kb = k.astype(BF16)
        q_cross = (q * qw_ref[:, cols]).astype(BF16)
        k_state = (k * kw_ref[:, cols]).astype(BF16)
        zero = jnp.zeros_like(v)
        v_blocks = jnp.concatenate([jnp.where(row_chunk == n, v, zero) for n in range(nchunk)], axis=1)
        incs = _dot_tn(k_state, v_blocks)
        cd = cd_ref[p]
        states = [state_scr[p]]
        for n in range(nchunk):
            states.append(cd * states[-1] + jnp.where(cd > 0.0, incs[:, n * LANES : (n + 1) * LANES], 0.0))
        state_scr[p] = states[-1]
        q_blocks = jnp.concatenate([jnp.where(row_chunk == n, q_cross, zero) for n in range(nchunk)], axis=1)
        y = _dot(q_blocks, jnp.concatenate([st.astype(BF16) for st in states[:nchunk]], axis=0))
        span = RET_GROUP * c
        span_low = lax.broadcasted_iota(jnp.int32, (span, LANES), 1) < HEAD_DIM
        span_head = (span_low, jnp.logical_not(span_low))
        inner = []
        for r0 in range(0, ts, span):
            r = slice(r0, r0 + span)
            per_head = []
            for h in range(2):
                q_head = jnp.where(span_head[h], q[r], 0.0).astype(BF16)
                scores = _dot_nt(q_head, kb[r]) * decay_ref[2 * p + h]
                per_head.append(_dot(scores.astype(BF16), v[r]))
            inner.append(jnp.where(span_low, per_head[0], per_head[1]))
        ys.append(y + jnp.concatenate(inner, axis=0))
    y = jnp.concatenate(ys, axis=1)
    avg = avg_ref[...]
    mu = _group_mean(y, avg)
    d = y - mu
    var = _group_mean(d * d, avg)
    yn = d * lax.rsqrt(var + EPS)
    g = x_ref[:, 3 * w :]
    o_ref[...] = (g * jax.nn.sigmoid(g) * yn).astype(o_ref.dtype)


def _retention(ret, cos, sin, decay, kw, qw, cd, avg, bsz):
    m = ret.shape[0]
    ts = ROW_TILE
    nt = m // bsz // ts
    w = RET_WIDTH
    return pl.pallas_call(
        _ret_kernel,
        grid=(bsz, nt),
        in_specs=[
            pl.BlockSpec((ts, 4 * w), lambda b, i: (b * nt + i, 0)),
            pl.BlockSpec((ts, LANES), lambda b, i: (b * nt + i, 0)),
            pl.BlockSpec((ts, LANES), lambda b, i: (b * nt + i, 0)),
            _full(decay.shape), _full(kw.shape), _full(qw.shape), _full(cd.shape), _full(avg.shape),
        ],
        out_specs=pl.BlockSpec((ts, w), lambda b, i: (b * nt + i, 0)),
        out_shape=jax.ShapeDtypeStruct((m, w), BF16),
        scratch_shapes=[pltpu.VMEM((RET_HEADS // 2, LANES, LANES), F32)],
        compiler_params=_params(("parallel", "arbitrary")),
        name="retention",
    )(ret, cos, sin, decay, kw, qw, cd, avg)


def _norm_matmul_kernel(x_ref, g_ref, w_ref, o_ref):
    o_ref[...] = _dot(_rms(x_ref[...], g_ref[...]).astype(BF16), w_ref[...]).astype(o_ref.dtype)


def _norm_matmul(x, g, w, tm):
    m, d = x.shape
    n = w.shape[1]
    return pl.pallas_call(
        _norm_matmul_kernel,
        grid=(m // tm,),
        in_specs=[pl.BlockSpec((tm, d), lambda i: (i, 0)), _full(g.shape), _full(w.shape)],
        out_specs=pl.BlockSpec((tm, n), lambda i: (i, 0)),
        out_shape=jax.ShapeDtypeStruct((m, n), BF16),
        compiler_params=_params(("parallel",)),
        name="memory_kv_proj",
    )(x, g, w)


def _tail_kernel(x_ref, lru_ref, fox_ref, ret_ref, wm_ref, gm_ref, g1_ref, wq_ref, kt_ref, v_ref, wo_ref, g2_ref,
                 g3_ref, wg_ref, wu_ref, wd_ref, g4_ref, o_ref):
    d = x_ref.shape[1]
    a, b = LRU_WIDTH, LRU_WIDTH + FOX_WIDTH
    mix = _dot(lru_ref[...], wm_ref[:a, :]) + _dot(fox_ref[...], wm_ref[a:b, :]) + _dot(ret_ref[...], wm_ref[b:, :])
    x = x_ref[...] + _rms(mix, gm_ref[...])

    hd = d // CROSS_HEADS
    q = _dot(_rms(x, g1_ref[...]).astype(BF16), wq_ref[...]).astype(BF16)
    outs = []
    for h in range(CROSS_HEADS):
        cols = slice(h * hd, (h + 1) * hd)
        s = _dot(q[:, cols], kt_ref[0, cols, :]) * hd**-0.5
        e = jnp.exp(s - jnp.max(s, axis=-1, keepdims=True))
        p = e / jnp.sum(e, axis=-1, keepdims=True)
        outs.append(_dot(p.astype(BF16), v_ref[0, :, cols]).astype(BF16))
    x = x + _rms(_dot(jnp.concatenate(outs, axis=1), wo_ref[...]), g2_ref[...])

    h = _rms(x, g3_ref[...]).astype(BF16)
    dff = wg_ref.shape[1]
    y = None
    for lo in range(0, dff, FFN_CHUNK):
        hi = min(lo + FFN_CHUNK, dff)
        gate = _dot(h, wg_ref[:, lo:hi])
        up = _dot(h, wu_ref[:, lo:hi])
        act = (gate * jax.nn.sigmoid(gate) * up).astype(BF16)
        part = _dot(act, wd_ref[lo:hi, :])
        y = part if y is None else y + part
    o_ref[...] = x + _rms(y, g4_ref[...])


def _layer_tail(x, lru, fox, ret, wm, gm, g1, wq, kt, v, wo, g2, g3, wg, wu, wd, g4, bsz):
    m, d = x.shape
    tm = ROW_TILE
    nt = m // bsz // tm
    mem_len = v.shape[1]
    row = lambda n: pl.BlockSpec((tm, n), lambda i: (i, 0))
    resident = lambda arr: pl.BlockSpec(arr.shape, lambda i: (0,) * arr.ndim, pipeline_mode=pl.Buffered(1))
    return pl.pallas_call(
        _tail_kernel,
        grid=(m // tm,),
        in_specs=[
            row(d), row(lru.shape[1]), row(fox.shape[1]), row(ret.shape[1]),
            resident(wm), resident(gm), resident(g1), resident(wq),
            pl.BlockSpec((1, d, mem_len), lambda i: (i // nt, 0, 0)),
            pl.BlockSpec((1, mem_len, d), lambda i: (i // nt, 0, 0)),
            resident(wo), resident(g2), resident(g3), resident(wg), resident(wu), resident(wd), resident(g4),
        ],
        out_specs=row(d),
        out_shape=jax.ShapeDtypeStruct((m, d), F32),
        compiler_params=pltpu.CompilerParams(dimension_semantics=("parallel",), vmem_limit_bytes=TAIL_VMEM_LIMIT),
        name="layer_tail",
    )(x, lru, fox, ret, wm, gm, g1, wq, kt, v, wo, g2, g3, wg, wu, wd, g4)


def _block_diag(w):
    n, a, b = w.shape
    eye = jnp.eye(n, dtype=w.dtype)
    return (eye[:, None, :, None] * w[:, :, None, :]).reshape(n * a, n * b)


def _retention_tables():
    c = RET_CHUNK
    f32 = np.float32
    log_gamma = np.log1p(-np.exp2(-5.0 - np.arange(RET_HEADS, dtype=f32))).astype(f32)
    idx = np.arange(c, dtype=f32)
    diff = idx[:, None] - idx[None, :]
    decay = np.where(diff >= 0, np.exp(log_gamma[:, None, None] * np.maximum(diff, f32(0))), f32(0)).astype(f32)
    decay = np.stack([np.kron(np.eye(RET_GROUP, dtype=f32), decay[h]) for h in range(RET_HEADS)])
    k_w = np.exp(log_gamma[None, :] * (c - 1.0 - idx)[:, None]).astype(f32)
    q_w = np.exp(log_gamma[None, :] * (idx + 1.0)[:, None]).astype(f32)
    chunk_decay = np.exp(log_gamma * f32(c)).astype(f32)
    reps = ROW_TILE // c
    kw = np.tile(np.repeat(k_w, HEAD_DIM, axis=1), (reps, 1))
    qw = np.tile(np.repeat(q_w, HEAD_DIM, axis=1), (reps, 1))
    ones = np.ones((HEAD_DIM, HEAD_DIM), f32)
    cd = np.stack([
        np.kron(np.diag(chunk_decay[2 * p : 2 * p + 2]), ones) for p in range(RET_HEADS // 2)
    ])
    avg = np.kron(np.eye(RET_HEADS, dtype=f32), np.full((HEAD_DIM, HEAD_DIM), 1.0 / HEAD_DIM, f32))
    return jnp.asarray(decay), jnp.asarray(kw), jnp.asarray(qw), jnp.asarray(cd), jnp.asarray(avg, dtype=BF16)


def kernel(x, mem, positions, pre_mix_g, post_mix_g, w_in, conv_w, conv_b, w_rg, b_rg, w_ig, b_ig, lru_lambda, fox_b_f, w_out, pre_cross_g, post_cross_g, mem_norm_g, w_cq, w_ck, w_cv, w_co, pre_ffn_g, post_ffn_g, w_gate, w_up, w_down):
    bsz, seq, d = x.shape
    depth = w_in.shape[0]
    m = bsz * seq
    mem_len = mem.shape[1]
    xf = x.reshape(m, d)
    row = lambda v: v.reshape(1, -1)

    half = HEAD_DIM // 2
    inv_freq = ROPE_THETA ** (-jnp.arange(half, dtype=F32) / half)
    freq = jnp.tile(inv_freq, LANES // half).reshape(1, LANES)
    pos = jnp.broadcast_to(positions.reshape(m, 1), (m, LANES))
    cos, sin = _rope_tables(pos, freq)
    decay, kw, qw, cd, avg = _retention_tables()

    o_fox = 2 * LRU_WIDTH
    o_ff = o_fox + 3 * FOX_WIDTH
    o_ret = o_ff + FOX_HEADS

    for l in range(depth):
        order = jnp.argsort(fox_b_f[l])
        by_head = lambda w: w.reshape(d, FOX_HEADS, HEAD_DIM)[:, order].reshape(d, FOX_WIDTH)
        wl = w_in[l, :, :o_fox].astype(BF16)
        wf = jnp.concatenate(
            [by_head(w_in[l, :, o_fox + i * FOX_WIDTH : o_fox + (i + 1) * FOX_WIDTH]) for i in range(3)], axis=1
        ).astype(BF16)
        wff = jnp.pad(w_in[l, :, o_ff:o_ret][:, order], ((0, 0), (0, LANES - FOX_HEADS))).astype(BF16)
        wr = w_in[l, :, o_ret:].astype(BF16)
        bf = jnp.pad(fox_b_f[l][order], (0, LANES - FOX_HEADS)).reshape(1, LANES)
        fox_rows = w_out[l, LRU_WIDTH : LRU_WIDTH + FOX_WIDTH].reshape(FOX_HEADS, HEAD_DIM, d)[order].reshape(FOX_WIDTH, d)
        wm = jnp.concatenate([w_out[l, :LRU_WIDTH], fox_rows, w_out[l, LRU_WIDTH + FOX_WIDTH :]], axis=0).astype(BF16)
        lru, ret, *fox_operands = _inproj(xf, row(pre_mix_g[l]), wl, wf, wr, wff, bf, bsz)
        fox_o = _fox_attention(*fox_operands)

        wg = jnp.concatenate([_block_diag(w_rg[l]), _block_diag(w_ig[l])], axis=1).astype(BF16)
        bg = jnp.concatenate([b_rg[l], b_ig[l]]).reshape(1, -1)
        lru_o = _rg_lru(lru, conv_w[l], row(conv_b[l]), wg, bg, row(lru_lambda[l]), bsz)

        ret_o = _retention(ret, cos, sin, decay, kw, qw, cd, avg, bsz)

        wkv = jnp.concatenate([w_ck[l], w_cv[l]], axis=1).astype(BF16)
        kv = _norm_matmul(mem.reshape(bsz * mem_len, d), row(mem_norm_g), wkv, mem_len)
        kt = kv[:, :d].reshape(bsz, mem_len, d).transpose(0, 2, 1)
        vv = kv[:, d:].reshape(bsz, mem_len, d)
        xf = _layer_tail(
            xf, lru_o, fox_o, ret_o, wm, row(post_mix_g[l]),
            row(pre_cross_g[l]), w_cq[l].astype(BF16), kt, vv, w_co[l].astype(BF16), row(post_cross_g[l]),
            row(pre_ffn_g[l]), w_gate[l].astype(BF16), w_up[l].astype(BF16), w_down[l].astype(BF16), row(post_ffn_g[l]), bsz)
    return xf.reshape(bsz, seq, d)
```

```python
import jax
import jax.numpy as jnp
import numpy as np
from jax import lax
from jax.experimental import pallas as pl
from jax.experimental.pallas import tpu as pltpu

F32 = jnp.float32
BF16 = jnp.bfloat16

HEAD_DIM = 64
LRU_WIDTH = 256
LRU_BLOCKS = 4
CONV_WIDTH = 4
LRU_C = 8.0
FOX_HEADS = 8
FOX_WIDTH = FOX_HEADS * HEAD_DIM
RET_HEADS = 4
RET_WIDTH = RET_HEADS * HEAD_DIM
CROSS_HEADS = 4
RET_CHUNK = 128
RET_GROUP = 2
ROPE_THETA = 10000.0
EPS = 1e-6

LANES = 128
SUBLANES = 8
VMEM_LIMIT = 48 * 1024 * 1024
TAIL_VMEM_LIMIT = 56 * 1024 * 1024

ROW_TILE = 512
FOX_TQ = 1024
FOX_TK = 512
SCAN_TILE = 256
FFN_CHUNK = 1024
TAIL_SPLIT = 2
NEG_BIG = -1e30
LOG2E = 1.4426950408889634
SKIP_LOG2 = 152.0
SKIP_REL = 2.0**-8
SKIP_REL_C = 2.0**-18
NORM_SAFETY = 1.0 + 2.0**-7


def _params(sem):
    return pltpu.CompilerParams(dimension_semantics=sem, vmem_limit_bytes=VMEM_LIMIT)


def _rms(x, g):
    ms = jnp.mean(x * x, axis=-1, keepdims=True)
    return x * lax.rsqrt(ms + EPS) * g


def _dot(a, b):
    return jnp.dot(a, b, preferred_element_type=F32)


def _dot_nt(a, b):
    return lax.dot_general(a, b, (((1,), (1,)), ((), ())), preferred_element_type=F32)


def _dot_tn(a, b):
    return lax.dot_general(a, b, (((0,), (0,)), ((), ())), preferred_element_type=F32)


def _full(shape):
    nd = len(shape)
    return pl.BlockSpec(shape, lambda *_: (0,) * nd)


def _scan_rows(a, b, carry):
    n = b.shape[0]
    sub = lax.broadcasted_iota(jnp.int32, b.shape, 0) % SUBLANES
    k = 1
    while k < SUBLANES:
        keep = sub >= k
        b_prev = jnp.where(keep, pltpu.roll(b, k, 0), 0.0)
        if a is None:
            b = b + b_prev
        else:
            a_prev = jnp.where(keep, pltpu.roll(a, k, 0), 1.0)
            b = a * b_prev + b
            a = a * a_prev
        k *= 2
    groups = []
    for g in range(n // SUBLANES):
        rows = slice(g * SUBLANES, (g + 1) * SUBLANES)
        h = b[rows] + carry if a is None else b[rows] + a[rows] * carry
        carry = h[SUBLANES - 1 :, :]
        groups.append(h)
    return jnp.concatenate(groups, axis=0)


def _split3(z):
    hi = z.astype(BF16).astype(F32)
    mid = (z - hi).astype(BF16).astype(F32)
    lo = (z - hi - mid).astype(BF16).astype(F32)
    return hi, mid, lo


def _inproj_kernel(x_ref, g_ref, wl_ref, wf_ref, wr_ref, wff_ref, bf_ref, place_ref,
                   lru_ref, ret_ref, q_ref, k_ref, vt_ref, stats_ref, carry_ref):
    ts = x_ref.shape[0]

    @pl.when(pl.program_id(1) == 0)
    def _():
        carry_ref[...] = jnp.zeros_like(carry_ref)

    h_in = _rms(x_ref[...], g_ref[...]).astype(BF16)
    log_f = jax.nn.log_sigmoid(_dot(h_in, wff_ref[...]) + bf_ref[...])
    c_nat = _scan_rows(None, log_f, carry_ref[...])
    carry_ref[...] = c_nat[-1:, :]
    q_all = (_dot(h_in, wf_ref[:, :FOX_WIDTH]) * (HEAD_DIM**-0.5 * LOG2E)).astype(BF16)
    k_all = _dot(h_in, wf_ref[:, FOX_WIDTH : 2 * FOX_WIDTH]).astype(BF16)
    v_all = _dot(h_in, wf_ref[:, 2 * FOX_WIDTH :]).astype(BF16)
    ccol = c_nat * LOG2E
    terms = jnp.concatenate(_split3(ccol), axis=1).astype(BF16)
    placed = _dot(terms, place_ref[...])

    lane = lax.broadcasted_iota(jnp.int32, (ts, LANES), 1)
    stat_lane = lax.broadcasted_iota(jnp.int32, (1, LANES), 1)
    same_head = (lax.broadcasted_iota(jnp.int32, (LANES, LANES), 0) < HEAD_DIM) == (
        lax.broadcasted_iota(jnp.int32, (LANES, LANES), 1) < HEAD_DIM)
    head_ones = jnp.where(same_head, 1.0, 0.0).astype(BF16)
    stats = []
    for pair in range(FOX_HEADS // 2):
        cols = slice(pair * LANES, (pair + 1) * LANES)
        gq, gk, gv = q_all[:, cols], k_all[:, cols], v_all[:, cols]
        norms = []
        for g in (gq, gk):
            g32 = g.astype(F32)
            sq = _dot((g32 * g32).astype(BF16), head_ones)
            norms.append(jnp.sqrt(jnp.max(sq, axis=0, keepdims=True)) * NORM_SAFETY)
        for e in range(2):
            h = 2 * pair + e
            data = (lane < HEAD_DIM) if e == 0 else (lane >= HEAD_DIM)
            el = lane - (HEAD_DIM if e == 0 else 0)
            extras = placed[:, h * LANES : (h + 1) * LANES]
            q_extra = jnp.where((el >= 3) & (el < 6), 1.0, extras)
            k_extra = jnp.where((el >= 0) & (el < 3), 1.0, extras)
            v_extra = jnp.where(el == 0, 1.0, 0.0)
            q_ref[0, h] = jnp.where(data, gq, q_extra.astype(BF16))
            k_ref[0, h] = jnp.where(data, gk, k_extra.astype(BF16))
            vt_ref[0, h] = jnp.where(data, gv, v_extra.astype(BF16)).T
            first = e * HEAD_DIM
            q_norm = norms[0][:, first : first + 1]
            k_norm = norms[1][:, first : first + 1]
            c_first = ccol[0:1, h : h + 1]
            c_last = ccol[ts - 1 : ts, h : h + 1]
            stats.append(jnp.where(stat_lane == 0, q_norm, jnp.where(stat_lane == 1, k_norm, jnp.where(stat_lane == 2, c_first, c_last))))
    stats_ref[0, 0] = jnp.concatenate(stats, axis=0)
    lru_ref[...] = _dot(h_in, wl_ref[...])
    ret_ref[...] = _dot(h_in, wr_ref[...])


def _fox_placement():
    place = np.zeros((3 * LANES, FOX_HEADS * LANES), np.float32)
    for h in range(FOX_HEADS):
        first = h * LANES + (HEAD_DIM if h % 2 == 0 else 0)
        for t in range(3):
            place[t * LANES + h, first + t] = 1.0
            place[t * LANES + h, first + 3 + t] = -1.0
    return jnp.asarray(place, dtype=BF16)


def _inproj(x, g, wl, wf, wr, wff, bf, bsz):
    m, d = x.shape
    s = m // bsz
    ts = FOX_TK
    nt = s // ts
    place = _fox_placement()
    row = lambda n: pl.BlockSpec((ts, n), lambda b, i: (b * nt + i, 0))
    per_head = pl.BlockSpec((1, FOX_HEADS, ts, LANES), lambda b, i: (b, 0, i, 0))
    return pl.pallas_call(
        _inproj_kernel,
        grid=(bsz, nt),
        in_specs=[row(d), _full(g.shape), _full(wl.shape), _full(wf.shape), _full(wr.shape), _full(wff.shape),
                  _full(bf.shape), _full(place.shape)],
        out_specs=[
            row(wl.shape[1]), row(wr.shape[1]), per_head, per_head,
            pl.BlockSpec((1, FOX_HEADS, LANES, ts), lambda b, i: (b, 0, 0, i)),
            pl.BlockSpec((1, 1, FOX_HEADS, LANES), lambda b, i: (b, i, 0, 0)),
        ],
        out_shape=[
            jax.ShapeDtypeStruct((m, wl.shape[1]), F32),
            jax.ShapeDtypeStruct((m, wr.shape[1]), F32),
            jax.ShapeDtypeStruct((bsz, FOX_HEADS, s, LANES), BF16),
            jax.ShapeDtypeStruct((bsz, FOX_HEADS, s, LANES), BF16),
            jax.ShapeDtypeStruct((bsz, FOX_HEADS, LANES, s), BF16),
            jax.ShapeDtypeStruct((bsz, nt, FOX_HEADS, LANES), F32),
        ],
        scratch_shapes=[pltpu.VMEM((1, LANES), F32)],
        compiler_params=_params(("parallel", "arbitrary")),
        name="mixer_inproj",
    )(x, g, wl, wf, wr, wff, bf, place)


def _fox_kernel(qn_ref, kn_ref, cf_ref, cl_ref, q_ref, k_ref, vt_ref, o_ref, m_scr, acc_scr, s_scr, mx_scr):
    tq = q_ref.shape[2]
    tk = FOX_TK
    sub = tq // tk
    nk = k_ref.shape[2] // tk
    nq = nk // sub
    qi = pl.program_id(2)

    def first_needed(h):
        head = (pl.program_id(0) * (FOX_HEADS // 2) + pl.program_id(1)) * 2 + h
        q_norm = qn_ref[head * nq + qi]
        c_first = cf_ref[head * nq + qi]
        floor = -q_norm * kn_ref[head * nk + qi * sub + sub - 1]

        def needed(kb):
            j = head * nk + jnp.maximum(kb, 0)
            dot_bound = q_norm * kn_ref[j]
            slack = SKIP_REL * (dot_bound - floor) + SKIP_REL_C * (jnp.abs(c_first) + jnp.abs(cl_ref[j]))
            return dot_bound + c_first - cl_ref[j] + slack >= floor - SKIP_LOG2

        kb = lax.while_loop(lambda kb: jnp.logical_and(kb >= 0, needed(kb)), lambda kb: kb - 1, qi * sub - 1)
        return kb + 1

    def logits(h, kb, slot, diag):
        start = pl.multiple_of(kb * tk, tk)
        q0 = 0 if diag is None else diag * tk
        s = _dot_nt(k_ref[0, h, pl.ds(start, tk), :], q_ref[0, h, q0:, :])
        if diag is not None:
            key = lax.broadcasted_iota(jnp.int32, s.shape, 0)
            qry = lax.broadcasted_iota(jnp.int32, s.shape, 1)
            s = jnp.where(key <= qry, s, NEG_BIG)
        s_scr[slot, :, q0:] = s
        mx_scr[slot, :, q0:] = jnp.max(s, axis=0, keepdims=True)

    def accumulate(h, kb, slot, diag=None):
        start = pl.multiple_of(kb * tk, tk)
        q0 = 0 if diag is None else diag * tk
        m_old = m_scr[h, :, q0:]
        m_new = jnp.maximum(m_old, mx_scr[slot, :, q0:])
        m_scr[h, :, q0:] = m_new
        p = jnp.exp2(s_scr[slot, :, q0:] - m_new).astype(BF16)
        acc_scr[h, :, q0:] = jnp.exp2(m_old - m_new) * acc_scr[h, :, q0:] + _dot(vt_ref[0, h, :, pl.ds(start, tk)], p)

    m_scr[...] = jnp.full_like(m_scr, NEG_BIG)
    acc_scr[...] = jnp.zeros_like(acc_scr)
    own = qi * sub
    lo = jnp.minimum(first_needed(0), first_needed(1))
    logits(0, own, 0, 0)
    logits(1, own, 1, 0)
    accumulate(0, own, 0, 0)
    for u in range(1, sub):
        logits(0, own + u, 0, u)
        accumulate(1, own + u - 1, 1, u - 1)
        logits(1, own + u, 1, u)
        accumulate(0, own + u, 0, u)
    last_own = own + sub - 1

    @pl.when(lo >= own)
    def _():
        accumulate(1, last_own, 1, sub - 1)

    @pl.when(lo < own)
    def _():
        logits(0, lo, 0, None)
        accumulate(1, last_own, 1, sub - 1)

        def body(kb, carry):
            logits(1, kb, 1, None)
            accumulate(0, kb, 0)
            logits(0, kb + 1, 0, None)
            accumulate(1, kb, 1)
            return carry

        lax.fori_loop(lo, own - 1, body, 0)
        logits(1, own - 1, 1, None)
        accumulate(0, own - 1, 0)
        accumulate(1, own - 1, 1)

    even, odd = acc_scr[0], acc_scr[1]
    row = lax.broadcasted_iota(jnp.int32, even.shape, 0)
    o = jnp.where(row < HEAD_DIM, even / even[HEAD_DIM : HEAD_DIM + 1, :], odd / odd[0:1, :])
    o_ref[...] = o.T.astype(o_ref.dtype)


def _fox_attention(q, k, vt, stats):
    bsz, nh, s, _ = q.shape
    tq = FOX_TQ
    sub = tq // FOX_TK
    nq = s // tq
    npair = nh // 2
    st = stats[..., :4].transpose(0, 2, 1, 3)
    per_q = lambda v: v.reshape(bsz, nh, nq, sub)
    scalars = (
        per_q(st[..., 0]).max(axis=-1),
        lax.cummax(st[..., 1], axis=2),
        per_q(st[..., 2])[..., 0],
        st[..., 3],
    )
    scalars = tuple(v.reshape(-1) for v in scalars)
    return pl.pallas_call(
        _fox_kernel,
        grid_spec=pltpu.PrefetchScalarGridSpec(
            num_scalar_prefetch=len(scalars),
            grid=(bsz, npair, nq),
            in_specs=[
                pl.BlockSpec((1, 2, tq, LANES), lambda b, p, i, *_: (b, p, i, 0)),
                pl.BlockSpec((1, 2, s, LANES), lambda b, p, i, *_: (b, p, 0, 0)),
                pl.BlockSpec((1, 2, LANES, s), lambda b, p, i, *_: (b, p, 0, 0)),
            ],
            out_specs=pl.BlockSpec((tq, LANES), lambda b, p, i, *_: (b * nq + i, p)),
            scratch_shapes=[
                pltpu.VMEM((2, 1, tq), F32),
                pltpu.VMEM((2, LANES, tq), F32),
                pltpu.VMEM((2, FOX_TK, tq), F32),
                pltpu.VMEM((2, 1, tq), F32),
            ],
        ),
        out_shape=jax.ShapeDtypeStruct((bsz * s, nh * HEAD_DIM), BF16),
        compiler_params=_params(("parallel", "parallel", "arbitrary")),
        name="fox_attention",
    )(*scalars, q, k, vt)


def _lru_kernel(xy_ref, cw_ref, cb_ref, wg_ref, bg_ref, lam_ref, o_ref, tail_scr, h_scr):
    w = LRU_WIDTH

    @pl.when(pl.program_id(1) == 0)
    def _():
        tail_scr[...] = jnp.zeros_like(tail_scr)
        h_scr[...] = jnp.zeros_like(h_scr)

    x = xy_ref[:, :w]
    y = xy_ref[:, w:]
    tail = tail_scr[...]
    row8 = lax.broadcasted_iota(jnp.int32, (SUBLANES, w), 0)
    conv = x * cw_ref[CONV_WIDTH - 1 : CONV_WIDTH, :] + cb_ref[...]
    for j in range(1, CONV_WIDTH):
        xs = pltpu.roll(x, j, 0)
        head = jnp.where(row8 < j, pltpu.roll(tail, j, 0), xs[:SUBLANES])
        xs = jnp.concatenate([head, xs[SUBLANES:]], axis=0)
        conv = conv + xs * cw_ref[CONV_WIDTH - 1 - j : CONV_WIDTH - j, :]
    tail_scr[...] = x[-SUBLANES:]

    gates = jax.nn.sigmoid(_dot(conv.astype(BF16), wg_ref[...]) + bg_ref[...])
    r = gates[:, :w]
    i = gates[:, w:]
    log_a = -LRU_C * r * jax.nn.softplus(-lam_ref[...])
    a = jnp.exp(log_a)
    th = jnp.tanh(log_a)
    u = jnp.sqrt(-2.0 * th / (1.0 - th)) * (i * conv)
    h = _scan_rows(a, u, h_scr[...])
    h_scr[...] = h[-1:, :]
    o_ref[...] = (h * jax.nn.gelu(y)).astype(o_ref.dtype)


def _rg_lru(xy, cw, cb, wg, bg, lam, bsz):
    m = xy.shape[0]
    ts = SCAN_TILE
    nt = m // bsz // ts
    w = LRU_WIDTH
    return pl.pallas_call(
        _lru_kernel,
        grid=(bsz, nt),
        in_specs=[
            pl.BlockSpec((ts, 2 * w), lambda b, i: (b * nt + i, 0)),
            _full(cw.shape), _full(cb.shape), _full(wg.shape), _full(bg.shape), _full(lam.shape),
        ],
        out_specs=pl.BlockSpec((ts, w), lambda b, i: (b * nt + i, 0)),
        out_shape=jax.ShapeDtypeStruct((m, w), BF16),
        scratch_shapes=[pltpu.VMEM((SUBLANES, w), F32), pltpu.VMEM((1, w), F32)],
        compiler_params=_params(("parallel", "arbitrary")),
        name="rg_lru",
    )(xy, cw, cb, wg, bg, lam)


def _rope_kernel(pos_ref, freq_ref, cos_ref, sin_ref):
    ang = pos_ref[...].astype(F32) * freq_ref[...]
    lane = lax.broadcasted_iota(jnp.int32, ang.shape, 1)
    first_half = (lane % HEAD_DIM) < HEAD_DIM // 2
    cos_ref[...] = jnp.cos(ang)
    s = jnp.sin(ang)
    sin_ref[...] = jnp.where(first_half, -s, s)


def _rope_tables(pos, freq):
    m = pos.shape[0]
    tm = ROW_TILE
    spec = pl.BlockSpec((tm, LANES), lambda i: (i, 0))
    return pl.pallas_call(
        _rope_kernel,
        grid=(m // tm,),
        in_specs=[spec, _full(freq.shape)],
        out_specs=[spec, spec],
        out_shape=[jax.ShapeDtypeStruct((m, LANES), F32)] * 2,
        compiler_params=_params(("parallel",)),
        name="rope_tables",
    )(pos, freq)


def _group_mean(z, avg):
    hi, mid, lo = _split3(z)
    return _dot(hi.astype(BF16), avg) + _dot(mid.astype(BF16), avg) + _dot(lo.astype(BF16), avg)


def _ret_kernel(x_ref, cos_ref, sin_ref, decay_ref, kw_ref, qw_ref, cd_ref, avg_ref, o_ref, state_scr):
    c = RET_CHUNK
    w = RET_WIDTH
    npair = RET_HEADS // 2

    @pl.when(pl.program_id(1) == 0)
    def _():
        state_scr[...] = jnp.zeros_like(state_scr)

    cos = cos_ref[...]
    sin = sin_ref[...]
    lane = lax.broadcasted_iota(jnp.int32, cos.shape, 1)
    first_half = (lane % HEAD_DIM) < HEAD_DIM // 2

    def rotary(z):
        swapped = jnp.where(first_half, pltpu.roll(z, LANES - HEAD_DIM // 2, 1), pltpu.roll(z, HEAD_DIM // 2, 1))
        return z * cos + swapped * sin

    ts = x_ref.shape[0]
    nchunk = ts // c
    row_chunk = lax.broadcasted_iota(jnp.int32, (ts, LANES), 0) // c
    ys = []
    for p in range(npair):
        cols = slice(p * LANES, (p + 1) * LANES)
        q = rotary(x_ref[:, cols])
        k = rotary(x_ref[:, w + p * LANES : w + (p + 1) * LANES]) * HEAD_DIM**-0.5
        v = x_ref[:, 2 * w + p * LANES : 2 * w + (p + 1) * LANES].astype(BF16)
        kb = k.astype(BF16)
        q_cross = (q * qw_ref[:, cols]).astype(BF16)
        k_state = (k * kw_ref[:, cols]).astype(BF16)
        zero = jnp.zeros_like(v)
        v_blocks = jnp.concatenate([jnp.where(row_chunk == n, v, zero) for n in range(nchunk)], axis=1)
        incs = _dot_tn(k_state, v_blocks)
        cd = cd_ref[p]
        states = [state_scr[p]]
        for n in range(nchunk):
            states.append(cd * states[-1] + jnp.where(cd > 0.0, incs[:, n * LANES : (n + 1) * LANES], 0.0))
        state_scr[p] = states[-1]
        q_blocks = jnp.concatenate([jnp.where(row_chunk == n, q_cross, zero) for n in range(nchunk)], axis=1)
        y = _dot(q_blocks, jnp.concatenate([st.astype(BF16) for st in states[:nchunk]], axis=0))
        span = RET_GROUP * c
        span_low = lax.broadcasted_iota(jnp.int32, (span, LANES), 1) < HEAD_DIM
        span_head = (span_low, jnp.logical_not(span_low))
        inner = []
        for r0 in range(0, ts, span):
            r = slice(r0, r0 + span)
            per_head = []
            for h in range(2):
                q_head = jnp.where(span_head[h], q[r], 0.0).astype(BF16)
                scores = _dot_nt(q_head, kb[r]) * decay_ref[2 * p + h]
                per_head.append(_dot(scores.astype(BF16), v[r]))
            inner.append(jnp.where(span_low, per_head[0], per_head[1]))
        ys.append(y + jnp.concatenate(inner, axis=0))
    y = jnp.concatenate(ys, axis=1)
    avg = avg_ref[...]
    mu = _group_mean(y, avg)
    d = y - mu
    var = _group_mean(d * d, avg)
    yn = d * lax.rsqrt(var + EPS)
    g = x_ref[:, 3 * w :]
    o_ref[...] = (g * jax.nn.sigmoid(g) * yn).astype(o_ref.dtype)


def _retention(ret, cos, sin, decay, kw, qw, cd, avg, bsz):
    m = ret.shape[0]
    ts = ROW_TILE
    nt = m // bsz // ts
    w = RET_WIDTH
    return pl.pallas_call(
        _ret_kernel,
        grid=(bsz, nt),
        in_specs=[
            pl.BlockSpec((ts, 4 * w), lambda b, i: (b * nt + i, 0)),
            pl.BlockSpec((ts, LANES), lambda b, i: (b * nt + i, 0)),
            pl.BlockSpec((ts, LANES), lambda b, i: (b * nt + i, 0)),
            _full(decay.shape), _full(kw.shape), _full(qw.shape), _full(cd.shape), _full(avg.shape),
        ],
        out_specs=pl.BlockSpec((ts, w), lambda b, i: (b * nt + i, 0)),
        out_shape=jax.ShapeDtypeStruct((m, w), BF16),
        scratch_shapes=[pltpu.VMEM((RET_HEADS // 2, LANES, LANES), F32)],
        compiler_params=_params(("parallel", "arbitrary")),
        name="retention",
    )(ret, cos, sin, decay, kw, qw, cd, avg)


def _norm_matmul_kernel(x_ref, g_ref, w_ref, o_ref):
    o_ref[...] = _dot(_rms(x_ref[...], g_ref[...]).astype(BF16), w_ref[...]).astype(o_ref.dtype)


def _norm_matmul(x, g, w, tm):
    m, d = x.shape
    n = w.shape[1]
    return pl.pallas_call(
        _norm_matmul_kernel,
        grid=(m // tm,),
        in_specs=[pl.BlockSpec((tm, d), lambda i: (i, 0)), _full(g.shape), _full(w.shape)],
        out_specs=pl.BlockSpec((tm, n), lambda i: (i, 0)),
        out_shape=jax.ShapeDtypeStruct((m, n), BF16),
        compiler_params=_params(("parallel",)),
        name="memory_kv_proj",
    )(x, g, w)


def _tail_kernel(x_ref, lru_ref, fox_ref, ret_ref, wm_ref, gm_ref, g1_ref, wq_ref, kt_ref, v_ref, wo_ref, g2_ref,
                 g3_ref, wg_ref, wu_ref, wd_ref, g4_ref, o_ref):
    d = x_ref.shape[1]
    hd = d // CROSS_HEADS
    a, b = LRU_WIDTH, LRU_WIDTH + FOX_WIDTH
    dff = wg_ref.shape[1]
    bounds = [(lo, min(lo + FFN_CHUNK, dff)) for lo in range(0, dff, FFN_CHUNK)]
    head_cols = [slice(h * hd, (h + 1) * hd) for h in range(CROSS_HEADS)]

    def phases(rows):
        mix = _dot(lru_ref[rows, :], wm_ref[:a, :]) + _dot(fox_ref[rows, :], wm_ref[a:b, :]) + _dot(ret_ref[rows, :], wm_ref[b:, :])
        yield
        x = x_ref[rows, :] + _rms(mix, gm_ref[...])
        h = _rms(x, g1_ref[...]).astype(BF16)
        yield
        q = _dot(h, wq_ref[...]).astype(BF16)
        yield
        logits = [_dot(q[:, cols], kt_ref[0, cols, :]) * hd**-0.5 for cols in head_cols]
        yield
        outs = []
        for cols, s in zip(head_cols, logits):
            e = jnp.exp(s - jnp.max(s, axis=-1, keepdims=True))
            p = e / jnp.sum(e, axis=-1, keepdims=True)
            outs.append(_dot(p.astype(BF16), v_ref[0, :, cols]).astype(BF16))
        yield
        attn = _dot(jnp.concatenate(outs, axis=1), wo_ref[...])
        yield
        x = x + _rms(attn, g2_ref[...])
        h = _rms(x, g3_ref[...]).astype(BF16)
        yield
        y = None
        for lo, hi in bounds:
            gate = _dot(h, wg_ref[:, lo:hi])
            up = _dot(h, wu_ref[:, lo:hi])
            yield
            act = (gate * jax.nn.sigmoid(gate) * up).astype(BF16)
            part = _dot(act, wd_ref[lo:hi, :])
            y = part if y is None else y + part
            yield
        o_ref[rows, :] = x + _rms(y, g4_ref[...])

    tm = x_ref.shape[0]
    step = tm // TAIL_SPLIT
    gens = [phases(slice(g * step, (g + 1) * step)) for g in range(TAIL_SPLIT)]
    live = list(gens)
    started = 0
    while live:
        if started < len(gens):
            started += 1
        for gen in list(gens[:started]):
            if gen in live:
                try:
                    next(gen)
                except StopIteration:
                    live.remove(gen)


def _layer_tail(x, lru, fox, ret, wm, gm, g1, wq, kt, v, wo, g2, g3, wg, wu, wd, g4, bsz):
    m, d = x.shape
    tm = ROW_TILE
    nt = m // bsz // tm
    mem_len = v.shape[1]
    row = lambda n: pl.BlockSpec((tm, n), lambda i: (i, 0))
    resident = lambda arr: pl.BlockSpec(arr.shape, lambda i: (0,) * arr.ndim, pipeline_mode=pl.Buffered(1))
    return pl.pallas_call(
        _tail_kernel,
        grid=(m // tm,),
        in_specs=[
            row(d), row(lru.shape[1]), row(fox.shape[1]), row(ret.shape[1]),
            resident(wm), resident(gm), resident(g1), resident(wq),
            pl.BlockSpec((1, d, mem_len), lambda i: (i // nt, 0, 0)),
            pl.BlockSpec((1, mem_len, d), lambda i: (i // nt, 0, 0)),
            resident(wo), resident(g2), resident(g3), resident(wg), resident(wu), resident(wd), resident(g4),
        ],
        out_specs=row(d),
        out_shape=jax.ShapeDtypeStruct((m, d), F32),
        compiler_params=pltpu.CompilerParams(dimension_semantics=("parallel",), vmem_limit_bytes=TAIL_VMEM_LIMIT),
        name="layer_tail",
    )(x, lru, fox, ret, wm, gm, g1, wq, kt, v, wo, g2, g3, wg, wu, wd, g4)


def _block_diag(w):
    n, a, b = w.shape
    eye = jnp.eye(n, dtype=w.dtype)
    return (eye[:, None, :, None] * w[:, :, None, :]).reshape(n * a, n * b)


def _retention_tables():
    c = RET_CHUNK
    f32 = np.float32
    log_gamma = np.log1p(-np.exp2(-5.0 - np.arange(RET_HEADS, dtype=f32))).astype(f32)
    idx = np.arange(c, dtype=f32)
    diff = idx[:, None] - idx[None, :]
    decay = np.where(diff >= 0, np.exp(log_gamma[:, None, None] * np.maximum(diff, f32(0))), f32(0)).astype(f32)
    decay = np.stack([np.kron(np.eye(RET_GROUP, dtype=f32), decay[h]) for h in range(RET_HEADS)])
    k_w = np.exp(log_gamma[None, :] * (c - 1.0 - idx)[:, None]).astype(f32)
    q_w = np.exp(log_gamma[None, :] * (idx + 1.0)[:, None]).astype(f32)
    chunk_decay = np.exp(log_gamma * f32(c)).astype(f32)
    reps = ROW_TILE // c
    kw = np.tile(np.repeat(k_w, HEAD_DIM, axis=1), (reps, 1))
    qw = np.tile(np.repeat(q_w, HEAD_DIM, axis=1), (reps, 1))
    ones = np.ones((HEAD_DIM, HEAD_DIM), f32)
    cd = np.stack([
        np.kron(np.diag(chunk_decay[2 * p : 2 * p + 2]), ones) for p in range(RET_HEADS // 2)
    ])
    avg = np.kron(np.eye(RET_HEADS, dtype=f32), np.full((HEAD_DIM, HEAD_DIM), 1.0 / HEAD_DIM, f32))
    return jnp.asarray(decay), jnp.asarray(kw), jnp.asarray(qw), jnp.asarray(cd), jnp.asarray(avg, dtype=BF16)


def kernel(x, mem, positions, pre_mix_g, post_mix_g, w_in, conv_w, conv_b, w_rg, b_rg, w_ig, b_ig, lru_lambda, fox_b_f, w_out, pre_cross_g, post_cross_g, mem_norm_g, w_cq, w_ck, w_cv, w_co, pre_ffn_g, post_ffn_g, w_gate, w_up, w_down):
    bsz, seq, d = x.shape
    depth = w_in.shape[0]
    m = bsz * seq
    mem_len = mem.shape[1]
    xf = x.reshape(m, d)
    row = lambda v: v.reshape(1, -1)

    half = HEAD_DIM // 2
    inv_freq = ROPE_THETA ** (-jnp.arange(half, dtype=F32) / half)
    freq = jnp.tile(inv_freq, LANES // half).reshape(1, LANES)
    pos = jnp.broadcast_to(positions.reshape(m, 1), (m, LANES))
    cos, sin = _rope_tables(pos, freq)
    decay, kw, qw, cd, avg = _retention_tables()

    o_fox = 2 * LRU_WIDTH
    o_ff = o_fox + 3 * FOX_WIDTH
    o_ret = o_ff + FOX_HEADS

    for l in range(depth):
        order = jnp.argsort(fox_b_f[l])
        by_head = lambda w: w.reshape(d, FOX_HEADS, HEAD_DIM)[:, order].reshape(d, FOX_WIDTH)
        wl = w_in[l, :, :o_fox].astype(BF16)
        wf = jnp.concatenate(
            [by_head(w_in[l, :, o_fox + i * FOX_WIDTH : o_fox + (i + 1) * FOX_WIDTH]) for i in range(3)], axis=1
        ).astype(BF16)
        wff = jnp.pad(w_in[l, :, o_ff:o_ret][:, order], ((0, 0), (0, LANES - FOX_HEADS))).astype(BF16)
        wr = w_in[l, :, o_ret:].astype(BF16)
        bf = jnp.pad(fox_b_f[l][order], (0, LANES - FOX_HEADS)).reshape(1, LANES)
        fox_rows = w_out[l, LRU_WIDTH : LRU_WIDTH + FOX_WIDTH].reshape(FOX_HEADS, HEAD_DIM, d)[order].reshape(FOX_WIDTH, d)
        wm = jnp.concatenate([w_out[l, :LRU_WIDTH], fox_rows, w_out[l, LRU_WIDTH + FOX_WIDTH :]], axis=0).astype(BF16)
        lru, ret, *fox_operands = _inproj(xf, row(pre_mix_g[l]), wl, wf, wr, wff, bf, bsz)
        fox_o = _fox_attention(*fox_operands)

        wg = jnp.concatenate([_block_diag(w_rg[l]), _block_diag(w_ig[l])], axis=1).astype(BF16)
        bg = jnp.concatenate([b_rg[l], b_ig[l]]).reshape(1, -1)
        lru_o = _rg_lru(lru, conv_w[l], row(conv_b[l]), wg, bg, row(lru_lambda[l]), bsz)

        ret_o = _retention(ret, cos, sin, decay, kw, qw, cd, avg, bsz)

        wkv = jnp.concatenate([w_ck[l], w_cv[l]], axis=1).astype(BF16)
        kv = _norm_matmul(mem.reshape(bsz * mem_len, d), row(mem_norm_g), wkv, mem_len)
        kt = kv[:, :d].reshape(bsz, mem_len, d).transpose(0, 2, 1)
        vv = kv[:, d:].reshape(bsz, mem_len, d)
        xf = _layer_tail(
            xf, lru_o, fox_o, ret_o, wm, row(post_mix_g[l]),
            row(pre_cross_g[l]), w_cq[l].astype(BF16), kt, vv, w_co[l].astype(BF16), row(post_cross_g[l]),
            row(pre_ffn_g[l]), w_gate[l].astype(BF16), w_up[l].astype(BF16), w_down[l].astype(BF16), row(post_ffn_g[l]), bsz)
    return xf.reshape(bsz, seq, d)
```

```python
import jax
import jax.numpy as jnp
import numpy as np
from jax import lax
from jax.experimental import pallas as pl
from jax.experimental.pallas import tpu as pltpu

F32 = jnp.float32
BF16 = jnp.bfloat16

HEAD_DIM = 64
LRU_WIDTH = 256
LRU_BLOCKS = 4
CONV_WIDTH = 4
LRU_C = 8.0
FOX_HEADS = 8
FOX_WIDTH = FOX_HEADS * HEAD_DIM
RET_HEADS = 4
RET_WIDTH = RET_HEADS * HEAD_DIM
CROSS_HEADS = 4
RET_CHUNK = 128
RET_GROUP = 2
ROPE_THETA = 10000.0
EPS = 1e-6

LANES = 128
SUBLANES = 8
VMEM_LIMIT = 48 * 1024 * 1024
TAIL_VMEM_LIMIT = 56 * 1024 * 1024

ROW_TILE = 512
FOX_TQ = 1024
FOX_TK = 512
FOX_PARTS = 2
FOX_ACC_ROWS = 80
SCAN_TILE = 512
FFN_CHUNK = 1024
TAIL_SPLIT = 2
NEG_BIG = -1e30
LOG2E = 1.4426950408889634
SKIP_LOG2 = 152.0
SKIP_REL = 2.0**-8
SKIP_REL_C = 2.0**-18
NORM_SAFETY = 1.0 + 2.0**-7


def _params(sem):
    return pltpu.CompilerParams(dimension_semantics=sem, vmem_limit_bytes=VMEM_LIMIT)


def _rms(x, g):
    ms = jnp.mean(x * x, axis=-1, keepdims=True)
    return x * lax.rsqrt(ms + EPS) * g


def _dot(a, b):
    return jnp.dot(a, b, preferred_element_type=F32)


def _dot_nt(a, b):
    return lax.dot_general(a, b, (((1,), (1,)), ((), ())), preferred_element_type=F32)


def _dot_tn(a, b):
    return lax.dot_general(a, b, (((0,), (0,)), ((), ())), preferred_element_type=F32)


def _full(shape):
    nd = len(shape)
    return pl.BlockSpec(shape, lambda *_: (0,) * nd)


def _scan_rows(a, b, carry):
    n = b.shape[0]
    sub = lax.broadcasted_iota(jnp.int32, b.shape, 0) % SUBLANES
    k = 1
    while k < SUBLANES:
        keep = sub >= k
        b_prev = jnp.where(keep, pltpu.roll(b, k, 0), 0.0)
        if a is None:
            b = b + b_prev
        else:
            a_prev = jnp.where(keep, pltpu.roll(a, k, 0), 1.0)
            b = a * b_prev + b
            a = a * a_prev
        k *= 2
    groups = []
    for g in range(n // SUBLANES):
        rows = slice(g * SUBLANES, (g + 1) * SUBLANES)
        h = b[rows] + carry if a is None else b[rows] + a[rows] * carry
        carry = h[SUBLANES - 1 :, :]
        groups.append(h)
    return jnp.concatenate(groups, axis=0)


def _split3(z):
    hi = z.astype(BF16).astype(F32)
    mid = (z - hi).astype(BF16).astype(F32)
    lo = (z - hi - mid).astype(BF16).astype(F32)
    return hi, mid, lo


def _inproj_kernel(x_ref, g_ref, wl_ref, wf_ref, wr_ref, wff_ref, bf_ref, place_ref,
                   lru_ref, ret_ref, q_ref, k_ref, vt_ref, stats_ref, carry_ref):
    ts = x_ref.shape[0]

    @pl.when(pl.program_id(1) == 0)
    def _():
        carry_ref[...] = jnp.zeros_like(carry_ref)

    h_in = _rms(x_ref[...], g_ref[...]).astype(BF16)
    log_f = jax.nn.log_sigmoid(_dot(h_in, wff_ref[...]) + bf_ref[...])
    c_nat = _scan_rows(None, log_f, carry_ref[...])
    carry_ref[...] = c_nat[-1:, :]
    q_all = (_dot(h_in, wf_ref[:, :FOX_WIDTH]) * (HEAD_DIM**-0.5 * LOG2E)).astype(BF16)
    k_all = _dot(h_in, wf_ref[:, FOX_WIDTH : 2 * FOX_WIDTH]).astype(BF16)
    v_all = _dot(h_in, wf_ref[:, 2 * FOX_WIDTH :]).astype(BF16)
    ccol = c_nat * LOG2E
    terms = jnp.concatenate(_split3(ccol), axis=1).astype(BF16)
    placed = _dot(terms, place_ref[...])

    lane = lax.broadcasted_iota(jnp.int32, (ts, LANES), 1)
    stat_lane = lax.broadcasted_iota(jnp.int32, (1, LANES), 1)
    same_head = (lax.broadcasted_iota(jnp.int32, (LANES, LANES), 0) < HEAD_DIM) == (
        lax.broadcasted_iota(jnp.int32, (LANES, LANES), 1) < HEAD_DIM)
    head_ones = jnp.where(same_head, 1.0, 0.0).astype(BF16)
    stats = []
    for pair in range(FOX_HEADS // 2):
        cols = slice(pair * LANES, (pair + 1) * LANES)
        gq, gk, gv = q_all[:, cols], k_all[:, cols], v_all[:, cols]
        norms = []
        for g in (gq, gk):
            g32 = g.astype(F32)
            sq = _dot((g32 * g32).astype(BF16), head_ones)
            norms.append(jnp.sqrt(jnp.max(sq, axis=0, keepdims=True)) * NORM_SAFETY)
        for e in range(2):
            h = 2 * pair + e
            data = (lane < HEAD_DIM) if e == 0 else (lane >= HEAD_DIM)
            el = lane - (HEAD_DIM if e == 0 else 0)
            extras = placed[:, h * LANES : (h + 1) * LANES]
            q_extra = jnp.where((el >= 3) & (el < 6), 1.0, extras)
            k_extra = jnp.where((el >= 0) & (el < 3), 1.0, extras)
            v_extra = jnp.where(el == (0 if e == 0 else HEAD_DIM - 1), 1.0, 0.0)
            q_ref[0, h] = jnp.where(data, gq, q_extra.astype(BF16))
            k_ref[0, h] = jnp.where(data, gk, k_extra.astype(BF16))
            vt_ref[0, h] = jnp.where(data, gv, v_extra.astype(BF16)).T
            first = e * HEAD_DIM
            q_norm = norms[0][:, first : first + 1]
            k_norm = norms[1][:, first : first + 1]
            c_first = ccol[0:1, h : h + 1]
            c_last = ccol[ts - 1 : ts, h : h + 1]
            stats.append(jnp.where(stat_lane == 0, q_norm, jnp.where(stat_lane == 1, k_norm, jnp.where(stat_lane == 2, c_first, c_last))))
    stats_ref[0, 0] = jnp.concatenate(stats, axis=0)
    lru_ref[...] = _dot(h_in, wl_ref[...])
    ret_ref[...] = _dot(h_in, wr_ref[...])


def _fox_placement():
    place = np.zeros((3 * LANES, FOX_HEADS * LANES), np.float32)
    for h in range(FOX_HEADS):
        first = h * LANES + (HEAD_DIM if h % 2 == 0 else 0)
        for t in range(3):
            place[t * LANES + h, first + t] = 1.0
            place[t * LANES + h, first + 3 + t] = -1.0
    return jnp.asarray(place, dtype=BF16)


def _inproj(x, g, wl, wf, wr, wff, bf, bsz):
    m, d = x.shape
    s = m // bsz
    ts = FOX_TK
    nt = s // ts
    place = _fox_placement()
    row = lambda n: pl.BlockSpec((ts, n), lambda b, i: (b * nt + i, 0))
    per_head = pl.BlockSpec((1, FOX_HEADS, ts, LANES), lambda b, i: (b, 0, i, 0))
    return pl.pallas_call(
        _inproj_kernel,
        grid=(bsz, nt),
        in_specs=[row(d), _full(g.shape), _full(wl.shape), _full(wf.shape), _full(wr.shape), _full(wff.shape),
                  _full(bf.shape), _full(place.shape)],
        out_specs=[
            row(wl.shape[1]), row(wr.shape[1]), per_head, per_head,
            pl.BlockSpec((1, FOX_HEADS, LANES, ts), lambda b, i: (b, 0, 0, i)),
            pl.BlockSpec((1, 1, FOX_HEADS, LANES), lambda b, i: (b, i, 0, 0)),
        ],
        out_shape=[
            jax.ShapeDtypeStruct((m, wl.shape[1]), F32),
            jax.ShapeDtypeStruct((m, wr.shape[1]), F32),
            jax.ShapeDtypeStruct((bsz, FOX_HEADS, s, LANES), BF16),
            jax.ShapeDtypeStruct((bsz, FOX_HEADS, s, LANES), BF16),
            jax.ShapeDtypeStruct((bsz, FOX_HEADS, LANES, s), BF16),
            jax.ShapeDtypeStruct((bsz, nt, FOX_HEADS, LANES), F32),
        ],
        scratch_shapes=[pltpu.VMEM((1, LANES), F32)],
        compiler_params=_params(("parallel", "arbitrary")),
        name="mixer_inproj",
    )(x, g, wl, wf, wr, wff, bf, place)


def _fox_kernel(qn_ref, kn_ref, cf_ref, cl_ref, q_ref, k_ref, vt_ref, o_ref, m_scr, acc_scr, s_scr, mx_scr):
    tq = q_ref.shape[2]
    tk = FOX_TK
    sub = tq // tk
    nk = k_ref.shape[2] // tk
    nq = nk // sub
    qi = pl.program_id(2)
    v_rows = (slice(0, FOX_ACC_ROWS), slice(LANES - FOX_ACC_ROWS, LANES))
    parts = [(i * (tq // FOX_PARTS), tq // FOX_PARTS) for i in range(FOX_PARTS)]

    def first_needed(h):
        head = (pl.program_id(0) * (FOX_HEADS // 2) + pl.program_id(1)) * 2 + h
        q_norm = qn_ref[head * nq + qi]
        c_first = cf_ref[head * nq + qi]
        floor = -q_norm * kn_ref[head * nk + qi * sub + sub - 1]

        def needed(kb):
            j = head * nk + jnp.maximum(kb, 0)
            dot_bound = q_norm * kn_ref[j]
            slack = SKIP_REL * (dot_bound - floor) + SKIP_REL_C * (jnp.abs(c_first) + jnp.abs(cl_ref[j]))
            return dot_bound + c_first - cl_ref[j] + slack >= floor - SKIP_LOG2

        kb = lax.while_loop(lambda kb: jnp.logical_and(kb >= 0, needed(kb)), lambda kb: kb - 1, qi * sub - 1)
        return kb + 1

    def logits(h, kb, slot, diag, part=None):
        start = pl.multiple_of(kb * tk, tk)
        q0, qn = part if part is not None else ((0 if diag is None else diag * tk), None)
        lanes = slice(q0, None if qn is None else q0 + qn)
        s = _dot_nt(k_ref[0, h, pl.ds(start, tk), :], q_ref[0, h, lanes, :])
        if diag is not None:
            key = lax.broadcasted_iota(jnp.int32, s.shape, 0)
            qry = lax.broadcasted_iota(jnp.int32, s.shape, 1)
            s = jnp.where(key <= qry, s, NEG_BIG)
        s_scr[slot, :, lanes] = s
        mx_scr[slot, :, lanes] = jnp.max(s, axis=0, keepdims=True)

    def accumulate(h, kb, slot, diag=None, part=None):
        start = pl.multiple_of(kb * tk, tk)
        q0, qn = part if part is not None else ((0 if diag is None else diag * tk), None)
        lanes = slice(q0, None if qn is None else q0 + qn)
        m_old = m_scr[h, :, lanes]
        m_new = jnp.maximum(m_old, mx_scr[slot, :, lanes])
        m_scr[h, :, lanes] = m_new
        p = jnp.exp2(s_scr[slot, :, lanes] - m_new).astype(BF16)
        vt = vt_ref[0, h, v_rows[h], pl.ds(start, tk)]
        acc_scr[h, :, lanes] = jnp.exp2(m_old - m_new) * acc_scr[h, :, lanes] + _dot(vt, p)

    m_scr[...] = jnp.full_like(m_scr, NEG_BIG)
    acc_scr[...] = jnp.zeros_like(acc_scr)
    own = qi * sub
    lo = jnp.minimum(first_needed(0), first_needed(1))
    logits(0, own, 0, 0)
    logits(1, own, 1, 0)
    accumulate(0, own, 0, 0)
    for u in range(1, sub):
        logits(0, own + u, 0, u)
        accumulate(1, own + u - 1, 1, u - 1)
        logits(1, own + u, 1, u)
        accumulate(0, own + u, 0, u)
    last_own = own + sub - 1

    @pl.when(lo >= own)
    def _():
        accumulate(1, last_own, 1, sub - 1)

    @pl.when(lo < own)
    def _():
        logits(0, lo, 0, None)
        accumulate(1, last_own, 1, sub - 1)

        def body(kb, carry):
            for part in parts:
                logits(1, kb, 1, None, part)
                accumulate(0, kb, 0, None, part)
            for part in parts:
                logits(0, kb + 1, 0, None, part)
                accumulate(1, kb, 1, None, part)
            return carry

        lax.fori_loop(lo, own - 1, body, 0)
        logits(1, own - 1, 1, None)
        accumulate(0, own - 1, 0)
        accumulate(1, own - 1, 1)

    even, odd = acc_scr[0], acc_scr[1]
    pad = FOX_ACC_ROWS - HEAD_DIM
    o = jnp.concatenate([even[:HEAD_DIM] / even[HEAD_DIM : HEAD_DIM + 1], odd[pad:] / odd[pad - 1 : pad]], axis=0)
    o_ref[...] = o.T.astype(o_ref.dtype)


def _fox_attention(q, k, vt, stats):
    bsz, nh, s, _ = q.shape
    tq = FOX_TQ
    sub = tq // FOX_TK
    nq = s // tq
    npair = nh // 2
    st = stats[..., :4].transpose(0, 2, 1, 3)
    per_q = lambda v: v.reshape(bsz, nh, nq, sub)
    scalars = (
        per_q(st[..., 0]).max(axis=-1),
        lax.cummax(st[..., 1], axis=2),
        per_q(st[..., 2])[..., 0],
        st[..., 3],
    )
    scalars = tuple(v.reshape(-1) for v in scalars)
    return pl.pallas_call(
        _fox_kernel,
        grid_spec=pltpu.PrefetchScalarGridSpec(
            num_scalar_prefetch=len(scalars),
            grid=(bsz, npair, nq),
            in_specs=[
                pl.BlockSpec((1, 2, tq, LANES), lambda b, p, i, *_: (b, p, i, 0)),
                pl.BlockSpec((1, 2, s, LANES), lambda b, p, i, *_: (b, p, 0, 0)),
                pl.BlockSpec((1, 2, LANES, s), lambda b, p, i, *_: (b, p, 0, 0)),
            ],
            out_specs=pl.BlockSpec((tq, LANES), lambda b, p, i, *_: (b * nq + i, p)),
            scratch_shapes=[
                pltpu.VMEM((2, 1, tq), F32),
                pltpu.VMEM((2, FOX_ACC_ROWS, tq), F32),
                pltpu.VMEM((2, FOX_TK, tq), F32),
                pltpu.VMEM((2, 1, tq), F32),
            ],
        ),
        out_shape=jax.ShapeDtypeStruct((bsz * s, nh * HEAD_DIM), BF16),
        compiler_params=_params(("parallel", "parallel", "arbitrary")),
        name="fox_attention",
    )(*scalars, q, k, vt)


def _lru_kernel(xy_ref, cw_ref, cb_ref, wg_ref, bg_ref, lam_ref, o_ref, tail_scr, h_scr):
    w = LRU_WIDTH

    @pl.when(pl.program_id(1) == 0)
    def _():
        tail_scr[...] = jnp.zeros_like(tail_scr)
        h_scr[...] = jnp.zeros_like(h_scr)

    x = xy_ref[:, :w]
    y = xy_ref[:, w:]
    tail = tail_scr[...]
    row8 = lax.broadcasted_iota(jnp.int32, (SUBLANES, w), 0)
    conv = x * cw_ref[CONV_WIDTH - 1 : CONV_WIDTH, :] + cb_ref[...]
    for j in range(1, CONV_WIDTH):
        xs = pltpu.roll(x, j, 0)
        head = jnp.where(row8 < j, pltpu.roll(tail, j, 0), xs[:SUBLANES])
        xs = jnp.concatenate([head, xs[SUBLANES:]], axis=0)
        conv = conv + xs * cw_ref[CONV_WIDTH - 1 - j : CONV_WIDTH - j, :]
    tail_scr[...] = x[-SUBLANES:]

    gates = jax.nn.sigmoid(_dot(conv.astype(BF16), wg_ref[...]) + bg_ref[...])
    r = gates[:, :w]
    i = gates[:, w:]
    log_a = -LRU_C * r * jax.nn.softplus(-lam_ref[...])
    a = jnp.exp(log_a)
    th = jnp.tanh(log_a)
    u = jnp.sqrt(-2.0 * th / (1.0 - th)) * (i * conv)
    h = _scan_rows(a, u, h_scr[...])
    h_scr[...] = h[-1:, :]
    o_ref[...] = (h * jax.nn.gelu(y)).astype(o_ref.dtype)


def _rg_lru(xy, cw, cb, wg, bg, lam, bsz):
    m = xy.shape[0]
    ts = SCAN_TILE
    nt = m // bsz // ts
    w = LRU_WIDTH
    return pl.pallas_call(
        _lru_kernel,
        grid=(bsz, nt),
        in_specs=[
            pl.BlockSpec((ts, 2 * w), lambda b, i: (b * nt + i, 0)),
            _full(cw.shape), _full(cb.shape), _full(wg.shape), _full(bg.shape), _full(lam.shape),
        ],
        out_specs=pl.BlockSpec((ts, w), lambda b, i: (b * nt + i, 0)),
        out_shape=jax.ShapeDtypeStruct((m, w), BF16),
        scratch_shapes=[pltpu.VMEM((SUBLANES, w), F32), pltpu.VMEM((1, w), F32)],
        compiler_params=_params(("parallel", "arbitrary")),
        name="rg_lru",
    )(xy, cw, cb, wg, bg, lam)


def _rope_kernel(pos_ref, freq_ref, cos_ref, sin_ref):
    ang = pos_ref[...].astype(F32) * freq_ref[...]
    lane = lax.broadcasted_iota(jnp.int32, ang.shape, 1)
    first_half = (lane % HEAD_DIM) < HEAD_DIM // 2
    cos_ref[...] = jnp.cos(ang)
    s = jnp.sin(ang)
    sin_ref[...] = jnp.where(first_half, -s, s)


def _rope_tables(pos, freq):
    m = pos.shape[0]
    tm = ROW_TILE
    spec = pl.BlockSpec((tm, LANES), lambda i: (i, 0))
    return pl.pallas_call(
        _rope_kernel,
        grid=(m // tm,),
        in_specs=[spec, _full(freq.shape)],
        out_specs=[spec, spec],
        out_shape=[jax.ShapeDtypeStruct((m, LANES), F32)] * 2,
        compiler_params=_params(("parallel",)),
        name="rope_tables",
    )(pos, freq)


def _group_mean(z, avg):
    hi, mid, lo = _split3(z)
    return _dot(hi.astype(BF16), avg) + _dot(mid.astype(BF16), avg) + _dot(lo.astype(BF16), avg)


def _ret_kernel(x_ref, cos_ref, sin_ref, decay_ref, kw_ref, qw_ref, cd_ref, avg_ref, o_ref, state_scr):
    c = RET_CHUNK
    w = RET_WIDTH
    npair = RET_HEADS // 2

    @pl.when(pl.program_id(1) == 0)
    def _():
        state_scr[...] = jnp.zeros_like(state_scr)

    cos = cos_ref[...]
    sin = sin_ref[...]
    lane = lax.broadcasted_iota(jnp.int32, cos.shape, 1)
    first_half = (lane % HEAD_DIM) < HEAD_DIM // 2

    def rotary(z):
        swapped = jnp.where(first_half, pltpu.roll(z, LANES - HEAD_DIM // 2, 1), pltpu.roll(z, HEAD_DIM // 2, 1))
        return z * cos + swapped * sin

    ts = x_ref.shape[0]
    nchunk = ts // c
    row_chunk = lax.broadcasted_iota(jnp.int32, (ts, LANES), 0) // c
    ys = []
    for p in range(npair):
        cols = slice(p * LANES, (p + 1) * LANES)
        q = rotary(x_ref[:, cols])
        k = rotary(x_ref[:, w + p * LANES : w + (p + 1) * LANES]) * HEAD_DIM**-0.5
        v = x_ref[:, 2 * w + p * LANES : 2 * w + (p + 1) * LANES].astype(BF16)
        kb = k.astype(BF16)
        q_cross = (q * qw_ref[:, cols]).astype(BF16)
        k_state = (k * kw_ref[:, cols]).astype(BF16)
        zero = jnp.zeros_like(v)
        v_blocks = jnp.concatenate([jnp.where(row_chunk == n, v, zero) for n in range(nchunk)], axis=1)
        incs = _dot_tn(k_state, v_blocks)
        cd = cd_ref[p]
        states = [state_scr[p]]
        for n in range(nchunk):
            states.append(cd * states[-1] + jnp.where(cd > 0.0, incs[:, n * LANES : (n + 1) * LANES], 0.0))
        state_scr[p] = states[-1]
        q_blocks = jnp.concatenate([jnp.where(row_chunk == n, q_cross, zero) for n in range(nchunk)], axis=1)
        y = _dot(q_blocks, jnp.concatenate([st.astype(BF16) for st in states[:nchunk]], axis=0))
        span = RET_GROUP * c
        span_low = lax.broadcasted_iota(jnp.int32, (span, LANES), 1) < HEAD_DIM
        span_head = (span_low, jnp.logical_not(span_low))
        inner = []
        for r0 in range(0, ts, span):
            r = slice(r0, r0 + span)
            per_head = []
            for h in range(2):
                q_head = jnp.where(span_head[h], q[r], 0.0).astype(BF16)
                scores = _dot_nt(q_head, kb[r]) * decay_ref[2 * p + h]
                per_head.append(_dot(scores.astype(BF16), v[r]))
            inner.append(jnp.where(span_low, per_head[0], per_head[1]))
        ys.append(y + jnp.concatenate(inner, axis=0))
    y = jnp.concatenate(ys, axis=1)
    avg = avg_ref[...]
    mu = _group_mean(y, avg)
    d = y - mu
    var = _group_mean(d * d, avg)
    yn = d * lax.rsqrt(var + EPS)
    g = x_ref[:, 3 * w :]
    o_ref[...] = (g * jax.nn.sigmoid(g) * yn).astype(o_ref.dtype)


def _retention(ret, cos, sin, decay, kw, qw, cd, avg, bsz):
    m = ret.shape[0]
    ts = ROW_TILE
    nt = m // bsz // ts
    w = RET_WIDTH
    return pl.pallas_call(
        _ret_kernel,
        grid=(bsz, nt),
        in_specs=[
            pl.BlockSpec((ts, 4 * w), lambda b, i: (b * nt + i, 0)),
            pl.BlockSpec((ts, LANES), lambda b, i: (b * nt + i, 0)),
            pl.BlockSpec((ts, LANES), lambda b, i: (b * nt + i, 0)),
            _full(decay.shape), _full(kw.shape), _full(qw.shape), _full(cd.shape), _full(avg.shape),
        ],
        out_specs=pl.BlockSpec((ts, w), lambda b, i: (b * nt + i, 0)),
        out_shape=jax.ShapeDtypeStruct((m, w), BF16),
        scratch_shapes=[pltpu.VMEM((RET_HEADS // 2, LANES, LANES), F32)],
        compiler_params=_params(("parallel", "arbitrary")),
        name="retention",
    )(ret, cos, sin, decay, kw, qw, cd, avg)


def _norm_matmul_kernel(x_ref, g_ref, w_ref, o_ref):
    o_ref[...] = _dot(_rms(x_ref[...], g_ref[...]).astype(BF16), w_ref[...]).astype(o_ref.dtype)


def _norm_matmul(x, g, w, tm):
    m, d = x.shape
    n = w.shape[1]
    return pl.pallas_call(
        _norm_matmul_kernel,
        grid=(m // tm,),
        in_specs=[pl.BlockSpec((tm, d), lambda i: (i, 0)), _full(g.shape), _full(w.shape)],
        out_specs=pl.BlockSpec((tm, n), lambda i: (i, 0)),
        out_shape=jax.ShapeDtypeStruct((m, n), BF16),
        compiler_params=_params(("parallel",)),
        name="memory_kv_proj",
    )(x, g, w)


def _tail_kernel(x_ref, lru_ref, fox_ref, ret_ref, wm_ref, gm_ref, g1_ref, wq_ref, kt_ref, v_ref, wo_ref, g2_ref,
                 g3_ref, wg_ref, wu_ref, wd_ref, g4_ref, o_ref):
    d = x_ref.shape[1]
    hd = d // CROSS_HEADS
    a, b = LRU_WIDTH, LRU_WIDTH + FOX_WIDTH
    dff = wg_ref.shape[1]
    bounds = [(lo, min(lo + FFN_CHUNK, dff)) for lo in range(0, dff, FFN_CHUNK)]
    head_cols = [slice(h * hd, (h + 1) * hd) for h in range(CROSS_HEADS)]

    def phases(rows):
        mix = _dot(lru_ref[rows, :], wm_ref[:a, :]) + _dot(fox_ref[rows, :], wm_ref[a:b, :]) + _dot(ret_ref[rows, :], wm_ref[b:, :])
        yield
        x = x_ref[rows, :] + _rms(mix, gm_ref[...])
        h = _rms(x, g1_ref[...]).astype(BF16)
        yield
        q = _dot(h, wq_ref[...]).astype(BF16)
        yield
        logits = [_dot(q[:, cols], kt_ref[0, cols, :]) * hd**-0.5 for cols in head_cols]
        yield
        outs = []
        for cols, s in zip(head_cols, logits):
            e = jnp.exp(s - jnp.max(s, axis=-1, keepdims=True))
            p = e / jnp.sum(e, axis=-1, keepdims=True)
            outs.append(_dot(p.astype(BF16), v_ref[0, :, cols]).astype(BF16))
        yield
        attn = _dot(jnp.concatenate(outs, axis=1), wo_ref[...])
        yield
        x = x + _rms(attn, g2_ref[...])
        h = _rms(x, g3_ref[...]).astype(BF16)
        yield
        y = None
        for lo, hi in bounds:
            gate = _dot(h, wg_ref[:, lo:hi])
            up = _dot(h, wu_ref[:, lo:hi])
            yield
            act = (gate * jax.nn.sigmoid(gate) * up).astype(BF16)
            part = _dot(act, wd_ref[lo:hi, :])
            y = part if y is None else y + part
            yield
        o_ref[rows, :] = x + _rms(y, g4_ref[...])

    tm = x_ref.shape[0]
    step = tm // TAIL_SPLIT
    gens = [phases(slice(g * step, (g + 1) * step)) for g in range(TAIL_SPLIT)]
    live = list(gens)
    started = 0
    while live:
        if started < len(gens):
            started += 1
        for gen in list(gens[:started]):
            if gen in live:
                try:
                    next(gen)
                except StopIteration:
                    live.remove(gen)


def _layer_tail(x, lru, fox, ret, wm, gm, g1, wq, kt, v, wo, g2, g3, wg, wu, wd, g4, bsz):
    m, d = x.shape
    tm = ROW_TILE
    nt = m // bsz // tm
    mem_len = v.shape[1]
    row = lambda n: pl.BlockSpec((tm, n), lambda i: (i, 0))
    resident = lambda arr: pl.BlockSpec(arr.shape, lambda i: (0,) * arr.ndim, pipeline_mode=pl.Buffered(1))
    return pl.pallas_call(
        _tail_kernel,
        grid=(m // tm,),
        in_specs=[
            row(d), row(lru.shape[1]), row(fox.shape[1]), row(ret.shape[1]),
            resident(wm), resident(gm), resident(g1), resident(wq),
            pl.BlockSpec((1, d, mem_len), lambda i: (i // nt, 0, 0)),
            pl.BlockSpec((1, mem_len, d), lambda i: (i // nt, 0, 0)),
            resident(wo), resident(g2), resident(g3), resident(wg), resident(wu), resident(wd), resident(g4),
        ],
        out_specs=row(d),
        out_shape=jax.ShapeDtypeStruct((m, d), F32),
        compiler_params=pltpu.CompilerParams(dimension_semantics=("parallel",), vmem_limit_bytes=TAIL_VMEM_LIMIT),
        name="layer_tail",
    )(x, lru, fox, ret, wm, gm, g1, wq, kt, v, wo, g2, g3, wg, wu, wd, g4)


def _block_diag(w):
    n, a, b = w.shape
    eye = jnp.eye(n, dtype=w.dtype)
    return (eye[:, None, :, None] * w[:, :, None, :]).reshape(n * a, n * b)


def _retention_tables():
    c = RET_CHUNK
    f32 = np.float32
    log_gamma = np.log1p(-np.exp2(-5.0 - np.arange(RET_HEADS, dtype=f32))).astype(f32)
    idx = np.arange(c, dtype=f32)
    diff = idx[:, None] - idx[None, :]
    decay = np.where(diff >= 0, np.exp(log_gamma[:, None, None] * np.maximum(diff, f32(0))), f32(0)).astype(f32)
    decay = np.stack([np.kron(np.eye(RET_GROUP, dtype=f32), decay[h]) for h in range(RET_HEADS)])
    k_w = np.exp(log_gamma[None, :] * (c - 1.0 - idx)[:, None]).astype(f32)
    q_w = np.exp(log_gamma[None, :] * (idx + 1.0)[:, None]).astype(f32)
    chunk_decay = np.exp(log_gamma * f32(c)).astype(f32)
    reps = ROW_TILE // c
    kw = np.tile(np.repeat(k_w, HEAD_DIM, axis=1), (reps, 1))
    qw = np.tile(np.repeat(q_w, HEAD_DIM, axis=1), (reps, 1))
    ones = np.ones((HEAD_DIM, HEAD_DIM), f32)
    cd = np.stack([
        np.kron(np.diag(chunk_decay[2 * p : 2 * p + 2]), ones) for p in range(RET_HEADS // 2)
    ])
    avg = np.kron(np.eye(RET_HEADS, dtype=f32), np.full((HEAD_DIM, HEAD_DIM), 1.0 / HEAD_DIM, f32))
    return jnp.asarray(decay), jnp.asarray(kw), jnp.asarray(qw), jnp.asarray(cd), jnp.asarray(avg, dtype=BF16)


def kernel(x, mem, positions, pre_mix_g, post_mix_g, w_in, conv_w, conv_b, w_rg, b_rg, w_ig, b_ig, lru_lambda, fox_b_f, w_out, pre_cross_g, post_cross_g, mem_norm_g, w_cq, w_ck, w_cv, w_co, pre_ffn_g, post_ffn_g, w_gate, w_up, w_down):
    bsz, seq, d = x.shape
    depth = w_in.shape[0]
    m = bsz * seq
    mem_len = mem.shape[1]
    xf = x.reshape(m, d)
    row = lambda v: v.reshape(1, -1)

    half = HEAD_DIM // 2
    inv_freq = ROPE_THETA ** (-jnp.arange(half, dtype=F32) / half)
    freq = jnp.tile(inv_freq, LANES // half).reshape(1, LANES)
    pos = jnp.broadcast_to(positions.reshape(m, 1), (m, LANES))
    cos, sin = _rope_tables(pos, freq)
    decay, kw, qw, cd, avg = _retention_tables()

    o_fox = 2 * LRU_WIDTH
    o_ff = o_fox + 3 * FOX_WIDTH
    o_ret = o_ff + FOX_HEADS

    for l in range(depth):
        order = jnp.argsort(fox_b_f[l])
        by_head = lambda w: w.reshape(d, FOX_HEADS, HEAD_DIM)[:, order].reshape(d, FOX_WIDTH)
        wl = w_in[l, :, :o_fox].astype(BF16)
        wf = jnp.concatenate(
            [by_head(w_in[l, :, o_fox + i * FOX_WIDTH : o_fox + (i + 1) * FOX_WIDTH]) for i in range(3)], axis=1
        ).astype(BF16)
        wff = jnp.pad(w_in[l, :, o_ff:o_ret][:, order], ((0, 0), (0, LANES - FOX_HEADS))).astype(BF16)
        wr = w_in[l, :, o_ret:].astype(BF16)
        bf = jnp.pad(fox_b_f[l][order], (0, LANES - FOX_HEADS)).reshape(1, LANES)
        fox_rows = w_out[l, LRU_WIDTH : LRU_WIDTH + FOX_WIDTH].reshape(FOX_HEADS, HEAD_DIM, d)[order].reshape(FOX_WIDTH, d)
        wm = jnp.concatenate([w_out[l, :LRU_WIDTH], fox_rows, w_out[l, LRU_WIDTH + FOX_WIDTH :]], axis=0).astype(BF16)
        lru, ret, *fox_operands = _inproj(xf, row(pre_mix_g[l]), wl, wf, wr, wff, bf, bsz)
        fox_o = _fox_attention(*fox_operands)

        wg = jnp.concatenate([_block_diag(w_rg[l]), _block_diag(w_ig[l])], axis=1).astype(BF16)
        bg = jnp.concatenate([b_rg[l], b_ig[l]]).reshape(1, -1)
        lru_o = _rg_lru(lru, conv_w[l], row(conv_b[l]), wg, bg, row(lru_lambda[l]), bsz)

        ret_o = _retention(ret, cos, sin, decay, kw, qw, cd, avg, bsz)

        wkv = jnp.concatenate([w_ck[l], w_cv[l]], axis=1).astype(BF16)
        kv = _norm_matmul(mem.reshape(bsz * mem_len, d), row(mem_norm_g), wkv, mem_len)
        kt = kv[:, :d].reshape(bsz, mem_len, d).transpose(0, 2, 1)
        vv = kv[:, d:].reshape(bsz, mem_len, d)
        xf = _layer_tail(
            xf, lru_o, fox_o, ret_o, wm, row(post_mix_g[l]),
            row(pre_cross_g[l]), w_cq[l].astype(BF16), kt, vv, w_co[l].astype(BF16), row(post_cross_g[l]),
            row(pre_ffn_g[l]), w_gate[l].astype(BF16), w_up[l].astype(BF16), w_down[l].astype(BF16), row(post_ffn_g[l]), bsz)
    return xf.reshape(bsz, seq, d)
```

```python
import jax
import jax.numpy as jnp
import numpy as np
from jax import lax
from jax.experimental import pallas as pl
from jax.experimental.pallas import tpu as pltpu

F32 = jnp.float32
BF16 = jnp.bfloat16

HEAD_DIM = 64
LRU_WIDTH = 256
CONV_WIDTH = 4
LRU_C = 8.0
FOX_HEADS = 8
FOX_WIDTH = FOX_HEADS * HEAD_DIM
RET_HEADS = 4
RET_WIDTH = RET_HEADS * HEAD_DIM
CROSS_HEADS = 4
RET_CHUNK = 128
RET_GROUP = 2
ROPE_THETA = 10000.0
EPS = 1e-6

LANES = 128
SUBLANES = 8
VMEM_LIMIT = 48 * 1024 * 1024
TAIL_VMEM_LIMIT = 56 * 1024 * 1024

ROW_TILE = 512
FOX_TQ = 1024
FOX_TK = 512
FOX_PARTS = 2
FOX_ACC_ROWS = 80
SCAN_TILE = 1024
FFN_CHUNK = 1024
TAIL_SPLIT = 2
NEG_BIG = -1e30
LOG2E = 1.4426950408889634
SKIP_LOG2 = 152.0
SKIP_REL = 2.0**-8
SKIP_REL_C = 2.0**-18
NORM_SAFETY = 1.0 + 2.0**-7


def _params(sem):
    return pltpu.CompilerParams(dimension_semantics=sem, vmem_limit_bytes=VMEM_LIMIT)


def _rms(x, g):
    ms = jnp.mean(x * x, axis=-1, keepdims=True)
    return x * lax.rsqrt(ms + EPS) * g


def _dot(a, b):
    return jnp.dot(a, b, preferred_element_type=F32)


def _dot_nt(a, b):
    return lax.dot_general(a, b, (((1,), (1,)), ((), ())), preferred_element_type=F32)


def _dot_tn(a, b):
    return lax.dot_general(a, b, (((0,), (0,)), ((), ())), preferred_element_type=F32)


def _full(shape):
    nd = len(shape)
    return pl.BlockSpec(shape, lambda *_: (0,) * nd)


def _scan_rows(a, b, carry):
    n = b.shape[0]
    sub = lax.broadcasted_iota(jnp.int32, b.shape, 0) % SUBLANES
    k = 1
    while k < SUBLANES:
        keep = sub >= k
        b_prev = jnp.where(keep, pltpu.roll(b, k, 0), 0.0)
        if a is None:
            b = b + b_prev
        else:
            a_prev = jnp.where(keep, pltpu.roll(a, k, 0), 1.0)
            b = a * b_prev + b
            a = a * a_prev
        k *= 2
    groups = []
    for g in range(n // SUBLANES):
        rows = slice(g * SUBLANES, (g + 1) * SUBLANES)
        h = b[rows] + carry if a is None else b[rows] + a[rows] * carry
        carry = h[SUBLANES - 1 :, :]
        groups.append(h)
    return jnp.concatenate(groups, axis=0)


def _split3(z):
    hi = z.astype(BF16).astype(F32)
    mid = (z - hi).astype(BF16).astype(F32)
    lo = (z - hi - mid).astype(BF16).astype(F32)
    return hi, mid, lo


def _inproj_kernel(x_ref, g_ref, wl_ref, wf_ref, wr_ref, wff_ref, bf_ref, place_ref,
                   lru_ref, ret_ref, q_ref, k_ref, vt_ref, stats_ref, carry_ref):
    ts = x_ref.shape[0]

    @pl.when(pl.program_id(1) == 0)
    def _():
        carry_ref[...] = jnp.zeros_like(carry_ref)

    h_in = _rms(x_ref[...], g_ref[...]).astype(BF16)
    log_f = jax.nn.log_sigmoid(_dot(h_in, wff_ref[...]) + bf_ref[...])
    c_nat = _scan_rows(None, log_f, carry_ref[...])
    carry_ref[...] = c_nat[-1:, :]
    q_all = (_dot(h_in, wf_ref[:, :FOX_WIDTH]) * (HEAD_DIM**-0.5 * LOG2E)).astype(BF16)
    k_all = _dot(h_in, wf_ref[:, FOX_WIDTH : 2 * FOX_WIDTH]).astype(BF16)
    v_all = _dot(h_in, wf_ref[:, 2 * FOX_WIDTH :]).astype(BF16)
    ccol = c_nat * LOG2E
    terms = jnp.concatenate(_split3(ccol), axis=1).astype(BF16)
    placed = _dot(terms, place_ref[...])

    lane = lax.broadcasted_iota(jnp.int32, (ts, LANES), 1)
    stat_lane = lax.broadcasted_iota(jnp.int32, (1, LANES), 1)
    same_head = (lax.broadcasted_iota(jnp.int32, (LANES, LANES), 0) < HEAD_DIM) == (
        lax.broadcasted_iota(jnp.int32, (LANES, LANES), 1) < HEAD_DIM)
    head_ones = jnp.where(same_head, 1.0, 0.0).astype(BF16)
    stats = []
    for pair in range(FOX_HEADS // 2):
        cols = slice(pair * LANES, (pair + 1) * LANES)
        gq, gk, gv = q_all[:, cols], k_all[:, cols], v_all[:, cols]
        norms = []
        for g in (gq, gk):
            g32 = g.astype(F32)
            sq = _dot((g32 * g32).astype(BF16), head_ones)
            norms.append(jnp.sqrt(jnp.max(sq, axis=0, keepdims=True)) * NORM_SAFETY)
        for e in range(2):
            h = 2 * pair + e
            data = (lane < HEAD_DIM) if e == 0 else (lane >= HEAD_DIM)
            el = lane - (HEAD_DIM if e == 0 else 0)
            extras = placed[:, h * LANES : (h + 1) * LANES]
            q_extra = jnp.where((el >= 3) & (el < 6), 1.0, extras)
            k_extra = jnp.where((el >= 0) & (el < 3), 1.0, extras)
            v_extra = jnp.where(el == (0 if e == 0 else HEAD_DIM - 1), 1.0, 0.0)
            q_ref[0, h] = jnp.where(data, gq, q_extra.astype(BF16))
            k_ref[0, h] = jnp.where(data, gk, k_extra.astype(BF16))
            vt_ref[0, h] = jnp.where(data, gv, v_extra.astype(BF16)).T
            first = e * HEAD_DIM
            q_norm = norms[0][:, first : first + 1]
            k_norm = norms[1][:, first : first + 1]
            c_first = ccol[0:1, h : h + 1]
            c_last = ccol[ts - 1 : ts, h : h + 1]
            stats.append(jnp.where(stat_lane == 0, q_norm, jnp.where(stat_lane == 1, k_norm, jnp.where(stat_lane == 2, c_first, c_last))))
    stats_ref[0, 0] = jnp.concatenate(stats, axis=0)
    lru_ref[...] = _dot(h_in, wl_ref[...])
    ret_ref[...] = _dot(h_in, wr_ref[...])


def _fox_placement():
    place = np.zeros((3 * LANES, FOX_HEADS * LANES), np.float32)
    for h in range(FOX_HEADS):
        first = h * LANES + (HEAD_DIM if h % 2 == 0 else 0)
        for t in range(3):
            place[t * LANES + h, first + t] = 1.0
            place[t * LANES + h, first + 3 + t] = -1.0
    return jnp.asarray(place, dtype=BF16)


def _inproj(x, g, wl, wf, wr, wff, bf, bsz):
    m, d = x.shape
    s = m // bsz
    ts = FOX_TK
    nt = s // ts
    place = _fox_placement()
    row = lambda n: pl.BlockSpec((ts, n), lambda b, i: (b * nt + i, 0))
    per_head = pl.BlockSpec((1, FOX_HEADS, ts, LANES), lambda b, i: (b, 0, i, 0))
    return pl.pallas_call(
        _inproj_kernel,
        grid=(bsz, nt),
        in_specs=[row(d), _full(g.shape), _full(wl.shape), _full(wf.shape), _full(wr.shape), _full(wff.shape),
                  _full(bf.shape), _full(place.shape)],
        out_specs=[
            row(wl.shape[1]), row(wr.shape[1]), per_head, per_head,
            pl.BlockSpec((1, FOX_HEADS, LANES, ts), lambda b, i: (b, 0, 0, i)),
            pl.BlockSpec((1, 1, FOX_HEADS, LANES), lambda b, i: (b, i, 0, 0)),
        ],
        out_shape=[
            jax.ShapeDtypeStruct((m, wl.shape[1]), F32),
            jax.ShapeDtypeStruct((m, wr.shape[1]), F32),
            jax.ShapeDtypeStruct((bsz, FOX_HEADS, s, LANES), BF16),
            jax.ShapeDtypeStruct((bsz, FOX_HEADS, s, LANES), BF16),
            jax.ShapeDtypeStruct((bsz, FOX_HEADS, LANES, s), BF16),
            jax.ShapeDtypeStruct((bsz, nt, FOX_HEADS, LANES), F32),
        ],
        scratch_shapes=[pltpu.VMEM((1, LANES), F32)],
        compiler_params=_params(("parallel", "arbitrary")),
        name="mixer_inproj",
    )(x, g, wl, wf, wr, wff, bf, place)


def _fox_kernel(qn_ref, kn_ref, cf_ref, cl_ref, q_ref, k_ref, vt_ref, o_ref, m_scr, acc_scr, s_scr, mx_scr):
    tq = q_ref.shape[2]
    tk = FOX_TK
    sub = tq // tk
    nk = k_ref.shape[2] // tk
    nq = nk // sub
    qi = pl.program_id(2)
    v_rows = (slice(0, FOX_ACC_ROWS), slice(LANES - FOX_ACC_ROWS, LANES))
    parts = [(i * (tq // FOX_PARTS), tq // FOX_PARTS) for i in range(FOX_PARTS)]

    def first_needed(h):
        head = (pl.program_id(0) * (FOX_HEADS // 2) + pl.program_id(1)) * 2 + h
        q_norm = qn_ref[head * nq + qi]
        c_first = cf_ref[head * nq + qi]
        floor = -q_norm * kn_ref[head * nk + qi * sub + sub - 1]

        def needed(kb):
            j = head * nk + jnp.maximum(kb, 0)
            dot_bound = q_norm * kn_ref[j]
            slack = SKIP_REL * (dot_bound - floor) + SKIP_REL_C * (jnp.abs(c_first) + jnp.abs(cl_ref[j]))
            return dot_bound + c_first - cl_ref[j] + slack >= floor - SKIP_LOG2

        kb = lax.while_loop(lambda kb: jnp.logical_and(kb >= 0, needed(kb)), lambda kb: kb - 1, qi * sub - 1)
        return kb + 1

    def logits(h, kb, slot, diag, part=None):
        start = pl.multiple_of(kb * tk, tk)
        q0, qn = part if part is not None else ((0 if diag is None else diag * tk), None)
        lanes = slice(q0, None if qn is None else q0 + qn)
        s = _dot_nt(k_ref[0, h, pl.ds(start, tk), :], q_ref[0, h, lanes, :])
        if diag is not None:
            key = lax.broadcasted_iota(jnp.int32, s.shape, 0)
            qry = lax.broadcasted_iota(jnp.int32, s.shape, 1)
            s = jnp.where(key <= qry, s, NEG_BIG)
        s_scr[slot, :, lanes] = s
        mx_scr[slot, :, lanes] = jnp.max(s, axis=0, keepdims=True)

    def accumulate(h, kb, slot, diag=None, part=None):
        start = pl.multiple_of(kb * tk, tk)
        q0, qn = part if part is not None else ((0 if diag is None else diag * tk), None)
        lanes = slice(q0, None if qn is None else q0 + qn)
        m_old = m_scr[h, :, lanes]
        m_new = jnp.maximum(m_old, mx_scr[slot, :, lanes])
        m_scr[h, :, lanes] = m_new
        p = jnp.exp2(s_scr[slot, :, lanes] - m_new).astype(BF16)
        vt = vt_ref[0, h, v_rows[h], pl.ds(start, tk)]
        acc_scr[h, :, lanes] = jnp.exp2(m_old - m_new) * acc_scr[h, :, lanes] + _dot(vt, p)

    m_scr[...] = jnp.full_like(m_scr, NEG_BIG)
    acc_scr[...] = jnp.zeros_like(acc_scr)
    own = qi * sub
    lo = jnp.minimum(first_needed(0), first_needed(1))
    logits(0, own, 0, 0)
    logits(1, own, 1, 0)
    accumulate(0, own, 0, 0)
    for u in range(1, sub):
        logits(0, own + u, 0, u)
        accumulate(1, own + u - 1, 1, u - 1)
        logits(1, own + u, 1, u)
        accumulate(0, own + u, 0, u)
    last_own = own + sub - 1

    @pl.when(lo >= own)
    def _():
        accumulate(1, last_own, 1, sub - 1)

    @pl.when(lo < own)
    def _():
        logits(0, lo, 0, None)
        accumulate(1, last_own, 1, sub - 1)

        def body(kb, carry):
            for part in parts:
                logits(1, kb, 1, None, part)
                accumulate(0, kb, 0, None, part)
            for part in parts:
                logits(0, kb + 1, 0, None, part)
                accumulate(1, kb, 1, None, part)
            return carry

        lax.fori_loop(lo, own - 1, body, 0)
        logits(1, own - 1, 1, None)
        accumulate(0, own - 1, 0)
        accumulate(1, own - 1, 1)

    even, odd = acc_scr[0], acc_scr[1]
    pad = FOX_ACC_ROWS - HEAD_DIM
    o = jnp.concatenate([even[:HEAD_DIM] / even[HEAD_DIM : HEAD_DIM + 1], odd[pad:] / odd[pad - 1 : pad]], axis=0)
    o_ref[...] = o.T.astype(o_ref.dtype)


def _fox_attention(q, k, vt, stats):
    bsz, nh, s, _ = q.shape
    tq = FOX_TQ
    sub = tq // FOX_TK
    nq = s // tq
    npair = nh // 2
    st = stats[..., :4].transpose(0, 2, 1, 3)
    per_q = lambda v: v.reshape(bsz, nh, nq, sub)
    scalars = (
        per_q(st[..., 0]).max(axis=-1),
        lax.cummax(st[..., 1], axis=2),
        per_q(st[..., 2])[..., 0],
        st[..., 3],
    )
    scalars = tuple(v.reshape(-1) for v in scalars)
    return pl.pallas_call(
        _fox_kernel,
        grid_spec=pltpu.PrefetchScalarGridSpec(
            num_scalar_prefetch=len(scalars),
            grid=(bsz, npair, nq),
            in_specs=[
                pl.BlockSpec((1, 2, tq, LANES), lambda b, p, i, *_: (b, p, i, 0)),
                pl.BlockSpec((1, 2, s, LANES), lambda b, p, i, *_: (b, p, 0, 0)),
                pl.BlockSpec((1, 2, LANES, s), lambda b, p, i, *_: (b, p, 0, 0)),
            ],
            out_specs=pl.BlockSpec((tq, LANES), lambda b, p, i, *_: (b * nq + i, p)),
            scratch_shapes=[
                pltpu.VMEM((2, 1, tq), F32),
                pltpu.VMEM((2, FOX_ACC_ROWS, tq), F32),
                pltpu.VMEM((2, FOX_TK, tq), F32),
                pltpu.VMEM((2, 1, tq), F32),
            ],
        ),
        out_shape=jax.ShapeDtypeStruct((bsz * s, nh * HEAD_DIM), BF16),
        compiler_params=_params(("parallel", "parallel", "arbitrary")),
        name="fox_attention",
    )(*scalars, q, k, vt)


def _lru_kernel(xy_ref, cw_ref, cb_ref, wg_ref, bg_ref, lam_ref, o_ref, tail_scr, h_scr):
    w = LRU_WIDTH

    @pl.when(pl.program_id(1) == 0)
    def _():
        tail_scr[...] = jnp.zeros_like(tail_scr)
        h_scr[...] = jnp.zeros_like(h_scr)

    x = xy_ref[:, :w]
    y = xy_ref[:, w:]
    tail = tail_scr[...]
    row8 = lax.broadcasted_iota(jnp.int32, (SUBLANES, w), 0)
    conv = x * cw_ref[CONV_WIDTH - 1 : CONV_WIDTH, :] + cb_ref[...]
    for j in range(1, CONV_WIDTH):
        xs = pltpu.roll(x, j, 0)
        head = jnp.where(row8 < j, pltpu.roll(tail, j, 0), xs[:SUBLANES])
        xs = jnp.concatenate([head, xs[SUBLANES:]], axis=0)
        conv = conv + xs * cw_ref[CONV_WIDTH - 1 - j : CONV_WIDTH - j, :]
    tail_scr[...] = x[-SUBLANES:]

    gates = jax.nn.sigmoid(_dot(conv.astype(BF16), wg_ref[...]) + bg_ref[...])
    r = gates[:, :w]
    i = gates[:, w:]
    log_a = -LRU_C * r * jax.nn.softplus(-lam_ref[...])
    a = jnp.exp(log_a)
    th = jnp.tanh(log_a)
    u = jnp.sqrt(-2.0 * th / (1.0 - th)) * (i * conv)
    h = _scan_rows(a, u, h_scr[...])
    h_scr[...] = h[-1:, :]
    o_ref[...] = (h * jax.nn.gelu(y)).astype(o_ref.dtype)


def _rg_lru(xy, cw, cb, wg, bg, lam, bsz):
    m = xy.shape[0]
    ts = SCAN_TILE
    nt = m // bsz // ts
    w = LRU_WIDTH
    return pl.pallas_call(
        _lru_kernel,
        grid=(bsz, nt),
        in_specs=[
            pl.BlockSpec((ts, 2 * w), lambda b, i: (b * nt + i, 0)),
            _full(cw.shape), _full(cb.shape), _full(wg.shape), _full(bg.shape), _full(lam.shape),
        ],
        out_specs=pl.BlockSpec((ts, w), lambda b, i: (b * nt + i, 0)),
        out_shape=jax.ShapeDtypeStruct((m, w), BF16),
        scratch_shapes=[pltpu.VMEM((SUBLANES, w), F32), pltpu.VMEM((1, w), F32)],
        compiler_params=_params(("parallel", "arbitrary")),
        name="rg_lru",
    )(xy, cw, cb, wg, bg, lam)


def _rope_kernel(pos_ref, freq_ref, cos_ref, sin_ref):
    ang = pos_ref[...].astype(F32) * freq_ref[...]
    lane = lax.broadcasted_iota(jnp.int32, ang.shape, 1)
    first_half = (lane % HEAD_DIM) < HEAD_DIM // 2
    cos_ref[...] = jnp.cos(ang)
    s = jnp.sin(ang)
    sin_ref[...] = jnp.where(first_half, -s, s)


def _rope_tables(pos, freq):
    m = pos.shape[0]
    tm = ROW_TILE
    spec = pl.BlockSpec((tm, LANES), lambda i: (i, 0))
    return pl.pallas_call(
        _rope_kernel,
        grid=(m // tm,),
        in_specs=[spec, _full(freq.shape)],
        out_specs=[spec, spec],
        out_shape=[jax.ShapeDtypeStruct((m, LANES), F32)] * 2,
        compiler_params=_params(("parallel",)),
        name="rope_tables",
    )(pos, freq)


def _group_mean(z, avg):
    hi, mid, lo = _split3(z)
    return _dot(hi.astype(BF16), avg) + _dot(mid.astype(BF16), avg) + _dot(lo.astype(BF16), avg)


def _ret_kernel(x_ref, cos_ref, sin_ref, decay_ref, kw_ref, qw_ref, cd_ref, avg_ref, o_ref, state_scr):
    c = RET_CHUNK
    w = RET_WIDTH
    npair = RET_HEADS // 2

    @pl.when(pl.program_id(1) == 0)
    def _():
        state_scr[...] = jnp.zeros_like(state_scr)

    cos = cos_ref[...]
    sin = sin_ref[...]
    lane = lax.broadcasted_iota(jnp.int32, cos.shape, 1)
    first_half = (lane % HEAD_DIM) < HEAD_DIM // 2

    def rotary(z):
        swapped = jnp.where(first_half, pltpu.roll(z, LANES - HEAD_DIM // 2, 1), pltpu.roll(z, HEAD_DIM // 2, 1))
        return z * cos + swapped * sin

    ts = x_ref.shape[0]
    nchunk = ts // c
    row_chunk = lax.broadcasted_iota(jnp.int32, (ts, LANES), 0) // c
    ys = []
    for p in range(npair):
        cols = slice(p * LANES, (p + 1) * LANES)
        q = rotary(x_ref[:, cols])
        k = rotary(x_ref[:, w + p * LANES : w + (p + 1) * LANES]) * HEAD_DIM**-0.5
        v = x_ref[:, 2 * w + p * LANES : 2 * w + (p + 1) * LANES].astype(BF16)
        kb = k.astype(BF16)
        q_cross = (q * qw_ref[:, cols]).astype(BF16)
        k_state = (k * kw_ref[:, cols]).astype(BF16)
        zero = jnp.zeros_like(v)
        v_blocks = jnp.concatenate([jnp.where(row_chunk == n, v, zero) for n in range(nchunk)], axis=1)
        incs = _dot_tn(k_state, v_blocks)
        cd = cd_ref[p]
        states = [state_scr[p]]
        for n in range(nchunk):
            states.append(cd * states[-1] + jnp.where(cd > 0.0, incs[:, n * LANES : (n + 1) * LANES], 0.0))
        state_scr[p] = states[-1]
        q_blocks = jnp.concatenate([jnp.where(row_chunk == n, q_cross, zero) for n in range(nchunk)], axis=1)
        y = _dot(q_blocks, jnp.concatenate([st.astype(BF16) for st in states[:nchunk]], axis=0))
        span = RET_GROUP * c
        span_low = lax.broadcasted_iota(jnp.int32, (span, LANES), 1) < HEAD_DIM
        span_head = (span_low, jnp.logical_not(span_low))
        inner = []
        for r0 in range(0, ts, span):
            r = slice(r0, r0 + span)
            per_head = []
            for h in range(2):
                q_head = jnp.where(span_head[h], q[r], 0.0).astype(BF16)
                scores = _dot_nt(q_head, kb[r]) * decay_ref[2 * p + h]
                per_head.append(_dot(scores.astype(BF16), v[r]))
            inner.append(jnp.where(span_low, per_head[0], per_head[1]))
        ys.append(y + jnp.concatenate(inner, axis=0))
    y = jnp.concatenate(ys, axis=1)
    avg = avg_ref[...]
    mu = _group_mean(y, avg)
    d = y - mu
    var = _group_mean(d * d, avg)
    yn = d * lax.rsqrt(var + EPS)
    g = x_ref[:, 3 * w :]
    o_ref[...] = (g * jax.nn.sigmoid(g) * yn).astype(o_ref.dtype)


def _retention(ret, cos, sin, decay, kw, qw, cd, avg, bsz):
    m = ret.shape[0]
    ts = ROW_TILE
    nt = m // bsz // ts
    w = RET_WIDTH
    return pl.pallas_call(
        _ret_kernel,
        grid=(bsz, nt),
        in_specs=[
            pl.BlockSpec((ts, 4 * w), lambda b, i: (b * nt + i, 0)),
            pl.BlockSpec((ts, LANES), lambda b, i: (b * nt + i, 0)),
            pl.BlockSpec((ts, LANES), lambda b, i: (b * nt + i, 0)),
            _full(decay.shape), _full(kw.shape), _full(qw.shape), _full(cd.shape), _full(avg.shape),
        ],
        out_specs=pl.BlockSpec((ts, w), lambda b, i: (b * nt + i, 0)),
        out_shape=jax.ShapeDtypeStruct((m, w), BF16),
        scratch_shapes=[pltpu.VMEM((RET_HEADS // 2, LANES, LANES), F32)],
        compiler_params=_params(("parallel", "arbitrary")),
        name="retention",
    )(ret, cos, sin, decay, kw, qw, cd, avg)


def _norm_matmul_kernel(x_ref, g_ref, w_ref, o_ref):
    o_ref[...] = _dot(_rms(x_ref[...], g_ref[...]).astype(BF16), w_ref[...]).astype(o_ref.dtype)


def _norm_matmul(x, g, w, tm):
    m, d = x.shape
    n = w.shape[1]
    return pl.pallas_call(
        _norm_matmul_kernel,
        grid=(m // tm,),
        in_specs=[pl.BlockSpec((tm, d), lambda i: (i, 0)), _full(g.shape), _full(w.shape)],
        out_specs=pl.BlockSpec((tm, n), lambda i: (i, 0)),
        out_shape=jax.ShapeDtypeStruct((m, n), BF16),
        compiler_params=_params(("parallel",)),
        name="memory_kv_proj",
    )(x, g, w)


def _tail_kernel(x_ref, lru_ref, fox_ref, ret_ref, wm_ref, gm_ref, g1_ref, wq_ref, kt_ref, v_ref, wo_ref, g2_ref,
                 g3_ref, wg_ref, wu_ref, wd_ref, g4_ref, o_ref):
    d = x_ref.shape[1]
    hd = d // CROSS_HEADS
    a, b = LRU_WIDTH, LRU_WIDTH + FOX_WIDTH
    dff = wg_ref.shape[1]
    bounds = [(lo, min(lo + FFN_CHUNK, dff)) for lo in range(0, dff, FFN_CHUNK)]
    head_cols = [slice(h * hd, (h + 1) * hd) for h in range(CROSS_HEADS)]

    def phases(rows):
        mix = _dot(lru_ref[rows, :], wm_ref[:a, :]) + _dot(fox_ref[rows, :], wm_ref[a:b, :]) + _dot(ret_ref[rows, :], wm_ref[b:, :])
        yield
        x = x_ref[rows, :] + _rms(mix, gm_ref[...])
        h = _rms(x, g1_ref[...]).astype(BF16)
        yield
        q = _dot(h, wq_ref[...]).astype(BF16)
        yield
        logits = [_dot(q[:, cols], kt_ref[0, cols, :]) * hd**-0.5 for cols in head_cols]
        yield
        outs = []
        for cols, s in zip(head_cols, logits):
            e = jnp.exp(s - jnp.max(s, axis=-1, keepdims=True))
            p = e / jnp.sum(e, axis=-1, keepdims=True)
            outs.append(_dot(p.astype(BF16), v_ref[0, :, cols]).astype(BF16))
        yield
        attn = _dot(jnp.concatenate(outs, axis=1), wo_ref[...])
        yield
        x = x + _rms(attn, g2_ref[...])
        h = _rms(x, g3_ref[...]).astype(BF16)
        yield
        y = None
        for lo, hi in bounds:
            gate = _dot(h, wg_ref[:, lo:hi])
            up = _dot(h, wu_ref[:, lo:hi])
            yield
            act = (gate * jax.nn.sigmoid(gate) * up).astype(BF16)
            part = _dot(act, wd_ref[lo:hi, :])
            y = part if y is None else y + part
            yield
        o_ref[rows, :] = x + _rms(y, g4_ref[...])

    tm = x_ref.shape[0]
    step = tm // TAIL_SPLIT
    gens = [phases(slice(g * step, (g + 1) * step)) for g in range(TAIL_SPLIT)]
    live = list(gens)
    started = 0
    while live:
        if started < len(gens):
            started += 1
        for gen in list(gens[:started]):
            if gen in live:
                try:
                    next(gen)
                except StopIteration:
                    live.remove(gen)


def _layer_tail(x, lru, fox, ret, wm, gm, g1, wq, kt, v, wo, g2, g3, wg, wu, wd, g4, bsz):
    m, d = x.shape
    tm = ROW_TILE
    nt = m // bsz // tm
    mem_len = v.shape[1]
    row = lambda n: pl.BlockSpec((tm, n), lambda i: (i, 0))
    resident = lambda arr: pl.BlockSpec(arr.shape, lambda i: (0,) * arr.ndim, pipeline_mode=pl.Buffered(1))
    return pl.pallas_call(
        _tail_kernel,
        grid=(m // tm,),
        in_specs=[
            row(d), row(lru.shape[1]), row(fox.shape[1]), row(ret.shape[1]),
            resident(wm), resident(gm), resident(g1), resident(wq),
            pl.BlockSpec((1, d, mem_len), lambda i: (i // nt, 0, 0)),
            pl.BlockSpec((1, mem_len, d), lambda i: (i // nt, 0, 0)),
            resident(wo), resident(g2), resident(g3), resident(wg), resident(wu), resident(wd), resident(g4),
        ],
        out_specs=row(d),
        out_shape=jax.ShapeDtypeStruct((m, d), F32),
        compiler_params=pltpu.CompilerParams(dimension_semantics=("parallel",), vmem_limit_bytes=TAIL_VMEM_LIMIT),
        name="layer_tail",
    )(x, lru, fox, ret, wm, gm, g1, wq, kt, v, wo, g2, g3, wg, wu, wd, g4)


def _block_diag(w):
    n, a, b = w.shape
    eye = jnp.eye(n, dtype=w.dtype)
    return (eye[:, None, :, None] * w[:, :, None, :]).reshape(n * a, n * b)


def _retention_tables():
    c = RET_CHUNK
    f32 = np.float32
    log_gamma = np.log1p(-np.exp2(-5.0 - np.arange(RET_HEADS, dtype=f32))).astype(f32)
    idx = np.arange(c, dtype=f32)
    diff = idx[:, None] - idx[None, :]
    decay = np.where(diff >= 0, np.exp(log_gamma[:, None, None] * np.maximum(diff, f32(0))), f32(0)).astype(f32)
    decay = np.stack([np.kron(np.eye(RET_GROUP, dtype=f32), decay[h]) for h in range(RET_HEADS)])
    k_w = np.exp(log_gamma[None, :] * (c - 1.0 - idx)[:, None]).astype(f32)
    q_w = np.exp(log_gamma[None, :] * (idx + 1.0)[:, None]).astype(f32)
    chunk_decay = np.exp(log_gamma * f32(c)).astype(f32)
    reps = ROW_TILE // c
    kw = np.tile(np.repeat(k_w, HEAD_DIM, axis=1), (reps, 1))
    qw = np.tile(np.repeat(q_w, HEAD_DIM, axis=1), (reps, 1))
    ones = np.ones((HEAD_DIM, HEAD_DIM), f32)
    cd = np.stack([
        np.kron(np.diag(chunk_decay[2 * p : 2 * p + 2]), ones) for p in range(RET_HEADS // 2)
    ])
    avg = np.kron(np.eye(RET_HEADS, dtype=f32), np.full((HEAD_DIM, HEAD_DIM), 1.0 / HEAD_DIM, f32))
    return jnp.asarray(decay), jnp.asarray(kw), jnp.asarray(qw), jnp.asarray(cd), jnp.asarray(avg, dtype=BF16)


def kernel(x, mem, positions, pre_mix_g, post_mix_g, w_in, conv_w, conv_b, w_rg, b_rg, w_ig, b_ig, lru_lambda, fox_b_f, w_out, pre_cross_g, post_cross_g, mem_norm_g, w_cq, w_ck, w_cv, w_co, pre_ffn_g, post_ffn_g, w_gate, w_up, w_down):
    bsz, seq, d = x.shape
    depth = w_in.shape[0]
    m = bsz * seq
    mem_len = mem.shape[1]
    xf = x.reshape(m, d)
    row = lambda v: v.reshape(1, -1)

    half = HEAD_DIM // 2
    inv_freq = ROPE_THETA ** (-jnp.arange(half, dtype=F32) / half)
    freq = jnp.tile(inv_freq, LANES // half).reshape(1, LANES)
    pos = jnp.broadcast_to(positions.reshape(m, 1), (m, LANES))
    cos, sin = _rope_tables(pos, freq)
    decay, kw, qw, cd, avg = _retention_tables()

    o_fox = 2 * LRU_WIDTH
    o_ff = o_fox + 3 * FOX_WIDTH
    o_ret = o_ff + FOX_HEADS

    for l in range(depth):
        order = jnp.argsort(fox_b_f[l])
        by_head = lambda w: w.reshape(d, FOX_HEADS, HEAD_DIM)[:, order].reshape(d, FOX_WIDTH)
        wl = w_in[l, :, :o_fox].astype(BF16)
        wf = jnp.concatenate(
            [by_head(w_in[l, :, o_fox + i * FOX_WIDTH : o_fox + (i + 1) * FOX_WIDTH]) for i in range(3)], axis=1
        ).astype(BF16)
        wff = jnp.pad(w_in[l, :, o_ff:o_ret][:, order], ((0, 0), (0, LANES - FOX_HEADS))).astype(BF16)
        wr = w_in[l, :, o_ret:].astype(BF16)
        bf = jnp.pad(fox_b_f[l][order], (0, LANES - FOX_HEADS)).reshape(1, LANES)
        fox_rows = w_out[l, LRU_WIDTH : LRU_WIDTH + FOX_WIDTH].reshape(FOX_HEADS, HEAD_DIM, d)[order].reshape(FOX_WIDTH, d)
        wm = jnp.concatenate([w_out[l, :LRU_WIDTH], fox_rows, w_out[l, LRU_WIDTH + FOX_WIDTH :]], axis=0).astype(BF16)
        lru, ret, *fox_operands = _inproj(xf, row(pre_mix_g[l]), wl, wf, wr, wff, bf, bsz)
        fox_o = _fox_attention(*fox_operands)

        wg = jnp.concatenate([_block_diag(w_rg[l]), _block_diag(w_ig[l])], axis=1).astype(BF16)
        bg = jnp.concatenate([b_rg[l], b_ig[l]]).reshape(1, -1)
        lru_o = _rg_lru(lru, conv_w[l], row(conv_b[l]), wg, bg, row(lru_lambda[l]), bsz)

        ret_o = _retention(ret, cos, sin, decay, kw, qw, cd, avg, bsz)

        wkv = jnp.concatenate([w_ck[l], w_cv[l]], axis=1).astype(BF16)
        kv = _norm_matmul(mem.reshape(bsz * mem_len, d), row(mem_norm_g), wkv, mem_len)
        kt = kv[:, :d].reshape(bsz, mem_len, d).transpose(0, 2, 1)
        vv = kv[:, d:].reshape(bsz, mem_len, d)
        xf = _layer_tail(
            xf, lru_o, fox_o, ret_o, wm, row(post_mix_g[l]),
            row(pre_cross_g[l]), w_cq[l].astype(BF16), kt, vv, w_co[l].astype(BF16), row(post_cross_g[l]),
            row(pre_ffn_g[l]), w_gate[l].astype(BF16), w_up[l].astype(BF16), w_down[l].astype(BF16), row(post_ffn_g[l]), bsz)
    return xf.reshape(bsz, seq, d)
```

```python
import jax
import jax.numpy as jnp
import numpy as np
from jax import lax
from jax.experimental import pallas as pl
from jax.experimental.pallas import tpu as pltpu

F32 = jnp.float32
BF16 = jnp.bfloat16

HEAD_DIM = 64
LRU_WIDTH = 256
CONV_WIDTH = 4
LRU_C = 8.0
FOX_HEADS = 8
FOX_WIDTH = FOX_HEADS * HEAD_DIM
RET_HEADS = 4
RET_WIDTH = RET_HEADS * HEAD_DIM
CROSS_HEADS = 4
RET_CHUNK = 128
RET_GROUP = 2
ROPE_THETA = 10000.0
EPS = 1e-6

LANES = 128
SUBLANES = 8
VMEM_LIMIT = 48 * 1024 * 1024
TAIL_VMEM_LIMIT = 56 * 1024 * 1024

ROW_TILE = 512
FOX_TQ = 1024
FOX_TK = 512
FOX_PARTS = 2
FOX_ACC_ROWS = 80
SCAN_TILE = 1024
ROPE_TILE = 2048
FFN_CHUNK = 1024
TAIL_SPLIT = 2
NEG_BIG = -1e30
LOG2E = 1.4426950408889634
SKIP_LOG2 = 152.0
SKIP_REL = 2.0**-8
SKIP_REL_C = 2.0**-18
NORM_SAFETY = 1.0 + 2.0**-7


def _params(sem):
    return pltpu.CompilerParams(dimension_semantics=sem, vmem_limit_bytes=VMEM_LIMIT)


def _rms(x, g):
    ms = jnp.mean(x * x, axis=-1, keepdims=True)
    return x * lax.rsqrt(ms + EPS) * g


def _dot(a, b):
    return jnp.dot(a, b, preferred_element_type=F32)


def _dot_nt(a, b):
    return lax.dot_general(a, b, (((1,), (1,)), ((), ())), preferred_element_type=F32)


def _dot_tn(a, b):
    return lax.dot_general(a, b, (((0,), (0,)), ((), ())), preferred_element_type=F32)


def _full(shape):
    nd = len(shape)
    return pl.BlockSpec(shape, lambda *_: (0,) * nd)


def _scan_rows(a, b, carry):
    n = b.shape[0]
    sub = lax.broadcasted_iota(jnp.int32, b.shape, 0) % SUBLANES
    k = 1
    while k < SUBLANES:
        keep = sub >= k
        b_prev = jnp.where(keep, pltpu.roll(b, k, 0), 0.0)
        if a is None:
            b = b + b_prev
        else:
            a_prev = jnp.where(keep, pltpu.roll(a, k, 0), 1.0)
            b = a * b_prev + b
            a = a * a_prev
        k *= 2
    groups = []
    for g in range(n // SUBLANES):
        rows = slice(g * SUBLANES, (g + 1) * SUBLANES)
        h = b[rows] + carry if a is None else b[rows] + a[rows] * carry
        carry = h[SUBLANES - 1 :, :]
        groups.append(h)
    return jnp.concatenate(groups, axis=0)


def _split3(z):
    hi = z.astype(BF16).astype(F32)
    mid = (z - hi).astype(BF16).astype(F32)
    lo = (z - hi - mid).astype(BF16).astype(F32)
    return hi, mid, lo


def _inproj_kernel(x_ref, g_ref, wl_ref, wf_ref, wr_ref, wff_ref, bf_ref, place_ref,
                   lru_ref, ret_ref, q_ref, k_ref, vt_ref, stats_ref, carry_ref):
    ts = x_ref.shape[0]

    @pl.when(pl.program_id(1) == 0)
    def _():
        carry_ref[...] = jnp.zeros_like(carry_ref)

    h_in = _rms(x_ref[...], g_ref[...]).astype(BF16)
    log_f = jax.nn.log_sigmoid(_dot(h_in, wff_ref[...]) + bf_ref[...])
    c_nat = _scan_rows(None, log_f, carry_ref[...])
    carry_ref[...] = c_nat[-1:, :]
    q_all = (_dot(h_in, wf_ref[:, :FOX_WIDTH]) * (HEAD_DIM**-0.5 * LOG2E)).astype(BF16)
    k_all = _dot(h_in, wf_ref[:, FOX_WIDTH : 2 * FOX_WIDTH]).astype(BF16)
    v_all = _dot(h_in, wf_ref[:, 2 * FOX_WIDTH :]).astype(BF16)
    ccol = c_nat * LOG2E
    terms = jnp.concatenate(_split3(ccol), axis=1).astype(BF16)
    placed = _dot(terms, place_ref[...])

    lane = lax.broadcasted_iota(jnp.int32, (ts, LANES), 1)
    stat_lane = lax.broadcasted_iota(jnp.int32, (1, LANES), 1)
    same_head = (lax.broadcasted_iota(jnp.int32, (LANES, LANES), 0) < HEAD_DIM) == (
        lax.broadcasted_iota(jnp.int32, (LANES, LANES), 1) < HEAD_DIM)
    head_ones = jnp.where(same_head, 1.0, 0.0).astype(BF16)
    stats = []
    for pair in range(FOX_HEADS // 2):
        cols = slice(pair * LANES, (pair + 1) * LANES)
        gq, gk, gv = q_all[:, cols], k_all[:, cols], v_all[:, cols]
        norms = []
        for g in (gq, gk):
            g32 = g.astype(F32)
            sq = _dot((g32 * g32).astype(BF16), head_ones)
            norms.append(jnp.sqrt(jnp.max(sq, axis=0, keepdims=True)) * NORM_SAFETY)
        for e in range(2):
            h = 2 * pair + e
            data = (lane < HEAD_DIM) if e == 0 else (lane >= HEAD_DIM)
            el = lane - (HEAD_DIM if e == 0 else 0)
            extras = placed[:, h * LANES : (h + 1) * LANES]
            q_extra = jnp.where((el >= 3) & (el < 6), 1.0, extras)
            k_extra = jnp.where((el >= 0) & (el < 3), 1.0, extras)
            v_extra = jnp.where(el == (0 if e == 0 else HEAD_DIM - 1), 1.0, 0.0)
            q_ref[0, h] = jnp.where(data, gq, q_extra.astype(BF16))
            k_ref[0, h] = jnp.where(data, gk, k_extra.astype(BF16))
            vt_ref[0, h] = jnp.where(data, gv, v_extra.astype(BF16)).T
            first = e * HEAD_DIM
            q_norm = norms[0][:, first : first + 1]
            k_norm = norms[1][:, first : first + 1]
            c_first = ccol[0:1, h : h + 1]
            c_last = ccol[ts - 1 : ts, h : h + 1]
            stats.append(jnp.where(stat_lane == 0, q_norm, jnp.where(stat_lane == 1, k_norm, jnp.where(stat_lane == 2, c_first, c_last))))
    stats_ref[0, 0] = jnp.concatenate(stats, axis=0)
    lru_ref[...] = _dot(h_in, wl_ref[...])
    ret_ref[...] = _dot(h_in, wr_ref[...])


def _fox_placement():
    place = np.zeros((3 * LANES, FOX_HEADS * LANES), np.float32)
    for h in range(FOX_HEADS):
        first = h * LANES + (HEAD_DIM if h % 2 == 0 else 0)
        for t in range(3):
            place[t * LANES + h, first + t] = 1.0
            place[t * LANES + h, first + 3 + t] = -1.0
    return jnp.asarray(place, dtype=BF16)


def _inproj(x, g, wl, wf, wr, wff, bf, bsz):
    m, d = x.shape
    s = m // bsz
    ts = FOX_TK
    nt = s // ts
    place = _fox_placement()
    row = lambda n: pl.BlockSpec((ts, n), lambda b, i: (b * nt + i, 0))
    per_head = pl.BlockSpec((1, FOX_HEADS, ts, LANES), lambda b, i: (b, 0, i, 0))
    return pl.pallas_call(
        _inproj_kernel,
        grid=(bsz, nt),
        in_specs=[row(d), _full(g.shape), _full(wl.shape), _full(wf.shape), _full(wr.shape), _full(wff.shape),
                  _full(bf.shape), _full(place.shape)],
        out_specs=[
            row(wl.shape[1]), row(wr.shape[1]), per_head, per_head,
            pl.BlockSpec((1, FOX_HEADS, LANES, ts), lambda b, i: (b, 0, 0, i)),
            pl.BlockSpec((1, 1, FOX_HEADS, LANES), lambda b, i: (b, i, 0, 0)),
        ],
        out_shape=[
            jax.ShapeDtypeStruct((m, wl.shape[1]), F32),
            jax.ShapeDtypeStruct((m, wr.shape[1]), F32),
            jax.ShapeDtypeStruct((bsz, FOX_HEADS, s, LANES), BF16),
            jax.ShapeDtypeStruct((bsz, FOX_HEADS, s, LANES), BF16),
            jax.ShapeDtypeStruct((bsz, FOX_HEADS, LANES, s), BF16),
            jax.ShapeDtypeStruct((bsz, nt, FOX_HEADS, LANES), F32),
        ],
        scratch_shapes=[pltpu.VMEM((1, LANES), F32)],
        compiler_params=_params(("parallel", "arbitrary")),
        name="mixer_inproj",
    )(x, g, wl, wf, wr, wff, bf, place)


def _fox_kernel(qn_ref, kn_ref, cf_ref, cl_ref, q_ref, k_ref, vt_ref, o_ref, m_scr, acc_scr, s_scr, mx_scr):
    tq = q_ref.shape[2]
    tk = FOX_TK
    sub = tq // tk
    nk = k_ref.shape[2] // tk
    nq = nk // sub
    qi = pl.program_id(2)
    v_rows = (slice(0, FOX_ACC_ROWS), slice(LANES - FOX_ACC_ROWS, LANES))
    parts = [(i * (tq // FOX_PARTS), tq // FOX_PARTS) for i in range(FOX_PARTS)]

    def first_needed(h):
        head = (pl.program_id(0) * (FOX_HEADS // 2) + pl.program_id(1)) * 2 + h
        q_norm = qn_ref[head * nq + qi]
        c_first = cf_ref[head * nq + qi]
        floor = -q_norm * kn_ref[head * nk + qi * sub + sub - 1]

        def needed(kb):
            j = head * nk + jnp.maximum(kb, 0)
            dot_bound = q_norm * kn_ref[j]
            slack = SKIP_REL * (dot_bound - floor) + SKIP_REL_C * (jnp.abs(c_first) + jnp.abs(cl_ref[j]))
            return dot_bound + c_first - cl_ref[j] + slack >= floor - SKIP_LOG2

        kb = lax.while_loop(lambda kb: jnp.logical_and(kb >= 0, needed(kb)), lambda kb: kb - 1, qi * sub - 1)
        return kb + 1

    def logits(h, kb, slot, diag, part=None):
        start = pl.multiple_of(kb * tk, tk)
        q0, qn = part if part is not None else ((0 if diag is None else diag * tk), None)
        lanes = slice(q0, None if qn is None else q0 + qn)
        s = _dot_nt(k_ref[0, h, pl.ds(start, tk), :], q_ref[0, h, lanes, :])
        if diag is not None:
            key = lax.broadcasted_iota(jnp.int32, s.shape, 0)
            qry = lax.broadcasted_iota(jnp.int32, s.shape, 1)
            s = jnp.where(key <= qry, s, NEG_BIG)
        s_scr[slot, :, lanes] = s
        mx_scr[slot, :, lanes] = jnp.max(s, axis=0, keepdims=True)

    def accumulate(h, kb, slot, diag=None, part=None):
        start = pl.multiple_of(kb * tk, tk)
        q0, qn = part if part is not None else ((0 if diag is None else diag * tk), None)
        lanes = slice(q0, None if qn is None else q0 + qn)
        m_old = m_scr[h, :, lanes]
        m_new = jnp.maximum(m_old, mx_scr[slot, :, lanes])
        m_scr[h, :, lanes] = m_new
        p = jnp.exp2(s_scr[slot, :, lanes] - m_new).astype(BF16)
        vt = vt_ref[0, h, v_rows[h], pl.ds(start, tk)]
        acc_scr[h, :, lanes] = jnp.exp2(m_old - m_new) * acc_scr[h, :, lanes] + _dot(vt, p)

    m_scr[...] = jnp.full_like(m_scr, NEG_BIG)
    acc_scr[...] = jnp.zeros_like(acc_scr)
    own = qi * sub
    lo = jnp.minimum(first_needed(0), first_needed(1))
    logits(0, own, 0, 0)
    logits(1, own, 1, 0)
    accumulate(0, own, 0, 0)
    for u in range(1, sub):
        logits(0, own + u, 0, u)
        accumulate(1, own + u - 1, 1, u - 1)
        logits(1, own + u, 1, u)
        accumulate(0, own + u, 0, u)
    last_own = own + sub - 1

    @pl.when(lo >= own)
    def _():
        accumulate(1, last_own, 1, sub - 1)

    @pl.when(lo < own)
    def _():
        logits(0, lo, 0, None)
        accumulate(1, last_own, 1, sub - 1)

        def body(kb, carry):
            for part in parts:
                logits(1, kb, 1, None, part)
                accumulate(0, kb, 0, None, part)
            for part in parts:
                logits(0, kb + 1, 0, None, part)
                accumulate(1, kb, 1, None, part)
            return carry

        lax.fori_loop(lo, own - 1, body, 0)
        logits(1, own - 1, 1, None)
        accumulate(0, own - 1, 0)
        accumulate(1, own - 1, 1)

    even, odd = acc_scr[0], acc_scr[1]
    pad = FOX_ACC_ROWS - HEAD_DIM
    o = jnp.concatenate([even[:HEAD_DIM] / even[HEAD_DIM : HEAD_DIM + 1], odd[pad:] / odd[pad - 1 : pad]], axis=0)
    o_ref[...] = o.T.astype(o_ref.dtype)


def _fox_attention(q, k, vt, stats):
    bsz, nh, s, _ = q.shape
    tq = FOX_TQ
    sub = tq // FOX_TK
    nq = s // tq
    npair = nh // 2
    st = stats[..., :4].transpose(0, 2, 1, 3)
    per_q = lambda v: v.reshape(bsz, nh, nq, sub)
    scalars = (
        per_q(st[..., 0]).max(axis=-1),
        lax.cummax(st[..., 1], axis=2),
        per_q(st[..., 2])[..., 0],
        st[..., 3],
    )
    scalars = tuple(v.reshape(-1) for v in scalars)
    return pl.pallas_call(
        _fox_kernel,
        grid_spec=pltpu.PrefetchScalarGridSpec(
            num_scalar_prefetch=len(scalars),
            grid=(bsz, npair, nq),
            in_specs=[
                pl.BlockSpec((1, 2, tq, LANES), lambda b, p, i, *_: (b, p, i, 0)),
                pl.BlockSpec((1, 2, s, LANES), lambda b, p, i, *_: (b, p, 0, 0)),
                pl.BlockSpec((1, 2, LANES, s), lambda b, p, i, *_: (b, p, 0, 0)),
            ],
            out_specs=pl.BlockSpec((tq, LANES), lambda b, p, i, *_: (b * nq + i, p)),
            scratch_shapes=[
                pltpu.VMEM((2, 1, tq), F32),
                pltpu.VMEM((2, FOX_ACC_ROWS, tq), F32),
                pltpu.VMEM((2, FOX_TK, tq), F32),
                pltpu.VMEM((2, 1, tq), F32),
            ],
        ),
        out_shape=jax.ShapeDtypeStruct((bsz * s, nh * HEAD_DIM), BF16),
        compiler_params=_params(("parallel", "parallel", "arbitrary")),
        name="fox_attention",
    )(*scalars, q, k, vt)


def _lru_kernel(xy_ref, cw_ref, cb_ref, wg_ref, bg_ref, lam_ref, o_ref, tail_scr, h_scr):
    w = LRU_WIDTH

    @pl.when(pl.program_id(1) == 0)
    def _():
        tail_scr[...] = jnp.zeros_like(tail_scr)
        h_scr[...] = jnp.zeros_like(h_scr)

    x = xy_ref[:, :w]
    y = xy_ref[:, w:]
    tail = tail_scr[...]
    row8 = lax.broadcasted_iota(jnp.int32, (SUBLANES, w), 0)
    conv = x * cw_ref[CONV_WIDTH - 1 : CONV_WIDTH, :] + cb_ref[...]
    for j in range(1, CONV_WIDTH):
        xs = pltpu.roll(x, j, 0)
        head = jnp.where(row8 < j, pltpu.roll(tail, j, 0), xs[:SUBLANES])
        xs = jnp.concatenate([head, xs[SUBLANES:]], axis=0)
        conv = conv + xs * cw_ref[CONV_WIDTH - 1 - j : CONV_WIDTH - j, :]
    tail_scr[...] = x[-SUBLANES:]

    gates = jax.nn.sigmoid(_dot(conv.astype(BF16), wg_ref[...]) + bg_ref[...])
    r = gates[:, :w]
    i = gates[:, w:]
    log_a = -LRU_C * r * jax.nn.softplus(-lam_ref[...])
    a = jnp.exp(log_a)
    th = jnp.tanh(log_a)
    u = jnp.sqrt(-2.0 * th / (1.0 - th)) * (i * conv)
    h = _scan_rows(a, u, h_scr[...])
    h_scr[...] = h[-1:, :]
    o_ref[...] = (h * jax.nn.gelu(y)).astype(o_ref.dtype)


def _rg_lru(xy, cw, cb, wg, bg, lam, bsz):
    m = xy.shape[0]
    ts = SCAN_TILE
    nt = m // bsz // ts
    w = LRU_WIDTH
    return pl.pallas_call(
        _lru_kernel,
        grid=(bsz, nt),
        in_specs=[
            pl.BlockSpec((ts, 2 * w), lambda b, i: (b * nt + i, 0)),
            _full(cw.shape), _full(cb.shape), _full(wg.shape), _full(bg.shape), _full(lam.shape),
        ],
        out_specs=pl.BlockSpec((ts, w), lambda b, i: (b * nt + i, 0)),
        out_shape=jax.ShapeDtypeStruct((m, w), BF16),
        scratch_shapes=[pltpu.VMEM((SUBLANES, w), F32), pltpu.VMEM((1, w), F32)],
        compiler_params=_params(("parallel", "arbitrary")),
        name="rg_lru",
    )(xy, cw, cb, wg, bg, lam)


def _rope_kernel(pos_ref, freq_ref, cos_ref, sin_ref):
    ang = pos_ref[...].astype(F32) * freq_ref[...]
    lane = lax.broadcasted_iota(jnp.int32, ang.shape, 1)
    first_half = (lane % HEAD_DIM) < HEAD_DIM // 2
    cos_ref[...] = jnp.cos(ang)
    s = jnp.sin(ang)
    sin_ref[...] = jnp.where(first_half, -s, s)


def _rope_tables(pos, freq):
    m = pos.shape[0]
    tm = ROPE_TILE
    spec = pl.BlockSpec((tm, LANES), lambda i: (i, 0))
    return pl.pallas_call(
        _rope_kernel,
        grid=(m // tm,),
        in_specs=[spec, _full(freq.shape)],
        out_specs=[spec, spec],
        out_shape=[jax.ShapeDtypeStruct((m, LANES), F32)] * 2,
        compiler_params=_params(("parallel",)),
        name="rope_tables",
    )(pos, freq)


def _group_mean(z, avg):
    hi, mid, lo = _split3(z)
    return _dot(hi.astype(BF16), avg) + _dot(mid.astype(BF16), avg) + _dot(lo.astype(BF16), avg)


def _ret_kernel(x_ref, cos_ref, sin_ref, decay_ref, kw_ref, qw_ref, cd_ref, avg_ref, o_ref, state_scr):
    c = RET_CHUNK
    w = RET_WIDTH
    npair = RET_HEADS // 2

    @pl.when(pl.program_id(1) == 0)
    def _():
        state_scr[...] = jnp.zeros_like(state_scr)

    cos = cos_ref[...]
    sin = sin_ref[...]
    lane = lax.broadcasted_iota(jnp.int32, cos.shape, 1)
    first_half = (lane % HEAD_DIM) < HEAD_DIM // 2

    def rotary(z):
        swapped = jnp.where(first_half, pltpu.roll(z, LANES - HEAD_DIM // 2, 1), pltpu.roll(z, HEAD_DIM // 2, 1))
        return z * cos + swapped * sin

    ts = x_ref.shape[0]
    nchunk = ts // c
    row_chunk = lax.broadcasted_iota(jnp.int32, (ts, LANES), 0) // c
    ys = []
    for p in range(npair):
        cols = slice(p * LANES, (p + 1) * LANES)
        q = rotary(x_ref[:, cols])
        k = rotary(x_ref[:, w + p * LANES : w + (p + 1) * LANES]) * HEAD_DIM**-0.5
        v = x_ref[:, 2 * w + p * LANES : 2 * w + (p + 1) * LANES].astype(BF16)
        kb = k.astype(BF16)
        q_cross = (q * qw_ref[:, cols]).astype(BF16)
        k_state = (k * kw_ref[:, cols]).astype(BF16)
        zero = jnp.zeros_like(v)
        v_blocks = jnp.concatenate([jnp.where(row_chunk == n, v, zero) for n in range(nchunk)], axis=1)
        incs = _dot_tn(k_state, v_blocks)
        cd = cd_ref[p]
        states = [state_scr[p]]
        for n in range(nchunk):
            states.append(cd * states[-1] + jnp.where(cd > 0.0, incs[:, n * LANES : (n + 1) * LANES], 0.0))
        state_scr[p] = states[-1]
        q_blocks = jnp.concatenate([jnp.where(row_chunk == n, q_cross, zero) for n in range(nchunk)], axis=1)
        y = _dot(q_blocks, jnp.concatenate([st.astype(BF16) for st in states[:nchunk]], axis=0))
        span = RET_GROUP * c
        span_low = lax.broadcasted_iota(jnp.int32, (span, LANES), 1) < HEAD_DIM
        span_head = (span_low, jnp.logical_not(span_low))
        inner = []
        for r0 in range(0, ts, span):
            r = slice(r0, r0 + span)
            per_head = []
            for h in range(2):
                q_head = jnp.where(span_head[h], q[r], 0.0).astype(BF16)
                scores = _dot_nt(q_head, kb[r]) * decay_ref[2 * p + h]
                per_head.append(_dot(scores.astype(BF16), v[r]))
            inner.append(jnp.where(span_low, per_head[0], per_head[1]))
        ys.append(y + jnp.concatenate(inner, axis=0))
    y = jnp.concatenate(ys, axis=1)
    avg = avg_ref[...]
    mu = _group_mean(y, avg)
    d = y - mu
    var = _group_mean(d * d, avg)
    yn = d * lax.rsqrt(var + EPS)
    g = x_ref[:, 3 * w :]
    o_ref[...] = (g * jax.nn.sigmoid(g) * yn).astype(o_ref.dtype)


def _retention(ret, cos, sin, decay, kw, qw, cd, avg, bsz):
    m = ret.shape[0]
    ts = ROW_TILE
    nt = m // bsz // ts
    w = RET_WIDTH
    return pl.pallas_call(
        _ret_kernel,
        grid=(bsz, nt),
        in_specs=[
            pl.BlockSpec((ts, 4 * w), lambda b, i: (b * nt + i, 0)),
            pl.BlockSpec((ts, LANES), lambda b, i: (b * nt + i, 0)),
            pl.BlockSpec((ts, LANES), lambda b, i: (b * nt + i, 0)),
            _full(decay.shape), _full(kw.shape), _full(qw.shape), _full(cd.shape), _full(avg.shape),
        ],
        out_specs=pl.BlockSpec((ts, w), lambda b, i: (b * nt + i, 0)),
        out_shape=jax.ShapeDtypeStruct((m, w), BF16),
        scratch_shapes=[pltpu.VMEM((RET_HEADS // 2, LANES, LANES), F32)],
        compiler_params=_params(("parallel", "arbitrary")),
        name="retention",
    )(ret, cos, sin, decay, kw, qw, cd, avg)


def _norm_matmul_kernel(x_ref, g_ref, w_ref, o_ref):
    o_ref[...] = _dot(_rms(x_ref[...], g_ref[...]).astype(BF16), w_ref[...]).astype(o_ref.dtype)


def _norm_matmul(x, g, w, tm):
    m, d = x.shape
    n = w.shape[1]
    return pl.pallas_call(
        _norm_matmul_kernel,
        grid=(m // tm,),
        in_specs=[pl.BlockSpec((tm, d), lambda i: (i, 0)), _full(g.shape), _full(w.shape)],
        out_specs=pl.BlockSpec((tm, n), lambda i: (i, 0)),
        out_shape=jax.ShapeDtypeStruct((m, n), BF16),
        compiler_params=_params(("parallel",)),
        name="memory_kv_proj",
    )(x, g, w)


def _tail_kernel(x_ref, lru_ref, fox_ref, ret_ref, wm_ref, gm_ref, g1_ref, wq_ref, kt_ref, v_ref, wo_ref, g2_ref,
                 g3_ref, wg_ref, wu_ref, wd_ref, g4_ref, o_ref):
    d = x_ref.shape[1]
    hd = d // CROSS_HEADS
    a, b = LRU_WIDTH, LRU_WIDTH + FOX_WIDTH
    dff = wg_ref.shape[1]
    bounds = [(lo, min(lo + FFN_CHUNK, dff)) for lo in range(0, dff, FFN_CHUNK)]
    head_cols = [slice(h * hd, (h + 1) * hd) for h in range(CROSS_HEADS)]

    def phases(rows):
        mix = _dot(lru_ref[rows, :], wm_ref[:a, :]) + _dot(fox_ref[rows, :], wm_ref[a:b, :]) + _dot(ret_ref[rows, :], wm_ref[b:, :])
        yield
        x = x_ref[rows, :] + _rms(mix, gm_ref[...])
        h = _rms(x, g1_ref[...]).astype(BF16)
        yield
        q = _dot(h, wq_ref[...]).astype(BF16)
        yield
        logits = [_dot(q[:, cols], kt_ref[0, cols, :]) * hd**-0.5 for cols in head_cols]
        yield
        outs = []
        for cols, s in zip(head_cols, logits):
            e = jnp.exp(s - jnp.max(s, axis=-1, keepdims=True))
            p = e / jnp.sum(e, axis=-1, keepdims=True)
            outs.append(_dot(p.astype(BF16), v_ref[0, :, cols]).astype(BF16))
        yield
        attn = _dot(jnp.concatenate(outs, axis=1), wo_ref[...])
        yield
        x = x + _rms(attn, g2_ref[...])
        h = _rms(x, g3_ref[...]).astype(BF16)
        yield
        y = None
        for lo, hi in bounds:
            gate = _dot(h, wg_ref[:, lo:hi])
            up = _dot(h, wu_ref[:, lo:hi])
            yield
            act = (gate * jax.nn.sigmoid(gate) * up).astype(BF16)
            part = _dot(act, wd_ref[lo:hi, :])
            y = part if y is None else y + part
            yield
        o_ref[rows, :] = x + _rms(y, g4_ref[...])

    tm = x_ref.shape[0]
    step = tm // TAIL_SPLIT
    gens = [phases(slice(g * step, (g + 1) * step)) for g in range(TAIL_SPLIT)]
    live = list(gens)
    started = 0
    while live:
        if started < len(gens):
            started += 1
        for gen in list(gens[:started]):
            if gen in live:
                try:
                    next(gen)
                except StopIteration:
                    live.remove(gen)


def _layer_tail(x, lru, fox, ret, wm, gm, g1, wq, kt, v, wo, g2, g3, wg, wu, wd, g4, bsz):
    m, d = x.shape
    tm = ROW_TILE
    nt = m // bsz // tm
    mem_len = v.shape[1]
    row = lambda n: pl.BlockSpec((tm, n), lambda i: (i, 0))
    resident = lambda arr: pl.BlockSpec(arr.shape, lambda i: (0,) * arr.ndim, pipeline_mode=pl.Buffered(1))
    return pl.pallas_call(
        _tail_kernel,
        grid=(m // tm,),
        in_specs=[
            row(d), row(lru.shape[1]), row(fox.shape[1]), row(ret.shape[1]),
            resident(wm), resident(gm), resident(g1), resident(wq),
            pl.BlockSpec((1, d, mem_len), lambda i: (i // nt, 0, 0)),
            pl.BlockSpec((1, mem_len, d), lambda i: (i // nt, 0, 0)),
            resident(wo), resident(g2), resident(g3), resident(wg), resident(wu), resident(wd), resident(g4),
        ],
        out_specs=row(d),
        out_shape=jax.ShapeDtypeStruct((m, d), F32),
        compiler_params=pltpu.CompilerParams(dimension_semantics=("parallel",), vmem_limit_bytes=TAIL_VMEM_LIMIT),
        name="layer_tail",
    )(x, lru, fox, ret, wm, gm, g1, wq, kt, v, wo, g2, g3, wg, wu, wd, g4)


def _block_diag(w):
    n, a, b = w.shape
    eye = jnp.eye(n, dtype=w.dtype)
    return (eye[:, None, :, None] * w[:, :, None, :]).reshape(n * a, n * b)


def _retention_tables():
    c = RET_CHUNK
    f32 = np.float32
    log_gamma = np.log1p(-np.exp2(-5.0 - np.arange(RET_HEADS, dtype=f32))).astype(f32)
    idx = np.arange(c, dtype=f32)
    diff = idx[:, None] - idx[None, :]
    decay = np.where(diff >= 0, np.exp(log_gamma[:, None, None] * np.maximum(diff, f32(0))), f32(0)).astype(f32)
    decay = np.stack([np.kron(np.eye(RET_GROUP, dtype=f32), decay[h]) for h in range(RET_HEADS)])
    k_w = np.exp(log_gamma[None, :] * (c - 1.0 - idx)[:, None]).astype(f32)
    q_w = np.exp(log_gamma[None, :] * (idx + 1.0)[:, None]).astype(f32)
    chunk_decay = np.exp(log_gamma * f32(c)).astype(f32)
    reps = ROW_TILE // c
    kw = np.tile(np.repeat(k_w, HEAD_DIM, axis=1), (reps, 1))
    qw = np.tile(np.repeat(q_w, HEAD_DIM, axis=1), (reps, 1))
    ones = np.ones((HEAD_DIM, HEAD_DIM), f32)
    cd = np.stack([
        np.kron(np.diag(chunk_decay[2 * p : 2 * p + 2]), ones) for p in range(RET_HEADS // 2)
    ])
    avg = np.kron(np.eye(RET_HEADS, dtype=f32), np.full((HEAD_DIM, HEAD_DIM), 1.0 / HEAD_DIM, f32))
    return jnp.asarray(decay), jnp.asarray(kw), jnp.asarray(qw), jnp.asarray(cd), jnp.asarray(avg, dtype=BF16)


def kernel(x, mem, positions, pre_mix_g, post_mix_g, w_in, conv_w, conv_b, w_rg, b_rg, w_ig, b_ig, lru_lambda, fox_b_f, w_out, pre_cross_g, post_cross_g, mem_norm_g, w_cq, w_ck, w_cv, w_co, pre_ffn_g, post_ffn_g, w_gate, w_up, w_down):
    bsz, seq, d = x.shape
    depth = w_in.shape[0]
    m = bsz * seq
    mem_len = mem.shape[1]
    xf = x.reshape(m, d)
    row = lambda v: v.reshape(1, -1)

    half = HEAD_DIM // 2
    inv_freq = ROPE_THETA ** (-jnp.arange(half, dtype=F32) / half)
    freq = jnp.tile(inv_freq, LANES // half).reshape(1, LANES)
    pos = jnp.broadcast_to(positions.reshape(m, 1), (m, LANES))
    cos, sin = _rope_tables(pos, freq)
    decay, kw, qw, cd, avg = _retention_tables()

    o_fox = 2 * LRU_WIDTH
    o_ff = o_fox + 3 * FOX_WIDTH
    o_ret = o_ff + FOX_HEADS

    for l in range(depth):
        order = jnp.argsort(fox_b_f[l])
        by_head = lambda w: w.reshape(d, FOX_HEADS, HEAD_DIM)[:, order].reshape(d, FOX_WIDTH)
        wl = w_in[l, :, :o_fox].astype(BF16)
        wf = jnp.concatenate(
            [by_head(w_in[l, :, o_fox + i * FOX_WIDTH : o_fox + (i + 1) * FOX_WIDTH]) for i in range(3)], axis=1
        ).astype(BF16)
        wff = jnp.pad(w_in[l, :, o_ff:o_ret][:, order], ((0, 0), (0, LANES - FOX_HEADS))).astype(BF16)
        wr = w_in[l, :, o_ret:].astype(BF16)
        bf = jnp.pad(fox_b_f[l][order], (0, LANES - FOX_HEADS)).reshape(1, LANES)
        fox_rows = w_out[l, LRU_WIDTH : LRU_WIDTH + FOX_WIDTH].reshape(FOX_HEADS, HEAD_DIM, d)[order].reshape(FOX_WIDTH, d)
        wm = jnp.concatenate([w_out[l, :LRU_WIDTH], fox_rows, w_out[l, LRU_WIDTH + FOX_WIDTH :]], axis=0).astype(BF16)
        lru, ret, *fox_operands = _inproj(xf, row(pre_mix_g[l]), wl, wf, wr, wff, bf, bsz)
        fox_o = _fox_attention(*fox_operands)

        wg = jnp.concatenate([_block_diag(w_rg[l]), _block_diag(w_ig[l])], axis=1).astype(BF16)
        bg = jnp.concatenate([b_rg[l], b_ig[l]]).reshape(1, -1)
        lru_o = _rg_lru(lru, conv_w[l], row(conv_b[l]), wg, bg, row(lru_lambda[l]), bsz)

        ret_o = _retention(ret, cos, sin, decay, kw, qw, cd, avg, bsz)

        wkv = jnp.concatenate([w_ck[l], w_cv[l]], axis=1).astype(BF16)
        kv = _norm_matmul(mem.reshape(bsz * mem_len, d), row(mem_norm_g), wkv, mem_len)
        kt = kv[:, :d].reshape(bsz, mem_len, d).transpose(0, 2, 1)
        vv = kv[:, d:].reshape(bsz, mem_len, d)
        xf = _layer_tail(
            xf, lru_o, fox_o, ret_o, wm, row(post_mix_g[l]),
            row(pre_cross_g[l]), w_cq[l].astype(BF16), kt, vv, w_co[l].astype(BF16), row(post_cross_g[l]),
            row(pre_ffn_g[l]), w_gate[l].astype(BF16), w_up[l].astype(BF16), w_down[l].astype(BF16), row(post_ffn_g[l]), bsz)
    return xf.reshape(bsz, seq, d)
```

```python
import jax
import jax.numpy as jnp
import numpy as np
from jax import lax
from jax.experimental import pallas as pl
from jax.experimental.pallas import tpu as pltpu

F32 = jnp.float32
BF16 = jnp.bfloat16

HEAD_DIM = 64
LRU_WIDTH = 256
CONV_WIDTH = 4
LRU_C = 8.0
FOX_HEADS = 8
FOX_WIDTH = FOX_HEADS * HEAD_DIM
RET_HEADS = 4
RET_WIDTH = RET_HEADS * HEAD_DIM
CROSS_HEADS = 4
RET_CHUNK = 128
RET_GROUP = 2
ROPE_THETA = 10000.0
EPS = 1e-6

LANES = 128
SUBLANES = 8
VMEM_LIMIT = 48 * 1024 * 1024
TAIL_VMEM_LIMIT = 56 * 1024 * 1024

ROW_TILE = 512
FOX_TQ = 1024
FOX_TK = 512
FOX_PARTS = 2
FOX_ACC_ROWS = 80
FFN_CHUNK = 1024
TAIL_SPLIT = 2
NEG_BIG = -1e30
LOG2E = 1.4426950408889634
SKIP_LOG2 = 152.0
SKIP_REL = 2.0**-8
SKIP_REL_C = 2.0**-18
NORM_SAFETY = 1.0 + 2.0**-7


def _params(sem):
    return pltpu.CompilerParams(dimension_semantics=sem, vmem_limit_bytes=VMEM_LIMIT)


def _rms(x, g):
    ms = jnp.mean(x * x, axis=-1, keepdims=True)
    return x * lax.rsqrt(ms + EPS) * g


def _dot(a, b):
    return jnp.dot(a, b, preferred_element_type=F32)


def _dot_nt(a, b):
    return lax.dot_general(a, b, (((1,), (1,)), ((), ())), preferred_element_type=F32)


def _dot_tn(a, b):
    return lax.dot_general(a, b, (((0,), (0,)), ((), ())), preferred_element_type=F32)


def _full(shape):
    nd = len(shape)
    return pl.BlockSpec(shape, lambda *_: (0,) * nd)


def _scan_rows(a, b, carry):
    n = b.shape[0]
    sub = lax.broadcasted_iota(jnp.int32, b.shape, 0) % SUBLANES
    k = 1
    while k < SUBLANES:
        keep = sub >= k
        b_prev = jnp.where(keep, pltpu.roll(b, k, 0), 0.0)
        if a is None:
            b = b + b_prev
        else:
            a_prev = jnp.where(keep, pltpu.roll(a, k, 0), 1.0)
            b = a * b_prev + b
            a = a * a_prev
        k *= 2
    groups = []
    for g in range(n // SUBLANES):
        rows = slice(g * SUBLANES, (g + 1) * SUBLANES)
        h = b[rows] + carry if a is None else b[rows] + a[rows] * carry
        carry = h[SUBLANES - 1 :, :]
        groups.append(h)
    return jnp.concatenate(groups, axis=0)


def _split3(z):
    hi = z.astype(BF16).astype(F32)
    mid = (z - hi).astype(BF16).astype(F32)
    lo = (z - hi - mid).astype(BF16).astype(F32)
    return hi, mid, lo


def _inproj_kernel(x_ref, g_ref, wl_ref, wf_ref, wr_ref, wff_ref, bf_ref, place_ref,
                   lru_ref, ret_ref, q_ref, k_ref, vt_ref, stats_ref, carry_ref):
    ts = x_ref.shape[0]

    @pl.when(pl.program_id(1) == 0)
    def _():
        carry_ref[...] = jnp.zeros_like(carry_ref)

    h_in = _rms(x_ref[...], g_ref[...]).astype(BF16)
    log_f = jax.nn.log_sigmoid(_dot(h_in, wff_ref[...]) + bf_ref[...])
    c_nat = _scan_rows(None, log_f, carry_ref[...])
    carry_ref[...] = c_nat[-1:, :]
    q_all = (_dot(h_in, wf_ref[:, :FOX_WIDTH]) * (HEAD_DIM**-0.5 * LOG2E)).astype(BF16)
    k_all = _dot(h_in, wf_ref[:, FOX_WIDTH : 2 * FOX_WIDTH]).astype(BF16)
    v_all = _dot(h_in, wf_ref[:, 2 * FOX_WIDTH :]).astype(BF16)
    ccol = c_nat * LOG2E
    terms = jnp.concatenate(_split3(ccol), axis=1).astype(BF16)
    placed = _dot(terms, place_ref[...])

    lane = lax.broadcasted_iota(jnp.int32, (ts, LANES), 1)
    stat_lane = lax.broadcasted_iota(jnp.int32, (1, LANES), 1)
    same_head = (lax.broadcasted_iota(jnp.int32, (LANES, LANES), 0) < HEAD_DIM) == (
        lax.broadcasted_iota(jnp.int32, (LANES, LANES), 1) < HEAD_DIM)
    head_ones = jnp.where(same_head, 1.0, 0.0).astype(BF16)
    stats = []
    for pair in range(FOX_HEADS // 2):
        cols = slice(pair * LANES, (pair + 1) * LANES)
        gq, gk, gv = q_all[:, cols], k_all[:, cols], v_all[:, cols]
        norms = []
        for g in (gq, gk):
            g32 = g.astype(F32)
            sq = _dot((g32 * g32).astype(BF16), head_ones)
            norms.append(jnp.sqrt(jnp.max(sq, axis=0, keepdims=True)) * NORM_SAFETY)
        for e in range(2):
            h = 2 * pair + e
            data = (lane < HEAD_DIM) if e == 0 else (lane >= HEAD_DIM)
            el = lane - (HEAD_DIM if e == 0 else 0)
            extras = placed[:, h * LANES : (h + 1) * LANES]
            q_extra = jnp.where((el >= 3) & (el < 6), 1.0, extras)
            k_extra = jnp.where((el >= 0) & (el < 3), 1.0, extras)
            v_extra = jnp.where(el == (0 if e == 0 else HEAD_DIM - 1), 1.0, 0.0)
            q_ref[0, h] = jnp.where(data, gq, q_extra.astype(BF16))
            k_ref[0, h] = jnp.where(data, gk, k_extra.astype(BF16))
            vt_ref[0, h] = jnp.where(data, gv, v_extra.astype(BF16)).T
            first = e * HEAD_DIM
            q_norm = norms[0][:, first : first + 1]
            k_norm = norms[1][:, first : first + 1]
            c_first = ccol[0:1, h : h + 1]
            c_last = ccol[ts - 1 : ts, h : h + 1]
            stats.append(jnp.where(stat_lane == 0, q_norm, jnp.where(stat_lane == 1, k_norm, jnp.where(stat_lane == 2, c_first, c_last))))
    stats_ref[0, 0] = jnp.concatenate(stats, axis=0)
    lru_ref[...] = _dot(h_in, wl_ref[...])
    ret_ref[...] = _dot(h_in, wr_ref[...])


def _fox_placement():
    place = np.zeros((3 * LANES, FOX_HEADS * LANES), np.float32)
    for h in range(FOX_HEADS):
        first = h * LANES + (HEAD_DIM if h % 2 == 0 else 0)
        for t in range(3):
            place[t * LANES + h, first + t] = 1.0
            place[t * LANES + h, first + 3 + t] = -1.0
    return jnp.asarray(place, dtype=BF16)


def _inproj(x, g, wl, wf, wr, wff, bf, bsz):
    m, d = x.shape
    s = m // bsz
    ts = FOX_TK
    nt = s // ts
    place = _fox_placement()
    row = lambda n: pl.BlockSpec((ts, n), lambda b, i: (b * nt + i, 0))
    per_head = pl.BlockSpec((1, FOX_HEADS, ts, LANES), lambda b, i: (b, 0, i, 0))
    return pl.pallas_call(
        _inproj_kernel,
        grid=(bsz, nt),
        in_specs=[row(d), _full(g.shape), _full(wl.shape), _full(wf.shape), _full(wr.shape), _full(wff.shape),
                  _full(bf.shape), _full(place.shape)],
        out_specs=[
            row(wl.shape[1]), row(wr.shape[1]), per_head, per_head,
            pl.BlockSpec((1, FOX_HEADS, LANES, ts), lambda b, i: (b, 0, 0, i)),
            pl.BlockSpec((1, 1, FOX_HEADS, LANES), lambda b, i: (b, i, 0, 0)),
        ],
        out_shape=[
            jax.ShapeDtypeStruct((m, wl.shape[1]), F32),
            jax.ShapeDtypeStruct((m, wr.shape[1]), F32),
            jax.ShapeDtypeStruct((bsz, FOX_HEADS, s, LANES), BF16),
            jax.ShapeDtypeStruct((bsz, FOX_HEADS, s, LANES), BF16),
            jax.ShapeDtypeStruct((bsz, FOX_HEADS, LANES, s), BF16),
            jax.ShapeDtypeStruct((bsz, nt, FOX_HEADS, LANES), F32),
        ],
        scratch_shapes=[pltpu.VMEM((1, LANES), F32)],
        compiler_params=_params(("parallel", "arbitrary")),
        name="mixer_inproj",
    )(x, g, wl, wf, wr, wff, bf, place)


def _fox_kernel(qn_ref, kn_ref, cf_ref, cl_ref, q_ref, k_ref, vt_ref, o_ref, m_scr, acc_scr, s_scr, mx_scr):
    tq = q_ref.shape[2]
    tk = FOX_TK
    sub = tq // tk
    nk = k_ref.shape[2] // tk
    nq = nk // sub
    qi = pl.program_id(2)
    v_rows = (slice(0, FOX_ACC_ROWS), slice(LANES - FOX_ACC_ROWS, LANES))
    parts = [(i * (tq // FOX_PARTS), tq // FOX_PARTS) for i in range(FOX_PARTS)]

    def first_needed(h):
        head = (pl.program_id(0) * (FOX_HEADS // 2) + pl.program_id(1)) * 2 + h
        q_norm = qn_ref[head * nq + qi]
        c_first = cf_ref[head * nq + qi]
        floor = -q_norm * kn_ref[head * nk + qi * sub + sub - 1]

        def needed(kb):
            j = head * nk + jnp.maximum(kb, 0)
            dot_bound = q_norm * kn_ref[j]
            slack = SKIP_REL * (dot_bound - floor) + SKIP_REL_C * (jnp.abs(c_first) + jnp.abs(cl_ref[j]))
            return dot_bound + c_first - cl_ref[j] + slack >= floor - SKIP_LOG2

        kb = lax.while_loop(lambda kb: jnp.logical_and(kb >= 0, needed(kb)), lambda kb: kb - 1, qi * sub - 1)
        return kb + 1

    def logits(h, kb, slot, diag, part=None):
        start = pl.multiple_of(kb * tk, tk)
        q0, qn = part if part is not None else ((0 if diag is None else diag * tk), None)
        lanes = slice(q0, None if qn is None else q0 + qn)
        s = _dot_nt(k_ref[0, h, pl.ds(start, tk), :], q_ref[0, h, lanes, :])
        if diag is not None:
            key = lax.broadcasted_iota(jnp.int32, s.shape, 0)
            qry = lax.broadcasted_iota(jnp.int32, s.shape, 1)
            s = jnp.where(key <= qry, s, NEG_BIG)
        s_scr[slot, :, lanes] = s
        mx_scr[slot, :, lanes] = jnp.max(s, axis=0, keepdims=True)

    def accumulate(h, kb, slot, diag=None, part=None):
        start = pl.multiple_of(kb * tk, tk)
        q0, qn = part if part is not None else ((0 if diag is None else diag * tk), None)
        lanes = slice(q0, None if qn is None else q0 + qn)
        m_old = m_scr[h, :, lanes]
        m_new = jnp.maximum(m_old, mx_scr[slot, :, lanes])
        m_scr[h, :, lanes] = m_new
        p = jnp.exp2(s_scr[slot, :, lanes] - m_new).astype(BF16)
        vt = vt_ref[0, h, v_rows[h], pl.ds(start, tk)]
        acc_scr[h, :, lanes] = jnp.exp2(m_old - m_new) * acc_scr[h, :, lanes] + _dot(vt, p)

    m_scr[...] = jnp.full_like(m_scr, NEG_BIG)
    acc_scr[...] = jnp.zeros_like(acc_scr)
    own = qi * sub
    lo = jnp.minimum(first_needed(0), first_needed(1))
    logits(0, own, 0, 0)
    logits(1, own, 1, 0)
    accumulate(0, own, 0, 0)
    for u in range(1, sub):
        logits(0, own + u, 0, u)
        accumulate(1, own + u - 1, 1, u - 1)
        logits(1, own + u, 1, u)
        accumulate(0, own + u, 0, u)
    last_own = own + sub - 1

    @pl.when(lo >= own)
    def _():
        accumulate(1, last_own, 1, sub - 1)

    @pl.when(lo < own)
    def _():
        logits(0, lo, 0, None)
        accumulate(1, last_own, 1, sub - 1)

        def body(kb, carry):
            for part in parts:
                logits(1, kb, 1, None, part)
                accumulate(0, kb, 0, None, part)
            for part in parts:
                logits(0, kb + 1, 0, None, part)
                accumulate(1, kb, 1, None, part)
            return carry

        lax.fori_loop(lo, own - 1, body, 0)
        logits(1, own - 1, 1, None)
        accumulate(0, own - 1, 0)
        accumulate(1, own - 1, 1)

    even, odd = acc_scr[0], acc_scr[1]
    pad = FOX_ACC_ROWS - HEAD_DIM
    o = jnp.concatenate([even[:HEAD_DIM] / even[HEAD_DIM : HEAD_DIM + 1], odd[pad:] / odd[pad - 1 : pad]], axis=0)
    o_ref[...] = o.T.astype(o_ref.dtype)


def _fox_attention(q, k, vt, stats):
    bsz, nh, s, _ = q.shape
    tq = FOX_TQ
    sub = tq // FOX_TK
    nq = s // tq
    npair = nh // 2
    st = stats[..., :4].transpose(0, 2, 1, 3)
    per_q = lambda v: v.reshape(bsz, nh, nq, sub)
    scalars = (
        per_q(st[..., 0]).max(axis=-1),
        lax.cummax(st[..., 1], axis=2),
        per_q(st[..., 2])[..., 0],
        st[..., 3],
    )
    scalars = tuple(v.reshape(-1) for v in scalars)
    return pl.pallas_call(
        _fox_kernel,
        grid_spec=pltpu.PrefetchScalarGridSpec(
            num_scalar_prefetch=len(scalars),
            grid=(bsz, npair, nq),
            in_specs=[
                pl.BlockSpec((1, 2, tq, LANES), lambda b, p, i, *_: (b, p, i, 0)),
                pl.BlockSpec((1, 2, s, LANES), lambda b, p, i, *_: (b, p, 0, 0)),
                pl.BlockSpec((1, 2, LANES, s), lambda b, p, i, *_: (b, p, 0, 0)),
            ],
            out_specs=pl.BlockSpec((tq, LANES), lambda b, p, i, *_: (b * nq + i, p)),
            scratch_shapes=[
                pltpu.VMEM((2, 1, tq), F32),
                pltpu.VMEM((2, FOX_ACC_ROWS, tq), F32),
                pltpu.VMEM((2, FOX_TK, tq), F32),
                pltpu.VMEM((2, 1, tq), F32),
            ],
        ),
        out_shape=jax.ShapeDtypeStruct((bsz * s, nh * HEAD_DIM), BF16),
        compiler_params=_params(("parallel", "parallel", "arbitrary")),
        name="fox_attention",
    )(*scalars, q, k, vt)


def _lru_phases(xy_ref, cw_ref, cb_ref, wg_ref, bg_ref, lam_ref, o_ref, tail_scr, h_scr):
    w = LRU_WIDTH

    @pl.when(pl.program_id(1) == 0)
    def _():
        tail_scr[...] = jnp.zeros_like(tail_scr)
        h_scr[...] = jnp.zeros_like(h_scr)

    x = xy_ref[:, :w]
    tail = tail_scr[...]
    row8 = lax.broadcasted_iota(jnp.int32, (SUBLANES, w), 0)
    conv = x * cw_ref[CONV_WIDTH - 1 : CONV_WIDTH, :] + cb_ref[...]
    for j in range(1, CONV_WIDTH):
        xs = pltpu.roll(x, j, 0)
        head = jnp.where(row8 < j, pltpu.roll(tail, j, 0), xs[:SUBLANES])
        xs = jnp.concatenate([head, xs[SUBLANES:]], axis=0)
        conv = conv + xs * cw_ref[CONV_WIDTH - 1 - j : CONV_WIDTH - j, :]
    tail_scr[...] = x[-SUBLANES:]
    yield
    gates = jax.nn.sigmoid(_dot(conv.astype(BF16), wg_ref[...]) + bg_ref[...])
    yield
    r = gates[:, :w]
    i = gates[:, w:]
    log_a = -LRU_C * r * jax.nn.softplus(-lam_ref[...])
    a = jnp.exp(log_a)
    th = jnp.tanh(log_a)
    u = jnp.sqrt(-2.0 * th / (1.0 - th)) * (i * conv)
    yield
    h = _scan_rows(a, u, h_scr[...])
    h_scr[...] = h[-1:, :]
    yield
    o_ref[...] = (h * jax.nn.gelu(xy_ref[:, w:])).astype(o_ref.dtype)


def _rope_kernel(pos_ref, freq_ref, cos_ref, sin_ref):
    ang = pos_ref[...].astype(F32) * freq_ref[...]
    lane = lax.broadcasted_iota(jnp.int32, ang.shape, 1)
    first_half = (lane % HEAD_DIM) < HEAD_DIM // 2
    cos_ref[...] = jnp.cos(ang)
    s = jnp.sin(ang)
    sin_ref[...] = jnp.where(first_half, -s, s)


def _rope_tables(pos, freq):
    m = pos.shape[0]
    tm = ROW_TILE
    spec = pl.BlockSpec((tm, LANES), lambda i: (i, 0))
    return pl.pallas_call(
        _rope_kernel,
        grid=(m // tm,),
        in_specs=[spec, _full(freq.shape)],
        out_specs=[spec, spec],
        out_shape=[jax.ShapeDtypeStruct((m, LANES), F32)] * 2,
        compiler_params=_params(("parallel",)),
        name="rope_tables",
    )(pos, freq)


def _group_mean(z, avg):
    hi, mid, lo = _split3(z)
    return _dot(hi.astype(BF16), avg) + _dot(mid.astype(BF16), avg) + _dot(lo.astype(BF16), avg)


def _ret_phases(x_ref, cos_ref, sin_ref, decay_ref, kw_ref, qw_ref, cd_ref, avg_ref, o_ref, state_scr):
    c = RET_CHUNK
    w = RET_WIDTH
    npair = RET_HEADS // 2

    @pl.when(pl.program_id(1) == 0)
    def _():
        state_scr[...] = jnp.zeros_like(state_scr)

    cos = cos_ref[...]
    sin = sin_ref[...]
    lane = lax.broadcasted_iota(jnp.int32, cos.shape, 1)
    first_half = (lane % HEAD_DIM) < HEAD_DIM // 2

    def rotary(z):
        swapped = jnp.where(first_half, pltpu.roll(z, LANES - HEAD_DIM // 2, 1), pltpu.roll(z, HEAD_DIM // 2, 1))
        return z * cos + swapped * sin

    ts = x_ref.shape[0]
    nchunk = ts // c
    row_chunk = lax.broadcasted_iota(jnp.int32, (ts, LANES), 0) // c
    ys = []
    for p in range(npair):
        cols = slice(p * LANES, (p + 1) * LANES)
        q = rotary(x_ref[:, cols])
        k = rotary(x_ref[:, w + p * LANES : w + (p + 1) * LANES]) * HEAD_DIM**-0.5
        v = x_ref[:, 2 * w + p * LANES : 2 * w + (p + 1) * LANES].astype(BF16)
        kb = k.astype(BF16)
        q_cross = (q * qw_ref[:, cols]).astype(BF16)
        k_state = (k * kw_ref[:, cols]).astype(BF16)
        zero = jnp.zeros_like(v)
        v_blocks = jnp.concatenate([jnp.where(row_chunk == n, v, zero) for n in range(nchunk)], axis=1)
        yield
        incs = _dot_tn(k_state, v_blocks)
        yield
        cd = cd_ref[p]
        states = [state_scr[p]]
        for n in range(nchunk):
            states.append(cd * states[-1] + jnp.where(cd > 0.0, incs[:, n * LANES : (n + 1) * LANES], 0.0))
        state_scr[p] = states[-1]
        q_blocks = jnp.concatenate([jnp.where(row_chunk == n, q_cross, zero) for n in range(nchunk)], axis=1)
        yield
        y = _dot(q_blocks, jnp.concatenate([st.astype(BF16) for st in states[:nchunk]], axis=0))
        yield
        span = RET_GROUP * c
        span_low = lax.broadcasted_iota(jnp.int32, (span, LANES), 1) < HEAD_DIM
        span_head = (span_low, jnp.logical_not(span_low))
        inner = []
        for r0 in range(0, ts, span):
            r = slice(r0, r0 + span)
            per_head = []
            for h in range(2):
                q_head = jnp.where(span_head[h], q[r], 0.0).astype(BF16)
                scores = _dot_nt(q_head, kb[r]) * decay_ref[2 * p + h]
                per_head.append(_dot(scores.astype(BF16), v[r]))
            inner.append(jnp.where(span_low, per_head[0], per_head[1]))
            yield
        ys.append(y + jnp.concatenate(inner, axis=0))
    yield
    y = jnp.concatenate(ys, axis=1)
    avg = avg_ref[...]
    mu = _group_mean(y, avg)
    d = y - mu
    var = _group_mean(d * d, avg)
    yn = d * lax.rsqrt(var + EPS)
    g = x_ref[:, 3 * w :]
    o_ref[...] = (g * jax.nn.sigmoid(g) * yn).astype(o_ref.dtype)


def _interleave(gens):
    live = list(gens)
    started = 0
    while live:
        started = min(started + 1, len(gens))
        for gen in gens[:started]:
            if gen in live:
                try:
                    next(gen)
                except StopIteration:
                    live.remove(gen)


def _recurrent_kernel(xy_ref, cw_ref, cb_ref, wg_ref, bg_ref, lam_ref, x_ref, cos_ref, sin_ref, decay_ref, kw_ref,
                      qw_ref, cd_ref, avg_ref, lru_o_ref, ret_o_ref, tail_scr, h_scr, state_scr):
    _interleave([
        _ret_phases(x_ref, cos_ref, sin_ref, decay_ref, kw_ref, qw_ref, cd_ref, avg_ref, ret_o_ref, state_scr),
        _lru_phases(xy_ref, cw_ref, cb_ref, wg_ref, bg_ref, lam_ref, lru_o_ref, tail_scr, h_scr),
    ])


def _recurrent_mixers(xy, cw, cb, wg, bg, lam, ret, cos, sin, decay, kw, qw, cd, avg, bsz):
    m = ret.shape[0]
    ts = ROW_TILE
    nt = m // bsz // ts
    row = lambda n: pl.BlockSpec((ts, n), lambda b, i: (b * nt + i, 0))
    consts = (cw, cb, wg, bg, lam)
    tables = (decay, kw, qw, cd, avg)
    return pl.pallas_call(
        _recurrent_kernel,
        grid=(bsz, nt),
        in_specs=[row(2 * LRU_WIDTH)] + [_full(c.shape) for c in consts]
        + [row(4 * RET_WIDTH), row(LANES), row(LANES)] + [_full(t.shape) for t in tables],
        out_specs=[row(LRU_WIDTH), row(RET_WIDTH)],
        out_shape=[jax.ShapeDtypeStruct((m, LRU_WIDTH), BF16), jax.ShapeDtypeStruct((m, RET_WIDTH), BF16)],
        scratch_shapes=[
            pltpu.VMEM((SUBLANES, LRU_WIDTH), F32),
            pltpu.VMEM((1, LRU_WIDTH), F32),
            pltpu.VMEM((RET_HEADS // 2, LANES, LANES), F32),
        ],
        compiler_params=_params(("parallel", "arbitrary")),
        name="recurrent_mixers",
    )(xy, *consts, ret, cos, sin, *tables)


def _norm_matmul_kernel(x_ref, g_ref, w_ref, o_ref):
    o_ref[...] = _dot(_rms(x_ref[...], g_ref[...]).astype(BF16), w_ref[...]).astype(o_ref.dtype)


def _norm_matmul(x, g, w, tm):
    m, d = x.shape
    n = w.shape[1]
    return pl.pallas_call(
        _norm_matmul_kernel,
        grid=(m // tm,),
        in_specs=[pl.BlockSpec((tm, d), lambda i: (i, 0)), _full(g.shape), _full(w.shape)],
        out_specs=pl.BlockSpec((tm, n), lambda i: (i, 0)),
        out_shape=jax.ShapeDtypeStruct((m, n), BF16),
        compiler_params=_params(("parallel",)),
        name="memory_kv_proj",
    )(x, g, w)


def _tail_kernel(x_ref, lru_ref, fox_ref, ret_ref, wm_ref, gm_ref, g1_ref, wq_ref, kt_ref, v_ref, wo_ref, g2_ref,
                 g3_ref, wg_ref, wu_ref, wd_ref, g4_ref, o_ref):
    d = x_ref.shape[1]
    hd = d // CROSS_HEADS
    a, b = LRU_WIDTH, LRU_WIDTH + FOX_WIDTH
    dff = wg_ref.shape[1]
    bounds = [(lo, min(lo + FFN_CHUNK, dff)) for lo in range(0, dff, FFN_CHUNK)]
    head_cols = [slice(h * hd, (h + 1) * hd) for h in range(CROSS_HEADS)]

    def phases(rows):
        mix = _dot(lru_ref[rows, :], wm_ref[:a, :]) + _dot(fox_ref[rows, :], wm_ref[a:b, :]) + _dot(ret_ref[rows, :], wm_ref[b:, :])
        yield
        x = x_ref[rows, :] + _rms(mix, gm_ref[...])
        h = _rms(x, g1_ref[...]).astype(BF16)
        yield
        q = _dot(h, wq_ref[...]).astype(BF16)
        yield
        logits = [_dot(q[:, cols], kt_ref[0, cols, :]) * hd**-0.5 for cols in head_cols]
        yield
        outs = []
        for cols, s in zip(head_cols, logits):
            e = jnp.exp(s - jnp.max(s, axis=-1, keepdims=True))
            p = e / jnp.sum(e, axis=-1, keepdims=True)
            outs.append(_dot(p.astype(BF16), v_ref[0, :, cols]).astype(BF16))
        yield
        attn = _dot(jnp.concatenate(outs, axis=1), wo_ref[...])
        yield
        x = x + _rms(attn, g2_ref[...])
        h = _rms(x, g3_ref[...]).astype(BF16)
        yield
        y = None
        for lo, hi in bounds:
            gate = _dot(h, wg_ref[:, lo:hi])
            up = _dot(h, wu_ref[:, lo:hi])
            yield
            act = (gate * jax.nn.sigmoid(gate) * up).astype(BF16)
            part = _dot(act, wd_ref[lo:hi, :])
            y = part if y is None else y + part
            yield
        o_ref[rows, :] = x + _rms(y, g4_ref[...])

    tm = x_ref.shape[0]
    step = tm // TAIL_SPLIT
    gens = [phases(slice(g * step, (g + 1) * step)) for g in range(TAIL_SPLIT)]
    live = list(gens)
    started = 0
    while live:
        if started < len(gens):
            started += 1
        for gen in list(gens[:started]):
            if gen in live:
                try:
                    next(gen)
                except StopIteration:
                    live.remove(gen)


def _layer_tail(x, lru, fox, ret, wm, gm, g1, wq, kt, v, wo, g2, g3, wg, wu, wd, g4, bsz):
    m, d = x.shape
    tm = ROW_TILE
    nt = m // bsz // tm
    mem_len = v.shape[1]
    row = lambda n: pl.BlockSpec((tm, n), lambda i: (i, 0))
    resident = lambda arr: pl.BlockSpec(arr.shape, lambda i: (0,) * arr.ndim, pipeline_mode=pl.Buffered(1))
    return pl.pallas_call(
        _tail_kernel,
        grid=(m // tm,),
        in_specs=[
            row(d), row(lru.shape[1]), row(fox.shape[1]), row(ret.shape[1]),
            resident(wm), resident(gm), resident(g1), resident(wq),
            pl.BlockSpec((1, d, mem_len), lambda i: (i // nt, 0, 0)),
            pl.BlockSpec((1, mem_len, d), lambda i: (i // nt, 0, 0)),
            resident(wo), resident(g2), resident(g3), resident(wg), resident(wu), resident(wd), resident(g4),
        ],
        out_specs=row(d),
        out_shape=jax.ShapeDtypeStruct((m, d), F32),
        compiler_params=pltpu.CompilerParams(dimension_semantics=("parallel",), vmem_limit_bytes=TAIL_VMEM_LIMIT),
        name="layer_tail",
    )(x, lru, fox, ret, wm, gm, g1, wq, kt, v, wo, g2, g3, wg, wu, wd, g4)


def _block_diag(w):
    n, a, b = w.shape
    eye = jnp.eye(n, dtype=w.dtype)
    return (eye[:, None, :, None] * w[:, :, None, :]).reshape(n * a, n * b)


def _retention_tables():
    c = RET_CHUNK
    f32 = np.float32
    log_gamma = np.log1p(-np.exp2(-5.0 - np.arange(RET_HEADS, dtype=f32))).astype(f32)
    idx = np.arange(c, dtype=f32)
    diff = idx[:, None] - idx[None, :]
    decay = np.where(diff >= 0, np.exp(log_gamma[:, None, None] * np.maximum(diff, f32(0))), f32(0)).astype(f32)
    decay = np.stack([np.kron(np.eye(RET_GROUP, dtype=f32), decay[h]) for h in range(RET_HEADS)])
    k_w = np.exp(log_gamma[None, :] * (c - 1.0 - idx)[:, None]).astype(f32)
    q_w = np.exp(log_gamma[None, :] * (idx + 1.0)[:, None]).astype(f32)
    chunk_decay = np.exp(log_gamma * f32(c)).astype(f32)
    reps = ROW_TILE // c
    kw = np.tile(np.repeat(k_w, HEAD_DIM, axis=1), (reps, 1))
    qw = np.tile(np.repeat(q_w, HEAD_DIM, axis=1), (reps, 1))
    ones = np.ones((HEAD_DIM, HEAD_DIM), f32)
    cd = np.stack([
        np.kron(np.diag(chunk_decay[2 * p : 2 * p + 2]), ones) for p in range(RET_HEADS // 2)
    ])
    avg = np.kron(np.eye(RET_HEADS, dtype=f32), np.full((HEAD_DIM, HEAD_DIM), 1.0 / HEAD_DIM, f32))
    return jnp.asarray(decay), jnp.asarray(kw), jnp.asarray(qw), jnp.asarray(cd), jnp.asarray(avg, dtype=BF16)


def kernel(x, mem, positions, pre_mix_g, post_mix_g, w_in, conv_w, conv_b, w_rg, b_rg, w_ig, b_ig, lru_lambda, fox_b_f, w_out, pre_cross_g, post_cross_g, mem_norm_g, w_cq, w_ck, w_cv, w_co, pre_ffn_g, post_ffn_g, w_gate, w_up, w_down):
    bsz, seq, d = x.shape
    depth = w_in.shape[0]
    m = bsz * seq
    mem_len = mem.shape[1]
    xf = x.reshape(m, d)
    row = lambda v: v.reshape(1, -1)

    half = HEAD_DIM // 2
    inv_freq = ROPE_THETA ** (-jnp.arange(half, dtype=F32) / half)
    freq = jnp.tile(inv_freq, LANES // half).reshape(1, LANES)
    pos = jnp.broadcast_to(positions.reshape(m, 1), (m, LANES))
    cos, sin = _rope_tables(pos, freq)
    decay, kw, qw, cd, avg = _retention_tables()

    o_fox = 2 * LRU_WIDTH
    o_ff = o_fox + 3 * FOX_WIDTH
    o_ret = o_ff + FOX_HEADS

    for l in range(depth):
        order = jnp.argsort(fox_b_f[l])
        by_head = lambda w: w.reshape(d, FOX_HEADS, HEAD_DIM)[:, order].reshape(d, FOX_WIDTH)
        wl = w_in[l, :, :o_fox].astype(BF16)
        wf = jnp.concatenate(
            [by_head(w_in[l, :, o_fox + i * FOX_WIDTH : o_fox + (i + 1) * FOX_WIDTH]) for i in range(3)], axis=1
        ).astype(BF16)
        wff = jnp.pad(w_in[l, :, o_ff:o_ret][:, order], ((0, 0), (0, LANES - FOX_HEADS))).astype(BF16)
        wr = w_in[l, :, o_ret:].astype(BF16)
        bf = jnp.pad(fox_b_f[l][order], (0, LANES - FOX_HEADS)).reshape(1, LANES)
        fox_rows = w_out[l, LRU_WIDTH : LRU_WIDTH + FOX_WIDTH].reshape(FOX_HEADS, HEAD_DIM, d)[order].reshape(FOX_WIDTH, d)
        wm = jnp.concatenate([w_out[l, :LRU_WIDTH], fox_rows, w_out[l, LRU_WIDTH + FOX_WIDTH :]], axis=0).astype(BF16)
        lru, ret, *fox_operands = _inproj(xf, row(pre_mix_g[l]), wl, wf, wr, wff, bf, bsz)
        fox_o = _fox_attention(*fox_operands)

        wg = jnp.concatenate([_block_diag(w_rg[l]), _block_diag(w_ig[l])], axis=1).astype(BF16)
        bg = jnp.concatenate([b_rg[l], b_ig[l]]).reshape(1, -1)
        lru_o, ret_o = _recurrent_mixers(
            lru, conv_w[l], row(conv_b[l]), wg, bg, row(lru_lambda[l]), ret, cos, sin, decay, kw, qw, cd, avg, bsz)

        wkv = jnp.concatenate([w_ck[l], w_cv[l]], axis=1).astype(BF16)
        kv = _norm_matmul(mem.reshape(bsz * mem_len, d), row(mem_norm_g), wkv, mem_len)
        kt = kv[:, :d].reshape(bsz, mem_len, d).transpose(0, 2, 1)
        vv = kv[:, d:].reshape(bsz, mem_len, d)
        xf = _layer_tail(
            xf, lru_o, fox_o, ret_o, wm, row(post_mix_g[l]),
            row(pre_cross_g[l]), w_cq[l].astype(BF16), kt, vv, w_co[l].astype(BF16), row(post_cross_g[l]),
            row(pre_ffn_g[l]), w_gate[l].astype(BF16), w_up[l].astype(BF16), w_down[l].astype(BF16), row(post_ffn_g[l]), bsz)
    return xf.reshape(bsz, seq, d)
```
